```python
import jax, jax.numpy as jnp
from jax import lax
import numpy as np

D_MODEL = 1024
BATCH = 32
SEQ = 256
DEPTH = 4
DEC_BATCH = 4
DEC_SEQ = 1024
PAST_LEN = 512

GRID_W = 64
N_HEADS = 8
N_KV_HEADS = 2
HEAD_DIM = 64
GQA_GROUP = N_HEADS // N_KV_HEADS
AXIS_ROPE_DIM = HEAD_DIM // 2
ROPE_THETA = 10000.0
Q_BLOCK = 128
ATT_Q = N_HEADS * HEAD_DIM
ATT_KV = N_KV_HEADS * HEAD_DIM
CONV_CH = 512
CONV_K = 31
FOURIER_GROUPS = 4
FOURIER_GROUP_CH = 128
FOURIER_CH = FOURIER_GROUPS * FOURIER_GROUP_CH
SGU_GROUPS = 4
SGU_GROUP_CH = 128
SGU_CH = SGU_GROUPS * SGU_GROUP_CH
CHUNK = 128
FFN_DIM = 2816
N_SUBLAYERS = 3
N_MOD = 3 * N_SUBLAYERS
EVEN_IN = ATT_Q + 2 * ATT_KV + 2 * CONV_CH
EVEN_OUT = ATT_Q + CONV_CH
ODD_IN = FOURIER_CH + 2 * SGU_CH
ODD_OUT = FOURIER_CH + SGU_CH
N_EVEN = (DEPTH + 1) // 2
N_ODD = DEPTH // 2
RMS_EPS = 1e-6
LN_EPS = 1e-5

kernel_name = 'hybrid_diffusion_prefix_trunk_step'


def rms_norm(x, g):
    xf = x.astype(jnp.float32)
    y = xf * lax.rsqrt(jnp.mean(xf * xf, axis=-1, keepdims=True) + RMS_EPS)
    return (y * g.astype(jnp.float32)).astype(x.dtype)


def layer_norm(x, g, b):
    xf = x.astype(jnp.float32)
    mu = jnp.mean(xf, axis=-1, keepdims=True)
    xc = xf - mu
    var = jnp.mean(xc * xc, axis=-1, keepdims=True)
    return (xc * lax.rsqrt(var + LN_EPS) * g.astype(jnp.float32) + b.astype(jnp.float32)).astype(x.dtype)


def adaln(cond, w, b):
    return (jax.nn.silu(cond) @ w + b).reshape(cond.shape[0], N_MOD, D_MODEL)


def modulate(x, g, shift, scale):
    return rms_norm(x, g) * (1 + scale) + shift


def swiglu(h, w_in, w_out):
    gate, up = jnp.split(h @ w_in, 2, axis=-1)
    return (jax.nn.silu(gate) * up) @ w_out


def rope_tables(rows):
    t = jnp.arange(rows * GRID_W)
    row = (t // GRID_W).astype(jnp.float32)
    col = (t % GRID_W).astype(jnp.float32)
    inv_freq = 1.0 / (ROPE_THETA ** (jnp.arange(0, AXIS_ROPE_DIM, 2, dtype=jnp.float32) / AXIS_ROPE_DIM))
    ang = jnp.concatenate([row[:, None] * inv_freq, col[:, None] * inv_freq], axis=-1)
    return jnp.cos(ang), jnp.sin(ang)


def apply_rope(x, cos, sin):
    xf = x.astype(jnp.float32).reshape(*x.shape[:-1], HEAD_DIM // 2, 2)
    x0, x1 = xf[..., 0], xf[..., 1]
    cs = cos[None, :, None, :]
    sn = sin[None, :, None, :]
    out = jnp.stack([x0 * cs - x1 * sn, x0 * sn + x1 * cs], axis=-1)
    return out.reshape(x.shape).astype(x.dtype)


def block_attention(q, k, v):
    bsz, lq = q.shape[0], q.shape[1]
    nb = lq // Q_BLOCK
    qb = q.reshape(bsz, nb, Q_BLOCK, N_KV_HEADS, GQA_GROUP, HEAD_DIM).swapaxes(0, 1)
    scale = HEAD_DIM ** -0.5

    def one_block(qblk):
        s = jnp.einsum('bqkgd,bskd->bkgqs', qblk, k).astype(jnp.float32) * scale
        p = jax.nn.softmax(s, axis=-1).astype(v.dtype)
        return jnp.einsum('bkgqs,bskd->bqkgd', p, v)

    out = lax.map(one_block, qb)
    return out.swapaxes(0, 1).reshape(bsz, lq, ATT_Q)


def depthwise_conv(x, w, b):
    y = lax.conv_general_dilated(
        x, w[:, None, :], window_strides=(1,), padding=[(CONV_K // 2, CONV_K // 2)],
        dimension_numbers=('NWC', 'WIO', 'NWC'), feature_group_count=x.shape[-1])
    return y + b


def even_mixer(h, w_in, w_out, q_g, k_g, conv_w, conv_b, cn_g, cn_b, rope, ctx_kv):
    bsz, length, _ = h.shape
    q, k, v, a = jnp.split(h @ w_in, [ATT_Q, ATT_Q + ATT_KV, ATT_Q + 2 * ATT_KV], axis=-1)
    q = rms_norm(q.reshape(bsz, length, N_HEADS, HEAD_DIM), q_g)
    k = rms_norm(k.reshape(bsz, length, N_KV_HEADS, HEAD_DIM), k_g)
    v = v.reshape(bsz, length, N_KV_HEADS, HEAD_DIM)
    if ctx_kv is None:
        own_kv = (k, v)
        attn = block_attention(q, k, v)
    else:
        cos, sin = rope
        ck, cv = ctx_kv
        attn = block_attention(apply_rope(q, cos, sin),
                               jnp.concatenate([ck, apply_rope(k, cos, sin)], axis=1),
                               jnp.concatenate([cv, v], axis=1))
        own_kv = None
    glu = a[..., :CONV_CH] * jax.nn.sigmoid(a[..., CONV_CH:])
    conv = jax.nn.silu(layer_norm(depthwise_conv(glu, conv_w, conv_b), cn_g, cn_b))
    return jnp.concatenate([attn, conv], axis=-1) @ w_out, own_kv


def odd_mixer(h, w_in, w_out, sn_g, sn_b, w_s, b_s):
    bsz, length, _ = h.shape
    f, u, v = jnp.split(h @ w_in, [FOURIER_CH, FOURIER_CH + SGU_CH], axis=-1)
    f = f.reshape(bsz, length, FOURIER_GROUPS, FOURIER_GROUP_CH).astype(jnp.float32)
    four = jnp.fft.fft2(f, axes=(1, 3), norm='ortho').real.astype(h.dtype).reshape(bsz, length, FOURIER_CH)
    v = layer_norm(v, sn_g, sn_b).reshape(bsz, length // CHUNK, CHUNK, SGU_GROUPS, SGU_GROUP_CH)
    v = jnp.einsum('gpq,bnqgc->bnpgc', w_s, v) + jnp.swapaxes(b_s, 0, 1)[:, :, None]
    sgu = u * v.reshape(bsz, length, SGU_CH)
    return jnp.concatenate([four, sgu], axis=-1) @ w_out, None


def macaron_layer(x, mod, norm_g, ffn_in, ffn_out, mixer):
    m = mod[:, :, None, :]
    x = x + 0.5 * m[:, 2] * swiglu(modulate(x, norm_g[0], m[:, 0], m[:, 1]), ffn_in[0], ffn_out[0])
    mixed, own_kv = mixer(modulate(x, norm_g[1], m[:, 3], m[:, 4]))
    x = x + m[:, 5] * mixed
    x = x + 0.5 * m[:, 8] * swiglu(modulate(x, norm_g[2], m[:, 6], m[:, 7]), ffn_in[1], ffn_out[1])
    return x, own_kv


def run_stream(x, cond, rope, ctx_k, ctx_v, w_mod, b_mod, norm_g, ffn_w_in, ffn_w_out,
               ev_w_in, ev_w_out, q_norm_g, k_norm_g, conv_w, conv_b, conv_norm_g, conv_norm_b,
               od_w_in, od_w_out, sgu_norm_g, sgu_norm_b, sgu_w, sgu_b):
    ks, vs = [], []
    for layer in range(DEPTH):
        mod = adaln(cond, w_mod[layer], b_mod[layer])
        j = layer // 2
        if layer % 2 == 0:
            ctx_kv = None if ctx_k is None else (ctx_k[:, j], ctx_v[:, j])

            def mixer(h):
                return even_mixer(h, ev_w_in[j], ev_w_out[j], q_norm_g[j], k_norm_g[j], conv_w[j],
                                  conv_b[j], conv_norm_g[j], conv_norm_b[j], rope, ctx_kv)
        else:
            def mixer(h):
                return odd_mixer(h, od_w_in[j], od_w_out[j], sgu_norm_g[j], sgu_norm_b[j],
                                 sgu_w[j], sgu_b[j])
        x, own_kv = macaron_layer(x, mod, norm_g[layer], ffn_w_in[layer], ffn_w_out[layer], mixer)
        if own_kv is not None:
            ks.append(own_kv[0])
            vs.append(own_kv[1])
    return x, ks, vs


def setup_inputs(seed: int = 0) -> dict:
    key = jax.random.key(seed)
    ks = jax.random.split(key, 25)

    def nrm(k, shape, s):
        return jax.random.normal(k, shape, jnp.float32) * s

    return {
        'x_prompt': nrm(ks[0], (BATCH, SEQ, D_MODEL), 1.0),
        'x_sample': nrm(ks[1], (DEC_BATCH, DEC_SEQ, D_MODEL), 1.0),
        'cache_k': nrm(ks[2], (DEC_BATCH, N_EVEN, PAST_LEN, N_KV_HEADS, HEAD_DIM), 1.0),
        'cache_v': nrm(ks[3], (DEC_BATCH, N_EVEN, PAST_LEN, N_KV_HEADS, HEAD_DIM), 1.0),
        'c': nrm(ks[4], (DEC_BATCH, D_MODEL), 1.0),
        'c_ctx': nrm(ks[5], (D_MODEL,), 1.0),
        'w_mod': nrm(ks[6], (DEPTH, D_MODEL, N_MOD * D_MODEL), 0.3 * D_MODEL ** -0.5),
        'b_mod': nrm(ks[7], (DEPTH, N_MOD * D_MODEL), 0.01),
        'norm_g': 1.0 + nrm(ks[8], (DEPTH, N_SUBLAYERS, D_MODEL), 0.05),
        'ffn_w_in': nrm(ks[9], (DEPTH, 2, D_MODEL, 2 * FFN_DIM), D_MODEL ** -0.5),
        'ffn_w_out': nrm(ks[10], (DEPTH, 2, FFN_DIM, D_MODEL), FFN_DIM ** -0.5),
        'ev_w_in': nrm(ks[11], (N_EVEN, D_MODEL, EVEN_IN), D_MODEL ** -0.5),
        'ev_w_out': nrm(ks[12], (N_EVEN, EVEN_OUT, D_MODEL), EVEN_OUT ** -0.5),
        'q_norm_g': 1.0 + nrm(ks[13], (N_EVEN, HEAD_DIM), 0.05),
        'k_norm_g': 1.0 + nrm(ks[14], (N_EVEN, HEAD_DIM), 0.05),
        'conv_w': nrm(ks[15], (N_EVEN, CONV_K, CONV_CH), CONV_K ** -0.5),
        'conv_b': nrm(ks[16], (N_EVEN, CONV_CH), 0.01),
        'conv_norm_g': 1.0 + nrm(ks[17], (N_EVEN, CONV_CH), 0.05),
        'conv_norm_b': nrm(ks[18], (N_EVEN, CONV_CH), 0.01),
        'od_w_in': nrm(ks[19], (N_ODD, D_MODEL, ODD_IN), D_MODEL ** -0.5),
        'od_w_out': nrm(ks[20], (N_ODD, ODD_OUT, D_MODEL), ODD_OUT ** -0.5),
        'sgu_norm_g': 1.0 + nrm(ks[21], (N_ODD, SGU_CH), 0.05),
        'sgu_norm_b': nrm(ks[22], (N_ODD, SGU_CH), 0.01),
        'sgu_w': nrm(ks[23], (N_ODD, SGU_GROUPS, CHUNK, CHUNK), CHUNK ** -0.5),
        'sgu_b': 1.0 + nrm(ks[24], (N_ODD, SGU_GROUPS, CHUNK), 0.01),
    }


def reference(x_prompt, x_sample, cache_k, cache_v, c, c_ctx, w_mod, b_mod, norm_g, ffn_w_in,
              ffn_w_out, ev_w_in, ev_w_out, q_norm_g, k_norm_g, conv_w, conv_b, conv_norm_g,
              conv_norm_b, od_w_in, od_w_out, sgu_norm_g, sgu_norm_b, sgu_w, sgu_b):
    weights = (w_mod, b_mod, norm_g, ffn_w_in, ffn_w_out, ev_w_in, ev_w_out, q_norm_g, k_norm_g,
               conv_w, conv_b, conv_norm_g, conv_norm_b, od_w_in, od_w_out, sgu_norm_g,
               sgu_norm_b, sgu_w, sgu_b)
    y_prompt, ctx_ks, ctx_vs = run_stream(x_prompt, c_ctx[None, :], None, None, None, *weights)
    new_cache_k = jnp.stack(ctx_ks, axis=1)
    new_cache_v = jnp.stack(ctx_vs, axis=1)
    rows = x_sample.shape[1] // GRID_W
    rope = rope_tables(rows)
    y_sample, _, _ = run_stream(x_sample, c, rope, cache_k, cache_v, *weights)
    return (y_prompt, y_sample, new_cache_k, new_cache_v)
```

```python
import functools

import numpy as np
import jax
import jax.numpy as jnp
from jax import lax
from jax.experimental import pallas as pl
from jax.experimental.pallas import tpu as pltpu

D_MODEL = 1024
BATCH = 32
SEQ = 256
DEPTH = 4
DEC_BATCH = 4
DEC_SEQ = 1024
PAST_LEN = 512
GRID_W = 64
N_HEADS = 8
N_KV_HEADS = 2
HEAD_DIM = 64
GQA_GROUP = N_HEADS // N_KV_HEADS
AXIS_ROPE_DIM = HEAD_DIM // 2
ROPE_THETA = 10000.0
ATT_Q = N_HEADS * HEAD_DIM
ATT_KV = N_KV_HEADS * HEAD_DIM
CONV_CH = 512
CONV_K = 31
FOURIER_GROUPS = 4
FOURIER_GROUP_CH = 128
FOURIER_CH = FOURIER_GROUPS * FOURIER_GROUP_CH
SGU_GROUPS = 4
SGU_GROUP_CH = 128
SGU_CH = SGU_GROUPS * SGU_GROUP_CH
CHUNK = 128
FFN_DIM = 2816
N_MOD = 9
EVEN_IN = ATT_Q + 2 * ATT_KV + 2 * CONV_CH
ODD_IN = FOURIER_CH + 2 * SGU_CH
RMS_EPS = 1e-6
LN_EPS = 1e-5

BF16 = jnp.bfloat16
F32 = jnp.float32

V7X_SUBLANES = 8
V7X_LANES = 128
NTOK_P = BATCH * SEQ
NTOK_S = DEC_BATCH * DEC_SEQ
NTOK = NTOK_P + NTOK_S
TM = DEC_SEQ
N_TILES = NTOK // TM
N_TILES_P = NTOK_P // TM
SEQ_PER_TILE = TM // SEQ
N_COND = 8
COND_LATENT0 = 2
FFN_TM = 2 * TM
FFN_CHUNK = 256
N_FFN_CHUNKS = FFN_DIM // FFN_CHUNK
EVEN_IN_ROWS = 512
ADALN_TN = 1024
CONV_ROWS = 128
CONV_HALO = 16
ATT_TQ = SEQ
V7X_VMEM_LIMIT_BYTES = 56 * 1024 * 1024


def _cparams(*sem):
    return pltpu.CompilerParams(dimension_semantics=sem, vmem_limit_bytes=V7X_VMEM_LIMIT_BYTES)


def _cond_of_tile(i, rows=TM):
    return jnp.maximum((i * rows - NTOK_P) // DEC_SEQ + COND_LATENT0, 0)


def _dot(a, b):
    return jnp.dot(a, b, preferred_element_type=F32)


def _modulate(x, g, shift, scale):
    ms = jnp.mean(x * x, axis=-1, keepdims=True)
    return (x * lax.rsqrt(ms + RMS_EPS) * g) * (1.0 + scale) + shift


def _layer_norm(x, g, b):
    mu = jnp.mean(x, axis=-1, keepdims=True)
    xc = x - mu
    var = jnp.mean(xc * xc, axis=-1, keepdims=True)
    return xc * lax.rsqrt(var + LN_EPS) * g + b


def _adaln_kernel(c_ref, w_ref, b_ref, o_ref):
    c = c_ref[...]
    s = (c * jax.nn.sigmoid(c)).astype(BF16)
    o_ref[...] = _dot(s, w_ref[...].astype(BF16)) + b_ref[...]


def _adaln(cond, w_mod, b_mod):
    n_out = N_MOD * D_MODEL
    return pl.pallas_call(
        _adaln_kernel,
        grid=(DEPTH, n_out // ADALN_TN),
        in_specs=[
            pl.BlockSpec((N_COND, D_MODEL), lambda l, n: (0, 0)),
            pl.BlockSpec((None, D_MODEL, ADALN_TN), lambda l, n: (l, 0, n)),
            pl.BlockSpec((None, 1, ADALN_TN), lambda l, n: (l, 0, n)),
        ],
        out_specs=pl.BlockSpec((None, N_COND, ADALN_TN), lambda l, n: (l, 0, n)),
        out_shape=jax.ShapeDtypeStruct((DEPTH, N_COND, n_out), F32),
        compiler_params=_cparams("parallel", "parallel"),
        name="adaln",
    )(cond, w_mod, b_mod.reshape(DEPTH, 1, n_out))


def _ffn_kernel(sub, x_ref, mod_ref, g_ref, wg_ref, wu_ref, wo_ref, o_ref, h_scr):
    j = pl.program_id(1)
    halves = [(c, slice(c * TM, (c + 1) * TM)) for c in range(FFN_TM // TM)]

    @pl.when(j == 0)
    def _():
        for c, rows in halves:
            h = _modulate(x_ref[rows, :], g_ref[...], mod_ref[c, 3 * sub:3 * sub + 1, :],
                          mod_ref[c, 3 * sub + 1:3 * sub + 2, :])
            h_scr[rows, :] = h.astype(BF16)
        o_ref[...] = jnp.zeros_like(o_ref)

    h = h_scr[...]
    gate = _dot(h, wg_ref[...].astype(BF16))
    up = _dot(h, wu_ref[...].astype(BF16))
    a = (gate * jax.nn.sigmoid(gate) * up).astype(BF16)
    o_ref[...] += _dot(a, wo_ref[...].astype(BF16))

    @pl.when(j == N_FFN_CHUNKS - 1)
    def _():
        for c, rows in halves:
            o_ref[rows, :] = x_ref[rows, :] + (0.5 * mod_ref[c, 3 * sub + 2:3 * sub + 3, :]) * o_ref[rows, :]


def _ffn(x, mod_l, norm_g, ffn_w_in, ffn_w_out, layer, which):
    sub = 2 * which
    per = FFN_TM // TM
    return pl.pallas_call(
        functools.partial(_ffn_kernel, sub),
        grid=(NTOK // FFN_TM, N_FFN_CHUNKS),
        in_specs=[
            pl.BlockSpec((FFN_TM, D_MODEL), lambda i, j: (i, 0)),
            pl.BlockSpec((per, N_MOD, D_MODEL), lambda i, j: (_cond_of_tile(i, FFN_TM) // per, 0, 0)),
            pl.BlockSpec((None, None, 1, D_MODEL), lambda i, j: (layer, sub, 0, 0)),
            pl.BlockSpec((None, None, D_MODEL, FFN_CHUNK), lambda i, j: (layer, which, 0, j)),
            pl.BlockSpec((None, None, D_MODEL, FFN_CHUNK),
                         lambda i, j: (layer, which, 0, N_FFN_CHUNKS + j)),
            pl.BlockSpec((None, None, FFN_CHUNK, D_MODEL), lambda i, j: (layer, which, j, 0)),
        ],
        out_specs=pl.BlockSpec((FFN_TM, D_MODEL), lambda i, j: (i, 0)),
        out_shape=jax.ShapeDtypeStruct((NTOK, D_MODEL), F32),
        scratch_shapes=[pltpu.VMEM((FFN_TM, D_MODEL), BF16)],
        compiler_params=_cparams("parallel", "arbitrary"),
        name="ffn",
    )(x, mod_l, norm_g.reshape(DEPTH, 3, 1, D_MODEL), ffn_w_in, ffn_w_in, ffn_w_out)


def _group_mean_sq(x, width):
    r = lax.broadcasted_iota(jnp.int32, (width, width), 0) // HEAD_DIM
    c = lax.broadcasted_iota(jnp.int32, (width, width), 1) // HEAD_DIM
    ones = (r == c).astype(BF16)
    return _dot((x * x).astype(BF16), ones) * (1.0 / HEAD_DIM)


def _head_norm_rope(x, gain, cos, sin_next, sin_prev):
    width = x.shape[-1]
    xn = x * lax.rsqrt(_group_mean_sq(x, width) + RMS_EPS) * gain
    nxt = pltpu.roll(xn, width - 1, 1)
    prv = pltpu.roll(xn, 1, 1)
    return xn * cos + nxt * sin_next + prv * sin_prev


def _even_in_kernel(x_ref, mod_ref, g_ref, w_ref, qg_ref, kg_ref,
                    cq_ref, snq_ref, spq_ref, ck_ref, snk_ref, spk_ref,
                    q_ref, k_ref, v_ref, glu_ref):
    h = _modulate(x_ref[...], g_ref[...], mod_ref[0, 3:4, :], mod_ref[0, 4:5, :]).astype(BF16)

    def proj(lo, hi):
        return _dot(h, w_ref[:, lo:hi].astype(BF16))

    q = proj(0, ATT_Q)
    q = _head_norm_rope(q, qg_ref[...], cq_ref[...], snq_ref[...], spq_ref[...])
    q_ref[...] = (q * (HEAD_DIM ** -0.5)).astype(BF16)
    k = proj(ATT_Q, ATT_Q + ATT_KV)
    k_ref[...] = _head_norm_rope(k, kg_ref[...], ck_ref[...], snk_ref[...], spk_ref[...])
    v_ref[...] = proj(ATT_Q + ATT_KV, ATT_Q + 2 * ATT_KV)
    a0 = ATT_Q + 2 * ATT_KV
    a = proj(a0, a0 + CONV_CH)
    gt = proj(a0 + CONV_CH, a0 + 2 * CONV_CH)
    glu_ref[...] = a * jax.nn.sigmoid(gt)


def _rope_tables():
    t = np.arange(DEC_SEQ)
    row = (t // GRID_W).astype(np.float64)
    col = (t % GRID_W).astype(np.float64)
    inv_freq = 1.0 / (ROPE_THETA ** (np.arange(0, AXIS_ROPE_DIM, 2, dtype=np.float64) / AXIS_ROPE_DIM))
    ang = np.concatenate([row[:, None] * inv_freq, col[:, None] * inv_freq], axis=-1)
    ang = np.repeat(ang, 2, axis=-1)
    even = (np.arange(HEAD_DIM) % 2 == 0)[None, :]
    cos = np.cos(ang)
    sin_next = np.where(even, -np.sin(ang), 0.0)
    sin_prev = np.where(even, 0.0, np.sin(ang))

    def both(tab):
        ident = np.ones_like(tab) if tab is cos else np.zeros_like(tab)
        return np.stack([np.tile(ident, (1, N_HEADS)), np.tile(tab, (1, N_HEADS))]).astype(np.float32)

    return both(cos), both(sin_next), both(sin_prev)


def _even_in(x, mod_l, norm_g, ev_w_in, q_norm_g, k_norm_g, layer, j):
    cos, sin_next, sin_prev = (jnp.asarray(t) for t in _rope_tables())
    qg = jnp.tile(q_norm_g[j], N_HEADS).reshape(1, ATT_Q)
    kg = jnp.tile(k_norm_g[j], N_KV_HEADS).reshape(1, ATT_KV)
    rows = EVEN_IN_ROWS
    per_seq = DEC_SEQ // rows
    first_latent = NTOK_P // rows
    tab_idx = lambda i: (jnp.minimum(i // first_latent, 1), i % per_seq, 0)
    tab_q = pl.BlockSpec((None, rows, ATT_Q), tab_idx)
    tab_k = pl.BlockSpec((None, rows, ATT_KV), tab_idx)
    tok = lambda n: pl.BlockSpec((rows, n), lambda i: (i, 0))
    return pl.pallas_call(
        _even_in_kernel,
        grid=(NTOK // rows,),
        in_specs=[
            tok(D_MODEL),
            pl.BlockSpec((1, N_MOD, D_MODEL), lambda i: (_cond_of_tile(i, rows), 0, 0)),
            pl.BlockSpec((None, None, 1, D_MODEL), lambda i: (layer, 1, 0, 0)),
            pl.BlockSpec((None, D_MODEL, EVEN_IN), lambda i: (j, 0, 0)),
            pl.BlockSpec((1, ATT_Q), lambda i: (0, 0)),
            pl.BlockSpec((1, ATT_KV), lambda i: (0, 0)),
            tab_q, tab_q, tab_q, tab_k, tab_k, tab_k,
        ],
        out_specs=[tok(ATT_Q), tok(ATT_KV), tok(ATT_KV), tok(CONV_CH)],
        out_shape=[
            jax.ShapeDtypeStruct((NTOK, ATT_Q), BF16),
            jax.ShapeDtypeStruct((NTOK, ATT_KV), F32),
            jax.ShapeDtypeStruct((NTOK, ATT_KV), F32),
            jax.ShapeDtypeStruct((NTOK, CONV_CH), F32),
        ],
        compiler_params=_cparams("parallel"),
        name="even_in",
    )(x, mod_l, norm_g.reshape(DEPTH, 3, 1, D_MODEL), ev_w_in, qg, kg,
      cos, sin_next, sin_prev, cos, sin_next, sin_prev)


ATT_WIDTH = GQA_GROUP * HEAD_DIM


def _expand_kv_head(x, kv):
    lane = lax.broadcasted_iota(jnp.int32, x.shape, 1)
    swapped = pltpu.roll(x, HEAD_DIM, 1)
    pair = jnp.where((lane // HEAD_DIM) == kv, x, swapped)
    return jnp.concatenate([pair, pair], axis=1)


def _attend(q_ref, q_rows, kbig, vbig, n_keys, o_ref):
    lane_group = lax.broadcasted_iota(jnp.int32, (ATT_TQ, ATT_WIDTH), 1) // HEAD_DIM
    for kv in range(N_KV_HEADS):
        cols = slice(kv * ATT_WIDTH, (kv + 1) * ATT_WIDTH)
        q = q_ref[q_rows, cols]
        kb = kbig[kv, 0:n_keys, :]
        vb = vbig[kv, 0:n_keys, :]
        acc = jnp.zeros((ATT_TQ, ATT_WIDTH), F32)
        for g in range(GQA_GROUP):
            mine = lane_group == g
            qm = jnp.where(mine, q, jnp.zeros_like(q))
            s = lax.dot_general(qm, kb, (((1,), (1,)), ((), ())), preferred_element_type=F32)
            m = jnp.max(s, axis=-1, keepdims=True)
            p = jnp.exp(s - m)
            denom = jnp.sum(p, axis=-1, keepdims=True)
            o = _dot(p.astype(BF16), vb)
            acc = jnp.where(mine, o / denom, acc)
        o_ref[q_rows, cols] = acc.astype(BF16)


def _attn_kernel(q_ref, k_ref, v_ref, ck_ref, cv_ref, o_ref, kbig, vbig):
    i = pl.program_id(0)
    t = pl.program_id(1)
    rows = pl.ds(pl.multiple_of(t * ATT_TQ, ATT_TQ), ATT_TQ)

    def expand(k, v, n):
        for kv in range(N_KV_HEADS):
            kbig[kv, 0:n, :] = _expand_kv_head(k, kv).astype(BF16)
            vbig[kv, 0:n, :] = _expand_kv_head(v, kv).astype(BF16)

    @pl.when(i < N_TILES_P)
    def _():
        expand(k_ref[rows, :], v_ref[rows, :], SEQ)
        _attend(q_ref, rows, kbig, vbig, SEQ, o_ref)

    @pl.when(i >= N_TILES_P)
    def _():
        @pl.when(t == 0)
        def _():
            expand(jnp.concatenate([ck_ref[...], k_ref[...]], axis=0),
                   jnp.concatenate([cv_ref[...], v_ref[...]], axis=0), PAST_LEN + DEC_SEQ)

        _attend(q_ref, rows, kbig, vbig, PAST_LEN + DEC_SEQ, o_ref)


def _attention(q, k, v, ck, cv, j):
    tok = lambda n: pl.BlockSpec((TM, n), lambda i, t: (i, 0))
    ctx = pl.BlockSpec((None, None, PAST_LEN, ATT_KV), lambda i, t: (jnp.maximum(i - N_TILES_P, 0), j, 0, 0))
    n_keys = PAST_LEN + DEC_SEQ
    return pl.pallas_call(
        _attn_kernel,
        grid=(N_TILES, TM // ATT_TQ),
        in_specs=[tok(ATT_Q), tok(ATT_KV), tok(ATT_KV), ctx, ctx],
        out_specs=tok(ATT_Q),
        out_shape=jax.ShapeDtypeStruct((NTOK, ATT_Q), BF16),
        scratch_shapes=[pltpu.VMEM((N_KV_HEADS, n_keys, ATT_WIDTH), BF16),
                        pltpu.VMEM((N_KV_HEADS, n_keys, ATT_WIDTH), BF16)],
        compiler_params=_cparams("parallel", "arbitrary"),
        name="attn",
    )(q, k, v, ck, cv)


def _conv_kernel(x_ref, prev_ref, next_ref, w_ref, b_ref, g_ref, beta_ref, o_ref, pad_scr, y_scr):
    r = pl.program_id(0)
    tiles_p = SEQ // CONV_ROWS
    tiles_s = DEC_SEQ // CONV_ROWS
    n_p = NTOK_P // CONV_ROWS
    pos = jnp.where(r < n_p, r % tiles_p, (r - n_p) % tiles_s)
    last = jnp.where(r < n_p, tiles_p - 1, tiles_s - 1)
    zero = jnp.zeros((CONV_HALO, CONV_CH), F32)
    pad_scr[0:CONV_HALO, :] = jnp.where(pos == 0, zero, prev_ref[...])
    pad_scr[CONV_HALO:CONV_HALO + CONV_ROWS, :] = x_ref[...]
    pad_scr[CONV_HALO + CONV_ROWS:, :] = jnp.where(pos == last, zero, next_ref[...])
    off = CONV_HALO - CONV_K // 2
    span = CONV_ROWS + (CONV_K // V7X_SUBLANES) * V7X_SUBLANES
    for c0 in range(0, CONV_CH, V7X_LANES):
        cols = slice(c0, c0 + V7X_LANES)
        acc = jnp.zeros((CONV_ROWS, V7X_LANES), F32)
        for res in range(V7X_SUBLANES):
            taps = [k for k in range(CONV_K) if (off + k) % V7X_SUBLANES == res]
            window = pad_scr[res:res + span, cols]
            for k in taps:
                a = off + k - res
                acc = acc + window[a:a + CONV_ROWS, :] * w_ref[k:k + 1, cols]
        y_scr[:, cols] = acc
    y = _layer_norm(y_scr[...] + b_ref[...], g_ref[...], beta_ref[...])
    o_ref[...] = (y * jax.nn.sigmoid(y)).astype(BF16)


def _conv(glu, conv_w, conv_b, cn_g, cn_b, j):
    per = CONV_ROWS // CONV_HALO
    n_halo = NTOK // CONV_HALO
    vec = pl.BlockSpec((None, 1, CONV_CH), lambda r: (j, 0, 0))
    n_even = conv_b.shape[0]
    return pl.pallas_call(
        _conv_kernel,
        grid=(NTOK // CONV_ROWS,),
        in_specs=[
            pl.BlockSpec((CONV_ROWS, CONV_CH), lambda r: (r, 0)),
            pl.BlockSpec((CONV_HALO, CONV_CH), lambda r: (jnp.maximum(r * per - 1, 0), 0)),
            pl.BlockSpec((CONV_HALO, CONV_CH), lambda r: (jnp.minimum((r + 1) * per, n_halo - 1), 0)),
            pl.BlockSpec((None, CONV_K, CONV_CH), lambda r: (j, 0, 0)),
            vec, vec, vec,
        ],
        out_specs=pl.BlockSpec((CONV_ROWS, CONV_CH), lambda r: (r, 0)),
        out_shape=jax.ShapeDtypeStruct((NTOK, CONV_CH), BF16),
        scratch_shapes=[pltpu.VMEM((CONV_ROWS + 2 * CONV_HALO, CONV_CH), F32),
                        pltpu.VMEM((CONV_ROWS, CONV_CH), F32)],
        compiler_params=_cparams("parallel"),
        name="conv",
    )(glu, glu, glu, conv_w, conv_b.reshape(n_even, 1, CONV_CH),
      cn_g.reshape(n_even, 1, CONV_CH), cn_b.reshape(n_even, 1, CONV_CH))


def _odd_in_kernel(x_ref, mod_ref, g_ref, w_ref, sg_ref, sb_ref, ws_ref, bs_ref, f_ref, sgu_ref):
    h = _modulate(x_ref[...], g_ref[...], mod_ref[0, 3:4, :], mod_ref[0, 4:5, :]).astype(BF16)

    def proj(lo, hi):
        return _dot(h, w_ref[:, lo:hi].astype(BF16))

    f_ref[...] = proj(0, FOURIER_CH).astype(BF16)
    u = proj(FOURIER_CH, FOURIER_CH + SGU_CH)
    v = _layer_norm(proj(FOURIER_CH + SGU_CH, ODD_IN), sg_ref[...], sb_ref[...]).astype(BF16)
    for g in range(SGU_GROUPS):
        wg = ws_ref[g].astype(BF16)
        bias = bs_ref[:, g:g + 1]
        cols = slice(g * SGU_GROUP_CH, (g + 1) * SGU_GROUP_CH)
        for n in range(TM // CHUNK):
            rows = slice(n * CHUNK, (n + 1) * CHUNK)
            mixed = _dot(wg, v[rows, cols]) + bias
            sgu_ref[rows, cols] = (u[rows, cols] * mixed).astype(BF16)


def _odd_in(x, mod_l, norm_g, od_w_in, sgu_norm_g, sgu_norm_b, sgu_w, sgu_b, layer, j):
    tok = lambda n: pl.BlockSpec((TM, n), lambda i: (i, 0))
    return pl.pallas_call(
        _odd_in_kernel,
        grid=(N_TILES,),
        in_specs=[
            tok(D_MODEL),
            pl.BlockSpec((1, N_MOD, D_MODEL), lambda i: (_cond_of_tile(i), 0, 0)),
            pl.BlockSpec((None, None, 1, D_MODEL), lambda i: (layer, 1, 0, 0)),
            pl.BlockSpec((None, D_MODEL, ODD_IN), lambda i: (j, 0, 0)),
            pl.BlockSpec((1, SGU_CH), lambda i: (0, 0)),
            pl.BlockSpec((1, SGU_CH), lambda i: (0, 0)),
            pl.BlockSpec((None, SGU_GROUPS, CHUNK, CHUNK), lambda i: (j, 0, 0, 0)),
            pl.BlockSpec((CHUNK, SGU_GROUPS), lambda i: (0, 0)),
        ],
        out_specs=[tok(FOURIER_CH), tok(SGU_CH)],
        out_shape=[jax.ShapeDtypeStruct((NTOK, FOURIER_CH), BF16),
                   jax.ShapeDtypeStruct((NTOK, SGU_CH), BF16)],
        compiler_params=_cparams("parallel"),
        name="odd_in",
    )(x, mod_l, norm_g.reshape(DEPTH, 3, 1, D_MODEL), od_w_in,
      sgu_norm_g[j].reshape(1, SGU_CH), sgu_norm_b[j].reshape(1, SGU_CH), sgu_w, sgu_b[j].T)


def _fourier_kernel(f_ref, wch_ref, wseq_p_ref, wseq_s_ref, o_ref):
    i = pl.program_id(0)
    pq = _dot(f_ref[...], wch_ref[...].astype(BF16)).astype(BF16)

    def seq_dft(rows, wseq, seq):
        stacked = jnp.concatenate([pq[rows, :FOURIER_CH], pq[rows, FOURIER_CH:]], axis=0)
        scale = 1.0 / np.sqrt(float(seq * FOURIER_GROUP_CH))
        o_ref[rows, :] = (_dot(wseq.astype(BF16), stacked) * scale).astype(BF16)

    @pl.when(i < N_TILES_P)
    def _():
        for s in range(SEQ_PER_TILE):
            seq_dft(slice(s * SEQ, (s + 1) * SEQ), wseq_p_ref[...], SEQ)

    @pl.when(i >= N_TILES_P)
    def _():
        seq_dft(slice(0, DEC_SEQ), wseq_s_ref[...], DEC_SEQ)


def _dft_constants():
    def cos_sin(n):
        idx = np.arange(n)
        ang = 2.0 * np.pi * ((idx[:, None] * idx[None, :]) % n) / n
        return np.cos(ang), np.sin(ang)

    cc, sc = cos_sin(FOURIER_GROUP_CH)
    eye = np.eye(FOURIER_GROUPS)
    wch = np.concatenate([np.kron(eye, cc), np.kron(eye, sc)], axis=1)

    def wseq(seq):
        cl, sl = cos_sin(seq)
        return jnp.asarray(np.concatenate([cl, -sl], axis=1), dtype=F32)

    return jnp.asarray(wch, dtype=F32), wseq(SEQ), wseq(DEC_SEQ)


def _fourier(f):
    wch, wseq_p, wseq_s = _dft_constants()
    tok = pl.BlockSpec((TM, FOURIER_CH), lambda i: (i, 0))
    full = lambda a: pl.BlockSpec(a.shape, lambda i: (0, 0))
    return pl.pallas_call(
        _fourier_kernel,
        grid=(N_TILES,),
        in_specs=[tok, full(wch), full(wseq_p), full(wseq_s)],
        out_specs=tok,
        out_shape=jax.ShapeDtypeStruct((NTOK, FOURIER_CH), BF16),
        compiler_params=_cparams("parallel"),
        name="fourier",
    )(f, wch, wseq_p, wseq_s)


def _out_proj_kernel(half, x_ref, mod_ref, a_ref, b_ref, w_ref, o_ref):
    mixed = _dot(a_ref[...], w_ref[:half, :].astype(BF16)) + _dot(b_ref[...], w_ref[half:, :].astype(BF16))
    o_ref[...] = x_ref[...] + mod_ref[0, 5:6, :] * mixed


def _out_proj(x, mod_l, a, b, w_out, j):
    half = a.shape[1]
    tok = lambda n: pl.BlockSpec((TM, n), lambda i: (i, 0))
    return pl.pallas_call(
        functools.partial(_out_proj_kernel, half),
        grid=(N_TILES,),
        in_specs=[
            tok(D_MODEL),
            pl.BlockSpec((1, N_MOD, D_MODEL), lambda i: (_cond_of_tile(i), 0, 0)),
            tok(half), tok(b.shape[1]),
            pl.BlockSpec((None, half + b.shape[1], D_MODEL), lambda i: (j, 0, 0)),
        ],
        out_specs=tok(D_MODEL),
        out_shape=jax.ShapeDtypeStruct((NTOK, D_MODEL), F32),
        compiler_params=_cparams("parallel"),
        name="out_proj",
    )(x, mod_l, a, b, w_out)


def kernel(x_prompt, x_sample, cache_k, cache_v, c, c_ctx, w_mod, b_mod, norm_g, ffn_w_in, ffn_w_out, ev_w_in, ev_w_out, q_norm_g, k_norm_g, conv_w, conv_b, conv_norm_g, conv_norm_b, od_w_in, od_w_out, sgu_norm_g, sgu_norm_b, sgu_w, sgu_b):
    x = jnp.concatenate([x_prompt.reshape(NTOK_P, D_MODEL), x_sample.reshape(NTOK_S, D_MODEL)], axis=0)
    cond = jnp.concatenate([c_ctx[None, :], c_ctx[None, :], c,
                            jnp.zeros((N_COND - COND_LATENT0 - DEC_BATCH, D_MODEL), F32)], axis=0)
    mod = _adaln(cond, w_mod, b_mod).reshape(DEPTH, N_COND, N_MOD, D_MODEL)
    ck = cache_k.reshape(DEC_BATCH, -1, PAST_LEN, ATT_KV)
    cv = cache_v.reshape(DEC_BATCH, -1, PAST_LEN, ATT_KV)

    new_k, new_v = [], []
    for layer in range(DEPTH):
        mod_l = mod[layer]
        j = layer // 2
        x = _ffn(x, mod_l, norm_g, ffn_w_in, ffn_w_out, layer, 0)
        if layer % 2 == 0:
            q, k, v, glu = _even_in(x, mod_l, norm_g, ev_w_in, q_norm_g, k_norm_g, layer, j)
            new_k.append(k[:NTOK_P].reshape(BATCH, SEQ, N_KV_HEADS, HEAD_DIM))
            new_v.append(v[:NTOK_P].reshape(BATCH, SEQ, N_KV_HEADS, HEAD_DIM))
            attn = _attention(q, k, v, ck, cv, j)
            conv = _conv(glu, conv_w, conv_b, conv_norm_g, conv_norm_b, j)
            x = _out_proj(x, mod_l, attn, conv, ev_w_out, j)
        else:
            f, sgu = _odd_in(x, mod_l, norm_g, od_w_in, sgu_norm_g, sgu_norm_b, sgu_w, sgu_b, layer, j)
            x = _out_proj(x, mod_l, _fourier(f), sgu, od_w_out, j)
        x = _ffn(x, mod_l, norm_g, ffn_w_in, ffn_w_out, layer, 1)

    y_prompt = x[:NTOK_P].reshape(BATCH, SEQ, D_MODEL)
    y_sample = x[NTOK_P:].reshape(DEC_BATCH, DEC_SEQ, D_MODEL)
    return y_prompt, y_sample, jnp.stack(new_k, axis=1), jnp.stack(new_v, axis=1)
```

```python
import functools

import numpy as np
import jax
import jax.numpy as jnp
from jax import lax
from jax.experimental import pallas as pl
from jax.experimental.pallas import tpu as pltpu

D_MODEL = 1024
BATCH = 32
SEQ = 256
DEPTH = 4
DEC_BATCH = 4
DEC_SEQ = 1024
PAST_LEN = 512
GRID_W = 64
N_HEADS = 8
N_KV_HEADS = 2
HEAD_DIM = 64
GQA_GROUP = N_HEADS // N_KV_HEADS
AXIS_ROPE_DIM = HEAD_DIM // 2
ROPE_THETA = 10000.0
ATT_Q = N_HEADS * HEAD_DIM
ATT_KV = N_KV_HEADS * HEAD_DIM
CONV_CH = 512
CONV_K = 31
FOURIER_GROUPS = 4
FOURIER_GROUP_CH = 128
FOURIER_CH = FOURIER_GROUPS * FOURIER_GROUP_CH
SGU_GROUPS = 4
SGU_GROUP_CH = 128
SGU_CH = SGU_GROUPS * SGU_GROUP_CH
CHUNK = 128
FFN_DIM = 2816
N_MOD = 9
EVEN_IN = ATT_Q + 2 * ATT_KV + 2 * CONV_CH
ODD_IN = FOURIER_CH + 2 * SGU_CH
RMS_EPS = 1e-6
LN_EPS = 1e-5

BF16 = jnp.bfloat16
F32 = jnp.float32

V7X_SUBLANES = 8
V7X_LANES = 128
NTOK_P = BATCH * SEQ
NTOK_S = DEC_BATCH * DEC_SEQ
NTOK = NTOK_P + NTOK_S
TM = DEC_SEQ
N_TILES = NTOK // TM
N_TILES_P = NTOK_P // TM
SEQ_PER_TILE = TM // SEQ
N_COND = 8
COND_LATENT0 = 2
FFN_TM = 2 * TM
FFN_CHUNK = 256
N_FFN_CHUNKS = FFN_DIM // FFN_CHUNK
EVEN_IN_ROWS = 512
ADALN_TN = 1024
CONV_ROWS = 128
CONV_HALO = 16
ATT_TQ = SEQ
V7X_VMEM_LIMIT_BYTES = 56 * 1024 * 1024


def _cparams(*sem):
    return pltpu.CompilerParams(dimension_semantics=sem, vmem_limit_bytes=V7X_VMEM_LIMIT_BYTES)


def _cond_of_tile(i, rows=TM):
    return jnp.maximum((i * rows - NTOK_P) // DEC_SEQ + COND_LATENT0, 0)


def _dot(a, b):
    return jnp.dot(a, b, preferred_element_type=F32)


def _modulate(x, g, shift, scale):
    ms = jnp.mean(x * x, axis=-1, keepdims=True)
    return (x * lax.rsqrt(ms + RMS_EPS) * g) * (1.0 + scale) + shift


def _layer_norm(x, g, b):
    mu = jnp.mean(x, axis=-1, keepdims=True)
    xc = x - mu
    var = jnp.mean(xc * xc, axis=-1, keepdims=True)
    return xc * lax.rsqrt(var + LN_EPS) * g + b


def _adaln_kernel(c_ref, w_ref, b_ref, o_ref):
    c = c_ref[...]
    s = (c * jax.nn.sigmoid(c)).astype(BF16)
    o_ref[...] = _dot(s, w_ref[...].astype(BF16)) + b_ref[...]


def _adaln(cond, w_mod, b_mod):
    n_out = N_MOD * D_MODEL
    return pl.pallas_call(
        _adaln_kernel,
        grid=(DEPTH, n_out // ADALN_TN),
        in_specs=[
            pl.BlockSpec((N_COND, D_MODEL), lambda l, n: (0, 0)),
            pl.BlockSpec((None, D_MODEL, ADALN_TN), lambda l, n: (l, 0, n)),
            pl.BlockSpec((None, 1, ADALN_TN), lambda l, n: (l, 0, n)),
        ],
        out_specs=pl.BlockSpec((None, N_COND, ADALN_TN), lambda l, n: (l, 0, n)),
        out_shape=jax.ShapeDtypeStruct((DEPTH, N_COND, n_out), F32),
        compiler_params=_cparams("parallel", "parallel"),
        name="adaln",
    )(cond, w_mod, b_mod.reshape(DEPTH, 1, n_out))


def _ffn_kernel(sub, x_ref, mod_ref, g_ref, wg_ref, wu_ref, wo_ref, o_ref, h_scr):
    j = pl.program_id(1)
    halves = [(c, slice(c * TM, (c + 1) * TM)) for c in range(FFN_TM // TM)]

    @pl.when(j == 0)
    def _():
        for c, rows in halves:
            h = _modulate(x_ref[rows, :], g_ref[...], mod_ref[c, 3 * sub:3 * sub + 1, :],
                          mod_ref[c, 3 * sub + 1:3 * sub + 2, :])
            h_scr[rows, :] = h.astype(BF16)
        o_ref[...] = jnp.zeros_like(o_ref)

    h = h_scr[...]
    gate = _dot(h, wg_ref[...].astype(BF16))
    up = _dot(h, wu_ref[...].astype(BF16))
    a = (gate * jax.nn.sigmoid(gate) * up).astype(BF16)
    o_ref[...] += _dot(a, wo_ref[...].astype(BF16))

    @pl.when(j == N_FFN_CHUNKS - 1)
    def _():
        for c, rows in halves:
            o_ref[rows, :] = x_ref[rows, :] + (0.5 * mod_ref[c, 3 * sub + 2:3 * sub + 3, :]) * o_ref[rows, :]


def _ffn(x, mod_l, norm_g, ffn_w_in, ffn_w_out, layer, which):
    sub = 2 * which
    per = FFN_TM // TM
    return pl.pallas_call(
        functools.partial(_ffn_kernel, sub),
        grid=(NTOK // FFN_TM, N_FFN_CHUNKS),
        in_specs=[
            pl.BlockSpec((FFN_TM, D_MODEL), lambda i, j: (i, 0)),
            pl.BlockSpec((per, N_MOD, D_MODEL), lambda i, j: (_cond_of_tile(i, FFN_TM) // per, 0, 0)),
            pl.BlockSpec((None, None, 1, D_MODEL), lambda i, j: (layer, sub, 0, 0)),
            pl.BlockSpec((None, None, D_MODEL, FFN_CHUNK), lambda i, j: (layer, which, 0, j)),
            pl.BlockSpec((None, None, D_MODEL, FFN_CHUNK),
                         lambda i, j: (layer, which, 0, N_FFN_CHUNKS + j)),
            pl.BlockSpec((None, None, FFN_CHUNK, D_MODEL), lambda i, j: (layer, which, j, 0)),
        ],
        out_specs=pl.BlockSpec((FFN_TM, D_MODEL), lambda i, j: (i, 0)),
        out_shape=jax.ShapeDtypeStruct((NTOK, D_MODEL), F32),
        scratch_shapes=[pltpu.VMEM((FFN_TM, D_MODEL), BF16)],
        compiler_params=_cparams("parallel", "arbitrary"),
        name="ffn",
    )(x, mod_l, norm_g.reshape(DEPTH, 3, 1, D_MODEL), ffn_w_in, ffn_w_in, ffn_w_out)


def _group_mean_sq(x, width):
    r = lax.broadcasted_iota(jnp.int32, (width, width), 0) // HEAD_DIM
    c = lax.broadcasted_iota(jnp.int32, (width, width), 1) // HEAD_DIM
    ones = (r == c).astype(BF16)
    return _dot((x * x).astype(BF16), ones) * (1.0 / HEAD_DIM)


def _head_norm_rope(x, gain, cos, sin_next, sin_prev):
    width = x.shape[-1]
    xn = x * lax.rsqrt(_group_mean_sq(x, width) + RMS_EPS) * gain
    nxt = pltpu.roll(xn, width - 1, 1)
    prv = pltpu.roll(xn, 1, 1)
    return xn * cos + nxt * sin_next + prv * sin_prev


def _even_in_kernel(x_ref, mod_ref, g_ref, w_ref, qg_ref, kg_ref,
                    cq_ref, snq_ref, spq_ref, ck_ref, snk_ref, spk_ref,
                    q_ref, k_ref, v_ref, glu_ref):
    h = _modulate(x_ref[...], g_ref[...], mod_ref[0, 3:4, :], mod_ref[0, 4:5, :]).astype(BF16)

    def proj(lo, hi):
        return _dot(h, w_ref[:, lo:hi].astype(BF16))

    q = proj(0, ATT_Q)
    q = _head_norm_rope(q, qg_ref[...], cq_ref[...], snq_ref[...], spq_ref[...])
    q_ref[...] = (q * (HEAD_DIM ** -0.5)).astype(BF16)
    k = proj(ATT_Q, ATT_Q + ATT_KV)
    k_ref[...] = _head_norm_rope(k, kg_ref[...], ck_ref[...], snk_ref[...], spk_ref[...])
    v_ref[...] = proj(ATT_Q + ATT_KV, ATT_Q + 2 * ATT_KV)
    a0 = ATT_Q + 2 * ATT_KV
    a = proj(a0, a0 + CONV_CH)
    gt = proj(a0 + CONV_CH, a0 + 2 * CONV_CH)
    glu_ref[...] = a * jax.nn.sigmoid(gt)


def _rope_tables():
    t = np.arange(DEC_SEQ)
    row = (t // GRID_W).astype(np.float64)
    col = (t % GRID_W).astype(np.float64)
    inv_freq = 1.0 / (ROPE_THETA ** (np.arange(0, AXIS_ROPE_DIM, 2, dtype=np.float64) / AXIS_ROPE_DIM))
    ang = np.concatenate([row[:, None] * inv_freq, col[:, None] * inv_freq], axis=-1)
    ang = np.repeat(ang, 2, axis=-1)
    even = (np.arange(HEAD_DIM) % 2 == 0)[None, :]
    cos = np.cos(ang)
    sin_next = np.where(even, -np.sin(ang), 0.0)
    sin_prev = np.where(even, 0.0, np.sin(ang))

    def both(tab):
        ident = np.ones_like(tab) if tab is cos else np.zeros_like(tab)
        return np.stack([np.tile(ident, (1, N_HEADS)), np.tile(tab, (1, N_HEADS))]).astype(np.float32)

    return both(cos), both(sin_next), both(sin_prev)


def _even_in(x, mod_l, norm_g, ev_w_in, q_norm_g, k_norm_g, layer, j):
    cos, sin_next, sin_prev = (jnp.asarray(t) for t in _rope_tables())
    qg = jnp.tile(q_norm_g[j], N_HEADS).reshape(1, ATT_Q)
    kg = jnp.tile(k_norm_g[j], N_KV_HEADS).reshape(1, ATT_KV)
    rows = EVEN_IN_ROWS
    per_seq = DEC_SEQ // rows
    first_latent = NTOK_P // rows
    tab_idx = lambda i: (jnp.minimum(i // first_latent, 1), i % per_seq, 0)
    tab_q = pl.BlockSpec((None, rows, ATT_Q), tab_idx)
    tab_k = pl.BlockSpec((None, rows, ATT_KV), tab_idx)
    tok = lambda n: pl.BlockSpec((rows, n), lambda i: (i, 0))
    return pl.pallas_call(
        _even_in_kernel,
        grid=(NTOK // rows,),
        in_specs=[
            tok(D_MODEL),
            pl.BlockSpec((1, N_MOD, D_MODEL), lambda i: (_cond_of_tile(i, rows), 0, 0)),
            pl.BlockSpec((None, None, 1, D_MODEL), lambda i: (layer, 1, 0, 0)),
            pl.BlockSpec((None, D_MODEL, EVEN_IN), lambda i: (j, 0, 0)),
            pl.BlockSpec((1, ATT_Q), lambda i: (0, 0)),
            pl.BlockSpec((1, ATT_KV), lambda i: (0, 0)),
            tab_q, tab_q, tab_q, tab_k, tab_k, tab_k,
        ],
        out_specs=[tok(ATT_Q), tok(ATT_KV), tok(ATT_KV), tok(CONV_CH)],
        out_shape=[
            jax.ShapeDtypeStruct((NTOK, ATT_Q), BF16),
            jax.ShapeDtypeStruct((NTOK, ATT_KV), F32),
            jax.ShapeDtypeStruct((NTOK, ATT_KV), F32),
            jax.ShapeDtypeStruct((NTOK, CONV_CH), F32),
        ],
        compiler_params=_cparams("parallel"),
        name="even_in",
    )(x, mod_l, norm_g.reshape(DEPTH, 3, 1, D_MODEL), ev_w_in, qg, kg,
      cos, sin_next, sin_prev, cos, sin_next, sin_prev)


ATT_WIDTH = GQA_GROUP * HEAD_DIM


def _expand_kv_head(x, kv):
    lane = lax.broadcasted_iota(jnp.int32, x.shape, 1)
    swapped = pltpu.roll(x, HEAD_DIM, 1)
    pair = jnp.where((lane // HEAD_DIM) == kv, x, swapped)
    return jnp.concatenate([pair, pair], axis=1)


def _attend(q_ref, q_rows, kbig, vbig, n_keys, o_ref):
    lane_group = lax.broadcasted_iota(jnp.int32, (ATT_TQ, ATT_WIDTH), 1) // HEAD_DIM
    for kv in range(N_KV_HEADS):
        cols = slice(kv * ATT_WIDTH, (kv + 1) * ATT_WIDTH)
        q = q_ref[q_rows, cols]
        kb = kbig[kv, 0:n_keys, :]
        vb = vbig[kv, 0:n_keys, :]
        acc = jnp.zeros((ATT_TQ, ATT_WIDTH), F32)
        for g in range(GQA_GROUP):
            mine = lane_group == g
            qm = jnp.where(mine, q, jnp.zeros_like(q))
            s = lax.dot_general(qm, kb, (((1,), (1,)), ((), ())), preferred_element_type=F32)
            m = jnp.max(s, axis=-1, keepdims=True)
            p = jnp.exp(s - m)
            denom = jnp.sum(p, axis=-1, keepdims=True)
            o = _dot(p.astype(BF16), vb)
            acc = jnp.where(mine, o / denom, acc)
        o_ref[q_rows, cols] = acc.astype(BF16)


def _attn_kernel(q_ref, k_ref, v_ref, ck_ref, cv_ref, o_ref, kbig, vbig):
    i = pl.program_id(0)
    t = pl.program_id(1)
    rows = pl.ds(pl.multiple_of(t * ATT_TQ, ATT_TQ), ATT_TQ)

    def expand(k, v, n):
        for kv in range(N_KV_HEADS):
            kbig[kv, 0:n, :] = _expand_kv_head(k, kv).astype(BF16)
            vbig[kv, 0:n, :] = _expand_kv_head(v, kv).astype(BF16)

    @pl.when(i < N_TILES_P)
    def _():
        expand(k_ref[rows, :], v_ref[rows, :], SEQ)
        _attend(q_ref, rows, kbig, vbig, SEQ, o_ref)

    @pl.when(i >= N_TILES_P)
    def _():
        @pl.when(t == 0)
        def _():
            expand(jnp.concatenate([ck_ref[...], k_ref[...]], axis=0),
                   jnp.concatenate([cv_ref[...], v_ref[...]], axis=0), PAST_LEN + DEC_SEQ)

        _attend(q_ref, rows, kbig, vbig, PAST_LEN + DEC_SEQ, o_ref)


def _attention(q, k, v, ck, cv, j):
    tok = lambda n: pl.BlockSpec((TM, n), lambda i, t: (i, 0))
    ctx = pl.BlockSpec((None, None, PAST_LEN, ATT_KV), lambda i, t: (jnp.maximum(i - N_TILES_P, 0), j, 0, 0))
    n_keys = PAST_LEN + DEC_SEQ
    return pl.pallas_call(
        _attn_kernel,
        grid=(N_TILES, TM // ATT_TQ),
        in_specs=[tok(ATT_Q), tok(ATT_KV), tok(ATT_KV), ctx, ctx],
        out_specs=tok(ATT_Q),
        out_shape=jax.ShapeDtypeStruct((NTOK, ATT_Q), BF16),
        scratch_shapes=[pltpu.VMEM((N_KV_HEADS, n_keys, ATT_WIDTH), BF16),
                        pltpu.VMEM((N_KV_HEADS, n_keys, ATT_WIDTH), BF16)],
        compiler_params=_cparams("parallel", "arbitrary"),
        name="attn",
    )(q, k, v, ck, cv)


def _conv_kernel(x_ref, prev_ref, next_ref, w_ref, b_ref, g_ref, beta_ref, o_ref, pad_scr, win_scr, y_scr):
    r = pl.program_id(0)
    tiles_p = SEQ // CONV_ROWS
    tiles_s = DEC_SEQ // CONV_ROWS
    n_p = NTOK_P // CONV_ROWS
    pos = jnp.where(r < n_p, r % tiles_p, (r - n_p) % tiles_s)
    last = jnp.where(r < n_p, tiles_p - 1, tiles_s - 1)
    zero = jnp.zeros((CONV_HALO, CONV_CH), F32)
    pad_scr[0:CONV_HALO, :] = jnp.where(pos == 0, zero, prev_ref[...])
    pad_scr[CONV_HALO:CONV_HALO + CONV_ROWS, :] = x_ref[...]
    pad_scr[CONV_HALO + CONV_ROWS:, :] = jnp.where(pos == last, zero, next_ref[...])
    off = CONV_HALO - CONV_K // 2
    span = win_scr.shape[1]
    for res in range(V7X_SUBLANES):
        win_scr[res] = pad_scr[res:res + span, :]
    for c0 in range(0, CONV_CH, V7X_LANES):
        cols = slice(c0, c0 + V7X_LANES)
        acc = jnp.zeros((CONV_ROWS, V7X_LANES), F32)
        for k in range(CONV_K):
            res = (off + k) % V7X_SUBLANES
            a = off + k - res
            acc = acc + win_scr[res, a:a + CONV_ROWS, cols] * w_ref[k:k + 1, cols]
        y_scr[:, cols] = acc
    y = _layer_norm(y_scr[...] + b_ref[...], g_ref[...], beta_ref[...])
    o_ref[...] = (y * jax.nn.sigmoid(y)).astype(BF16)


def _conv(glu, conv_w, conv_b, cn_g, cn_b, j):
    per = CONV_ROWS // CONV_HALO
    n_halo = NTOK // CONV_HALO
    vec = pl.BlockSpec((None, 1, CONV_CH), lambda r: (j, 0, 0))
    n_even = conv_b.shape[0]
    return pl.pallas_call(
        _conv_kernel,
        grid=(NTOK // CONV_ROWS,),
        in_specs=[
            pl.BlockSpec((CONV_ROWS, CONV_CH), lambda r: (r, 0)),
            pl.BlockSpec((CONV_HALO, CONV_CH), lambda r: (jnp.maximum(r * per - 1, 0), 0)),
            pl.BlockSpec((CONV_HALO, CONV_CH), lambda r: (jnp.minimum((r + 1) * per, n_halo - 1), 0)),
            pl.BlockSpec((None, CONV_K, CONV_CH), lambda r: (j, 0, 0)),
            vec, vec, vec,
        ],
        out_specs=pl.BlockSpec((CONV_ROWS, CONV_CH), lambda r: (r, 0)),
        out_shape=jax.ShapeDtypeStruct((NTOK, CONV_CH), BF16),
        scratch_shapes=[pltpu.VMEM((CONV_ROWS + 2 * CONV_HALO, CONV_CH), F32),
                        pltpu.VMEM((V7X_SUBLANES, CONV_ROWS + 2 * CONV_HALO - V7X_SUBLANES, CONV_CH), F32),
                        pltpu.VMEM((CONV_ROWS, CONV_CH), F32)],
        compiler_params=_cparams("parallel"),
        name="conv",
    )(glu, glu, glu, conv_w, conv_b.reshape(n_even, 1, CONV_CH),
      cn_g.reshape(n_even, 1, CONV_CH), cn_b.reshape(n_even, 1, CONV_CH))


def _odd_in_kernel(x_ref, mod_ref, g_ref, w_ref, sg_ref, sb_ref, ws_ref, bs_ref, f_ref, sgu_ref):
    h = _modulate(x_ref[...], g_ref[...], mod_ref[0, 3:4, :], mod_ref[0, 4:5, :]).astype(BF16)

    def proj(lo, hi):
        return _dot(h, w_ref[:, lo:hi].astype(BF16))

    f_ref[...] = proj(0, FOURIER_CH).astype(BF16)
    u = proj(FOURIER_CH, FOURIER_CH + SGU_CH)
    v = _layer_norm(proj(FOURIER_CH + SGU_CH, ODD_IN), sg_ref[...], sb_ref[...]).astype(BF16)
    for g in range(SGU_GROUPS):
        wg = ws_ref[g].astype(BF16)
        bias = bs_ref[:, g:g + 1]
        cols = slice(g * SGU_GROUP_CH, (g + 1) * SGU_GROUP_CH)
        for n in range(TM // CHUNK):
            rows = slice(n * CHUNK, (n + 1) * CHUNK)
            mixed = _dot(wg, v[rows, cols]) + bias
            sgu_ref[rows, cols] = (u[rows, cols] * mixed).astype(BF16)


def _odd_in(x, mod_l, norm_g, od_w_in, sgu_norm_g, sgu_norm_b, sgu_w, sgu_b, layer, j):
    tok = lambda n: pl.BlockSpec((TM, n), lambda i: (i, 0))
    return pl.pallas_call(
        _odd_in_kernel,
        grid=(N_TILES,),
        in_specs=[
            tok(D_MODEL),
            pl.BlockSpec((1, N_MOD, D_MODEL), lambda i: (_cond_of_tile(i), 0, 0)),
            pl.BlockSpec((None, None, 1, D_MODEL), lambda i: (layer, 1, 0, 0)),
            pl.BlockSpec((None, D_MODEL, ODD_IN), lambda i: (j, 0, 0)),
            pl.BlockSpec((1, SGU_CH), lambda i: (0, 0)),
            pl.BlockSpec((1, SGU_CH), lambda i: (0, 0)),
            pl.BlockSpec((None, SGU_GROUPS, CHUNK, CHUNK), lambda i: (j, 0, 0, 0)),
            pl.BlockSpec((CHUNK, SGU_GROUPS), lambda i: (0, 0)),
        ],
        out_specs=[tok(FOURIER_CH), tok(SGU_CH)],
        out_shape=[jax.ShapeDtypeStruct((NTOK, FOURIER_CH), BF16),
                   jax.ShapeDtypeStruct((NTOK, SGU_CH), BF16)],
        compiler_params=_cparams("parallel"),
        name="odd_in",
    )(x, mod_l, norm_g.reshape(DEPTH, 3, 1, D_MODEL), od_w_in,
      sgu_norm_g[j].reshape(1, SGU_CH), sgu_norm_b[j].reshape(1, SGU_CH), sgu_w, sgu_b[j].T)


def _fourier_kernel(f_ref, wch_ref, wseq_p_ref, wseq_s_ref, o_ref):
    i = pl.program_id(0)
    pq = _dot(f_ref[...], wch_ref[...].astype(BF16)).astype(BF16)

    def seq_dft(rows, wseq, seq):
        stacked = jnp.concatenate([pq[rows, :FOURIER_CH], pq[rows, FOURIER_CH:]], axis=0)
        scale = 1.0 / np.sqrt(float(seq * FOURIER_GROUP_CH))
        o_ref[rows, :] = (_dot(wseq.astype(BF16), stacked) * scale).astype(BF16)

    @pl.when(i < N_TILES_P)
    def _():
        for s in range(SEQ_PER_TILE):
            seq_dft(slice(s * SEQ, (s + 1) * SEQ), wseq_p_ref[...], SEQ)

    @pl.when(i >= N_TILES_P)
    def _():
        seq_dft(slice(0, DEC_SEQ), wseq_s_ref[...], DEC_SEQ)


def _dft_constants():
    def cos_sin(n):
        idx = np.arange(n)
        ang = 2.0 * np.pi * ((idx[:, None] * idx[None, :]) % n) / n
        return np.cos(ang), np.sin(ang)

    cc, sc = cos_sin(FOURIER_GROUP_CH)
    eye = np.eye(FOURIER_GROUPS)
    wch = np.concatenate([np.kron(eye, cc), np.kron(eye, sc)], axis=1)

    def wseq(seq):
        cl, sl = cos_sin(seq)
        return jnp.asarray(np.concatenate([cl, -sl], axis=1), dtype=F32)

    return jnp.asarray(wch, dtype=F32), wseq(SEQ), wseq(DEC_SEQ)


def _fourier(f):
    wch, wseq_p, wseq_s = _dft_constants()
    tok = pl.BlockSpec((TM, FOURIER_CH), lambda i: (i, 0))
    full = lambda a: pl.BlockSpec(a.shape, lambda i: (0, 0))
    return pl.pallas_call(
        _fourier_kernel,
        grid=(N_TILES,),
        in_specs=[tok, full(wch), full(wseq_p), full(wseq_s)],
        out_specs=tok,
        out_shape=jax.ShapeDtypeStruct((NTOK, FOURIER_CH), BF16),
        compiler_params=_cparams("parallel"),
        name="fourier",
    )(f, wch, wseq_p, wseq_s)


def _out_proj_kernel(half, x_ref, mod_ref, a_ref, b_ref, w_ref, o_ref):
    mixed = _dot(a_ref[...], w_ref[:half, :].astype(BF16)) + _dot(b_ref[...], w_ref[half:, :].astype(BF16))
    o_ref[...] = x_ref[...] + mod_ref[0, 5:6, :] * mixed


def _out_proj(x, mod_l, a, b, w_out, j):
    half = a.shape[1]
    tok = lambda n: pl.BlockSpec((TM, n), lambda i: (i, 0))
    return pl.pallas_call(
        functools.partial(_out_proj_kernel, half),
        grid=(N_TILES,),
        in_specs=[
            tok(D_MODEL),
            pl.BlockSpec((1, N_MOD, D_MODEL), lambda i: (_cond_of_tile(i), 0, 0)),
            tok(half), tok(b.shape[1]),
            pl.BlockSpec((None, half + b.shape[1], D_MODEL), lambda i: (j, 0, 0)),
        ],
        out_specs=tok(D_MODEL),
        out_shape=jax.ShapeDtypeStruct((NTOK, D_MODEL), F32),
        compiler_params=_cparams("parallel"),
        name="out_proj",
    )(x, mod_l, a, b, w_out)


def kernel(x_prompt, x_sample, cache_k, cache_v, c, c_ctx, w_mod, b_mod, norm_g, ffn_w_in, ffn_w_out, ev_w_in, ev_w_out, q_norm_g, k_norm_g, conv_w, conv_b, conv_norm_g, conv_norm_b, od_w_in, od_w_out, sgu_norm_g, sgu_norm_b, sgu_w, sgu_b):
    x = jnp.concatenate([x_prompt.reshape(NTOK_P, D_MODEL), x_sample.reshape(NTOK_S, D_MODEL)], axis=0)
    cond = jnp.concatenate([c_ctx[None, :], c_ctx[None, :], c,
                            jnp.zeros((N_COND - COND_LATENT0 - DEC_BATCH, D_MODEL), F32)], axis=0)
    mod = _adaln(cond, w_mod, b_mod).reshape(DEPTH, N_COND, N_MOD, D_MODEL)
    ck = cache_k.reshape(DEC_BATCH, -1, PAST_LEN, ATT_KV)
    cv = cache_v.reshape(DEC_BATCH, -1, PAST_LEN, ATT_KV)

    new_k, new_v = [], []
    for layer in range(DEPTH):
        mod_l = mod[layer]
        j = layer // 2
        x = _ffn(x, mod_l, norm_g, ffn_w_in, ffn_w_out, layer, 0)
        if layer % 2 == 0:
            q, k, v, glu = _even_in(x, mod_l, norm_g, ev_w_in, q_norm_g, k_norm_g, layer, j)
            new_k.append(k[:NTOK_P].reshape(BATCH, SEQ, N_KV_HEADS, HEAD_DIM))
            new_v.append(v[:NTOK_P].reshape(BATCH, SEQ, N_KV_HEADS, HEAD_DIM))
            attn = _attention(q, k, v, ck, cv, j)
            conv = _conv(glu, conv_w, conv_b, conv_norm_g, conv_norm_b, j)
            x = _out_proj(x, mod_l, attn, conv, ev_w_out, j)
        else:
            f, sgu = _odd_in(x, mod_l, norm_g, od_w_in, sgu_norm_g, sgu_norm_b, sgu_w, sgu_b, layer, j)
            x = _out_proj(x, mod_l, _fourier(f), sgu, od_w_out, j)
        x = _ffn(x, mod_l, norm_g, ffn_w_in, ffn_w_out, layer, 1)

    y_prompt = x[:NTOK_P].reshape(BATCH, SEQ, D_MODEL)
    y_sample = x[NTOK_P:].reshape(DEC_BATCH, DEC_SEQ, D_MODEL)
    return y_prompt, y_sample, jnp.stack(new_k, axis=1), jnp.stack(new_v, axis=1)
```

```python
import functools

import numpy as np
import jax
import jax.numpy as jnp
from jax import lax
from jax.experimental import pallas as pl
from jax.experimental.pallas import tpu as pltpu

D_MODEL = 1024
BATCH = 32
SEQ = 256
DEPTH = 4
DEC_BATCH = 4
DEC_SEQ = 1024
PAST_LEN = 512
GRID_W = 64
N_HEADS = 8
N_KV_HEADS = 2
HEAD_DIM = 64
GQA_GROUP = N_HEADS // N_KV_HEADS
AXIS_ROPE_DIM = HEAD_DIM // 2
ROPE_THETA = 10000.0
ATT_Q = N_HEADS * HEAD_DIM
ATT_KV = N_KV_HEADS * HEAD_DIM
CONV_CH = 512
CONV_K = 31
FOURIER_GROUPS = 4
FOURIER_GROUP_CH = 128
FOURIER_CH = FOURIER_GROUPS * FOURIER_GROUP_CH
SGU_GROUPS = 4
SGU_GROUP_CH = 128
SGU_CH = SGU_GROUPS * SGU_GROUP_CH
CHUNK = 128
FFN_DIM = 2816
N_MOD = 9
EVEN_IN = ATT_Q + 2 * ATT_KV + 2 * CONV_CH
ODD_IN = FOURIER_CH + 2 * SGU_CH
RMS_EPS = 1e-6
LN_EPS = 1e-5

BF16 = jnp.bfloat16
F32 = jnp.float32

V7X_SUBLANES = 8
V7X_LANES = 128
NTOK_P = BATCH * SEQ
NTOK_S = DEC_BATCH * DEC_SEQ
NTOK = NTOK_P + NTOK_S
TM = DEC_SEQ
N_TILES = NTOK // TM
N_TILES_P = NTOK_P // TM
SEQ_PER_TILE = TM // SEQ
N_COND = 8
COND_LATENT0 = 2
FFN_TM = 2 * TM
FFN_CHUNK = 256
N_FFN_CHUNKS = FFN_DIM // FFN_CHUNK
EVEN_IN_ROWS = 512
ADALN_TN = 1024
CONV_ROWS = 128
CONV_HALO = 16
ATT_TQ = SEQ
V7X_VMEM_LIMIT_BYTES = 56 * 1024 * 1024


def _cparams(*sem):
    return pltpu.CompilerParams(dimension_semantics=sem, vmem_limit_bytes=V7X_VMEM_LIMIT_BYTES)


def _cond_of_tile(i, rows=TM):
    return jnp.maximum((i * rows - NTOK_P) // DEC_SEQ + COND_LATENT0, 0)


def _dot(a, b):
    return jnp.dot(a, b, preferred_element_type=F32)


def _modulate(x, g, shift, scale):
    ms = jnp.mean(x * x, axis=-1, keepdims=True)
    return (x * lax.rsqrt(ms + RMS_EPS)) * (g * (1.0 + scale)) + shift


def _layer_norm(x, g, b):
    mu = jnp.mean(x, axis=-1, keepdims=True)
    xc = x - mu
    var = jnp.mean(xc * xc, axis=-1, keepdims=True)
    return xc * lax.rsqrt(var + LN_EPS) * g + b


def _adaln_kernel(c_ref, w_ref, b_ref, o_ref):
    c = c_ref[...]
    s = (c * jax.nn.sigmoid(c)).astype(BF16)
    o_ref[...] = _dot(s, w_ref[...].astype(BF16)) + b_ref[...]


def _adaln(cond, w_mod, b_mod):
    n_out = N_MOD * D_MODEL
    return pl.pallas_call(
        _adaln_kernel,
        grid=(DEPTH, n_out // ADALN_TN),
        in_specs=[
            pl.BlockSpec((N_COND, D_MODEL), lambda l, n: (0, 0)),
            pl.BlockSpec((None, D_MODEL, ADALN_TN), lambda l, n: (l, 0, n)),
            pl.BlockSpec((None, 1, ADALN_TN), lambda l, n: (l, 0, n)),
        ],
        out_specs=pl.BlockSpec((None, N_COND, ADALN_TN), lambda l, n: (l, 0, n)),
        out_shape=jax.ShapeDtypeStruct((DEPTH, N_COND, n_out), F32),
        compiler_params=_cparams("parallel", "parallel"),
        name="adaln",
    )(cond, w_mod, b_mod.reshape(DEPTH, 1, n_out))


def _ffn_kernel(sub, x_ref, mod_ref, g_ref, wg_ref, wu_ref, wo_ref, o_ref, h_scr):
    j = pl.program_id(1)
    halves = [(c, slice(c * TM, (c + 1) * TM)) for c in range(FFN_TM // TM)]

    @pl.when(j == 0)
    def _():
        for c, rows in halves:
            h = _modulate(x_ref[rows, :], g_ref[...], mod_ref[c, 3 * sub:3 * sub + 1, :],
                          mod_ref[c, 3 * sub + 1:3 * sub + 2, :])
            h_scr[rows, :] = h.astype(BF16)
        o_ref[...] = jnp.zeros_like(o_ref)

    h = h_scr[...]
    gate = _dot(h, wg_ref[...].astype(BF16))
    up = _dot(h, wu_ref[...].astype(BF16))
    a = (gate * jax.nn.sigmoid(gate) * up).astype(BF16)
    o_ref[...] += _dot(a, wo_ref[...].astype(BF16))

    @pl.when(j == N_FFN_CHUNKS - 1)
    def _():
        for c, rows in halves:
            o_ref[rows, :] = x_ref[rows, :] + (0.5 * mod_ref[c, 3 * sub + 2:3 * sub + 3, :]) * o_ref[rows, :]


def _ffn(x, mod_l, norm_g, ffn_w_in, ffn_w_out, layer, which):
    sub = 2 * which
    per = FFN_TM // TM
    return pl.pallas_call(
        functools.partial(_ffn_kernel, sub),
        grid=(NTOK // FFN_TM, N_FFN_CHUNKS),
        in_specs=[
            pl.BlockSpec((FFN_TM, D_MODEL), lambda i, j: (i, 0)),
            pl.BlockSpec((per, N_MOD, D_MODEL), lambda i, j: (_cond_of_tile(i, FFN_TM) // per, 0, 0)),
            pl.BlockSpec((None, None, 1, D_MODEL), lambda i, j: (layer, sub, 0, 0)),
            pl.BlockSpec((None, None, D_MODEL, FFN_CHUNK), lambda i, j: (layer, which, 0, j)),
            pl.BlockSpec((None, None, D_MODEL, FFN_CHUNK),
                         lambda i, j: (layer, which, 0, N_FFN_CHUNKS + j)),
            pl.BlockSpec((None, None, FFN_CHUNK, D_MODEL), lambda i, j: (layer, which, j, 0)),
        ],
        out_specs=pl.BlockSpec((FFN_TM, D_MODEL), lambda i, j: (i, 0)),
        out_shape=jax.ShapeDtypeStruct((NTOK, D_MODEL), F32),
        scratch_shapes=[pltpu.VMEM((FFN_TM, D_MODEL), BF16)],
        compiler_params=_cparams("parallel", "arbitrary"),
        name="ffn",
    )(x, mod_l, norm_g.reshape(DEPTH, 3, 1, D_MODEL), ffn_w_in, ffn_w_in, ffn_w_out)


def _group_mean_sq(x, width):
    r = lax.broadcasted_iota(jnp.int32, (width, width), 0) // HEAD_DIM
    c = lax.broadcasted_iota(jnp.int32, (width, width), 1) // HEAD_DIM
    ones = (r == c).astype(BF16)
    return _dot((x * x).astype(BF16), ones) * (1.0 / HEAD_DIM)


def _head_norm_rope(x, gain, cos, sin_next, sin_prev):
    width = x.shape[-1]
    xn = x * lax.rsqrt(_group_mean_sq(x, width) + RMS_EPS) * gain
    nxt = pltpu.roll(xn, width - 1, 1)
    prv = pltpu.roll(xn, 1, 1)
    return xn * cos + nxt * sin_next + prv * sin_prev


def _even_in_kernel(x_ref, mod_ref, g_ref, w_ref, qg_ref, kg_ref,
                    cq_ref, snq_ref, spq_ref, ck_ref, snk_ref, spk_ref,
                    q_ref, k_ref, v_ref, glu_ref):
    h = _modulate(x_ref[...], g_ref[...], mod_ref[0, 3:4, :], mod_ref[0, 4:5, :]).astype(BF16)

    def proj(lo, hi):
        return _dot(h, w_ref[:, lo:hi].astype(BF16))

    q = proj(0, ATT_Q)
    q = _head_norm_rope(q, qg_ref[...], cq_ref[...], snq_ref[...], spq_ref[...])
    q_ref[...] = (q * (HEAD_DIM ** -0.5)).astype(BF16)
    k = proj(ATT_Q, ATT_Q + ATT_KV)
    k_ref[...] = _head_norm_rope(k, kg_ref[...], ck_ref[...], snk_ref[...], spk_ref[...])
    v_ref[...] = proj(ATT_Q + ATT_KV, ATT_Q + 2 * ATT_KV)
    a0 = ATT_Q + 2 * ATT_KV
    a = proj(a0, a0 + CONV_CH)
    gt = proj(a0 + CONV_CH, a0 + 2 * CONV_CH)
    glu_ref[...] = a * jax.nn.sigmoid(gt)


def _rope_tables():
    t = np.arange(DEC_SEQ)
    row = (t // GRID_W).astype(np.float64)
    col = (t % GRID_W).astype(np.float64)
    inv_freq = 1.0 / (ROPE_THETA ** (np.arange(0, AXIS_ROPE_DIM, 2, dtype=np.float64) / AXIS_ROPE_DIM))
    ang = np.concatenate([row[:, None] * inv_freq, col[:, None] * inv_freq], axis=-1)
    ang = np.repeat(ang, 2, axis=-1)
    even = (np.arange(HEAD_DIM) % 2 == 0)[None, :]
    cos = np.cos(ang)
    sin_next = np.where(even, -np.sin(ang), 0.0)
    sin_prev = np.where(even, 0.0, np.sin(ang))

    def both(tab):
        ident = np.ones_like(tab) if tab is cos else np.zeros_like(tab)
        return np.stack([np.tile(ident, (1, N_HEADS)), np.tile(tab, (1, N_HEADS))]).astype(np.float32)

    return both(cos), both(sin_next), both(sin_prev)


def _even_in(x, mod_l, norm_g, ev_w_in, q_norm_g, k_norm_g, layer, j):
    cos, sin_next, sin_prev = (jnp.asarray(t) for t in _rope_tables())
    qg = jnp.tile(q_norm_g[j], N_HEADS).reshape(1, ATT_Q)
    kg = jnp.tile(k_norm_g[j], N_KV_HEADS).reshape(1, ATT_KV)
    rows = EVEN_IN_ROWS
    per_seq = DEC_SEQ // rows
    first_latent = NTOK_P // rows
    tab_idx = lambda i: (jnp.minimum(i // first_latent, 1), i % per_seq, 0)
    tab_q = pl.BlockSpec((None, rows, ATT_Q), tab_idx)
    tab_k = pl.BlockSpec((None, rows, ATT_KV), tab_idx)
    tok = lambda n: pl.BlockSpec((rows, n), lambda i: (i, 0))
    return pl.pallas_call(
        _even_in_kernel,
        grid=(NTOK // rows,),
        in_specs=[
            tok(D_MODEL),
            pl.BlockSpec((1, N_MOD, D_MODEL), lambda i: (_cond_of_tile(i, rows), 0, 0)),
            pl.BlockSpec((None, None, 1, D_MODEL), lambda i: (layer, 1, 0, 0)),
            pl.BlockSpec((None, D_MODEL, EVEN_IN), lambda i: (j, 0, 0)),
            pl.BlockSpec((1, ATT_Q), lambda i: (0, 0)),
            pl.BlockSpec((1, ATT_KV), lambda i: (0, 0)),
            tab_q, tab_q, tab_q, tab_k, tab_k, tab_k,
        ],
        out_specs=[tok(ATT_Q), tok(ATT_KV), tok(ATT_KV), tok(CONV_CH)],
        out_shape=[
            jax.ShapeDtypeStruct((NTOK, ATT_Q), BF16),
            jax.ShapeDtypeStruct((NTOK, ATT_KV), F32),
            jax.ShapeDtypeStruct((NTOK, ATT_KV), F32),
            jax.ShapeDtypeStruct((NTOK, CONV_CH), F32),
        ],
        compiler_params=_cparams("parallel"),
        name="even_in",
    )(x, mod_l, norm_g.reshape(DEPTH, 3, 1, D_MODEL), ev_w_in, qg, kg,
      cos, sin_next, sin_prev, cos, sin_next, sin_prev)


ATT_WIDTH = GQA_GROUP * HEAD_DIM


def _expand_kv_head(x, kv):
    lane = lax.broadcasted_iota(jnp.int32, x.shape, 1)
    swapped = pltpu.roll(x, HEAD_DIM, 1)
    pair = jnp.where((lane // HEAD_DIM) == kv, x, swapped)
    return jnp.concatenate([pair, pair], axis=1)


def _attend(q_ref, q_rows, kbig, vbig, n_keys, o_ref):
    lane_group = lax.broadcasted_iota(jnp.int32, (ATT_TQ, ATT_WIDTH), 1) // HEAD_DIM
    for kv in range(N_KV_HEADS):
        cols = slice(kv * ATT_WIDTH, (kv + 1) * ATT_WIDTH)
        q = q_ref[q_rows, cols]
        kb = kbig[kv, 0:n_keys, :]
        vb = vbig[kv, 0:n_keys, :]
        acc = jnp.zeros((ATT_TQ, ATT_WIDTH), F32)
        for g in range(GQA_GROUP):
            mine = lane_group == g
            qm = jnp.where(mine, q, jnp.zeros_like(q))
            s = lax.dot_general(qm, kb, (((1,), (1,)), ((), ())), preferred_element_type=F32)
            m = jnp.max(s, axis=-1, keepdims=True)
            p = jnp.exp(s - m)
            denom = jnp.sum(p, axis=-1, keepdims=True)
            o = _dot(p.astype(BF16), vb)
            acc = jnp.where(mine, o / denom, acc)
        o_ref[q_rows, cols] = acc.astype(BF16)


def _attn_kernel(q_ref, k_ref, v_ref, ck_ref, cv_ref, o_ref, kbig, vbig):
    i = pl.program_id(0)
    t = pl.program_id(1)
    rows = pl.ds(pl.multiple_of(t * ATT_TQ, ATT_TQ), ATT_TQ)

    def expand(k, v, n):
        for kv in range(N_KV_HEADS):
            kbig[kv, 0:n, :] = _expand_kv_head(k, kv).astype(BF16)
            vbig[kv, 0:n, :] = _expand_kv_head(v, kv).astype(BF16)

    @pl.when(i < N_TILES_P)
    def _():
        expand(k_ref[rows, :], v_ref[rows, :], SEQ)
        _attend(q_ref, rows, kbig, vbig, SEQ, o_ref)

    @pl.when(i >= N_TILES_P)
    def _():
        @pl.when(t == 0)
        def _():
            expand(jnp.concatenate([ck_ref[...], k_ref[...]], axis=0),
                   jnp.concatenate([cv_ref[...], v_ref[...]], axis=0), PAST_LEN + DEC_SEQ)

        _attend(q_ref, rows, kbig, vbig, PAST_LEN + DEC_SEQ, o_ref)


def _attention(q, k, v, ck, cv, j):
    tok = lambda n: pl.BlockSpec((TM, n), lambda i, t: (i, 0))
    ctx = pl.BlockSpec((None, None, PAST_LEN, ATT_KV), lambda i, t: (jnp.maximum(i - N_TILES_P, 0), j, 0, 0))
    n_keys = PAST_LEN + DEC_SEQ
    return pl.pallas_call(
        _attn_kernel,
        grid=(N_TILES, TM // ATT_TQ),
        in_specs=[tok(ATT_Q), tok(ATT_KV), tok(ATT_KV), ctx, ctx],
        out_specs=tok(ATT_Q),
        out_shape=jax.ShapeDtypeStruct((NTOK, ATT_Q), BF16),
        scratch_shapes=[pltpu.VMEM((N_KV_HEADS, n_keys, ATT_WIDTH), BF16),
                        pltpu.VMEM((N_KV_HEADS, n_keys, ATT_WIDTH), BF16)],
        compiler_params=_cparams("parallel", "arbitrary"),
        name="attn",
    )(q, k, v, ck, cv)


def _conv_out_kernel(x_ref, prev_ref, next_ref, w_ref, b_ref, g_ref, beta_ref,
                     res_ref, mod_ref, attn_ref, wo_ref, o_ref, pad_scr, win_scr, y_scr, wo_scr):
    r = pl.program_id(0)

    @pl.when(r == 0)
    def _():
        wo_scr[...] = wo_ref[...].astype(BF16)

    tiles_p = SEQ // CONV_ROWS
    tiles_s = DEC_SEQ // CONV_ROWS
    n_p = NTOK_P // CONV_ROWS
    pos = jnp.where(r < n_p, r % tiles_p, (r - n_p) % tiles_s)
    last = jnp.where(r < n_p, tiles_p - 1, tiles_s - 1)
    zero = jnp.zeros((CONV_HALO, CONV_CH), F32)
    pad_scr[0:CONV_HALO, :] = jnp.where(pos == 0, zero, prev_ref[...])
    pad_scr[CONV_HALO:CONV_HALO + CONV_ROWS, :] = x_ref[...]
    pad_scr[CONV_HALO + CONV_ROWS:, :] = jnp.where(pos == last, zero, next_ref[...])
    off = CONV_HALO - CONV_K // 2
    span = win_scr.shape[1]
    for res in range(V7X_SUBLANES):
        win_scr[res] = pad_scr[res:res + span, :]
    for c0 in range(0, CONV_CH, V7X_LANES):
        cols = slice(c0, c0 + V7X_LANES)
        acc = jnp.zeros((CONV_ROWS, V7X_LANES), F32)
        for k in range(CONV_K):
            res = (off + k) % V7X_SUBLANES
            a = off + k - res
            acc = acc + win_scr[res, a:a + CONV_ROWS, cols] * w_ref[k:k + 1, cols]
        y_scr[:, cols] = acc
    y = _layer_norm(y_scr[...] + b_ref[...], g_ref[...], beta_ref[...])
    conv = (y * jax.nn.sigmoid(y)).astype(BF16)
    mixed = _dot(attn_ref[...], wo_scr[:ATT_Q, :]) + _dot(conv, wo_scr[ATT_Q:, :])
    o_ref[...] = res_ref[...] + mod_ref[0, 5:6, :] * mixed


def _conv_out(x, mod_l, attn, glu, conv_w, conv_b, cn_g, cn_b, w_out, j):
    per = CONV_ROWS // CONV_HALO
    n_halo = NTOK // CONV_HALO
    vec = pl.BlockSpec((None, 1, CONV_CH), lambda r: (j, 0, 0))
    n_even = conv_b.shape[0]
    tok = lambda n: pl.BlockSpec((CONV_ROWS, n), lambda r: (r, 0))
    return pl.pallas_call(
        _conv_out_kernel,
        grid=(NTOK // CONV_ROWS,),
        in_specs=[
            tok(CONV_CH),
            pl.BlockSpec((CONV_HALO, CONV_CH), lambda r: (jnp.maximum(r * per - 1, 0), 0)),
            pl.BlockSpec((CONV_HALO, CONV_CH), lambda r: (jnp.minimum((r + 1) * per, n_halo - 1), 0)),
            pl.BlockSpec((None, CONV_K, CONV_CH), lambda r: (j, 0, 0)),
            vec, vec, vec,
            tok(D_MODEL),
            pl.BlockSpec((1, N_MOD, D_MODEL), lambda r: (_cond_of_tile(r, CONV_ROWS), 0, 0)),
            tok(ATT_Q),
            pl.BlockSpec((None, ATT_Q + CONV_CH, D_MODEL), lambda r: (j, 0, 0)),
        ],
        out_specs=tok(D_MODEL),
        out_shape=jax.ShapeDtypeStruct((NTOK, D_MODEL), F32),
        scratch_shapes=[pltpu.VMEM((CONV_ROWS + 2 * CONV_HALO, CONV_CH), F32),
                        pltpu.VMEM((V7X_SUBLANES, CONV_ROWS + 2 * CONV_HALO - V7X_SUBLANES, CONV_CH), F32),
                        pltpu.VMEM((CONV_ROWS, CONV_CH), F32),
                        pltpu.VMEM((ATT_Q + CONV_CH, D_MODEL), BF16)],
        compiler_params=_cparams("arbitrary"),
        name="conv_out",
    )(glu, glu, glu, conv_w, conv_b.reshape(n_even, 1, CONV_CH),
      cn_g.reshape(n_even, 1, CONV_CH), cn_b.reshape(n_even, 1, CONV_CH),
      x, mod_l, attn, w_out)


def _odd_in_kernel(x_ref, mod_ref, g_ref, w_ref, sg_ref, sb_ref, ws_ref, bs_ref, f_ref, sgu_ref):
    h = _modulate(x_ref[...], g_ref[...], mod_ref[0, 3:4, :], mod_ref[0, 4:5, :]).astype(BF16)

    def proj(lo, hi):
        return _dot(h, w_ref[:, lo:hi].astype(BF16))

    f_ref[...] = proj(0, FOURIER_CH).astype(BF16)
    u = proj(FOURIER_CH, FOURIER_CH + SGU_CH)
    v = _layer_norm(proj(FOURIER_CH + SGU_CH, ODD_IN), sg_ref[...], sb_ref[...]).astype(BF16)
    for g in range(SGU_GROUPS):
        wg = ws_ref[g].astype(BF16)
        bias = bs_ref[:, g:g + 1]
        cols = slice(g * SGU_GROUP_CH, (g + 1) * SGU_GROUP_CH)
        for n in range(TM // CHUNK):
            rows = slice(n * CHUNK, (n + 1) * CHUNK)
            mixed = _dot(wg, v[rows, cols]) + bias
            sgu_ref[rows, cols] = (u[rows, cols] * mixed).astype(BF16)


def _odd_in(x, mod_l, norm_g, od_w_in, sgu_norm_g, sgu_norm_b, sgu_w, sgu_b, layer, j):
    tok = lambda n: pl.BlockSpec((TM, n), lambda i: (i, 0))
    return pl.pallas_call(
        _odd_in_kernel,
        grid=(N_TILES,),
        in_specs=[
            tok(D_MODEL),
            pl.BlockSpec((1, N_MOD, D_MODEL), lambda i: (_cond_of_tile(i), 0, 0)),
            pl.BlockSpec((None, None, 1, D_MODEL), lambda i: (layer, 1, 0, 0)),
            pl.BlockSpec((None, D_MODEL, ODD_IN), lambda i: (j, 0, 0)),
            pl.BlockSpec((1, SGU_CH), lambda i: (0, 0)),
            pl.BlockSpec((1, SGU_CH), lambda i: (0, 0)),
            pl.BlockSpec((None, SGU_GROUPS, CHUNK, CHUNK), lambda i: (j, 0, 0, 0)),
            pl.BlockSpec((CHUNK, SGU_GROUPS), lambda i: (0, 0)),
        ],
        out_specs=[tok(FOURIER_CH), tok(SGU_CH)],
        out_shape=[jax.ShapeDtypeStruct((NTOK, FOURIER_CH), BF16),
                   jax.ShapeDtypeStruct((NTOK, SGU_CH), BF16)],
        compiler_params=_cparams("parallel"),
        name="odd_in",
    )(x, mod_l, norm_g.reshape(DEPTH, 3, 1, D_MODEL), od_w_in,
      sgu_norm_g[j].reshape(1, SGU_CH), sgu_norm_b[j].reshape(1, SGU_CH), sgu_w, sgu_b[j].T)


def _fourier_out_kernel(f_ref, wch_ref, wseq_p_ref, wseq_s_ref, res_ref, mod_ref, sgu_ref, wo_ref,
                        o_ref, four_scr):
    i = pl.program_id(0)
    pq = _dot(f_ref[...], wch_ref[...].astype(BF16)).astype(BF16)

    def seq_dft(rows, wseq, seq):
        stacked = jnp.concatenate([pq[rows, :FOURIER_CH], pq[rows, FOURIER_CH:]], axis=0)
        scale = 1.0 / np.sqrt(float(seq * FOURIER_GROUP_CH))
        four_scr[rows, :] = (_dot(wseq.astype(BF16), stacked) * scale).astype(BF16)

    @pl.when(i < N_TILES_P)
    def _():
        for s in range(SEQ_PER_TILE):
            seq_dft(slice(s * SEQ, (s + 1) * SEQ), wseq_p_ref[...], SEQ)

    @pl.when(i >= N_TILES_P)
    def _():
        seq_dft(slice(0, DEC_SEQ), wseq_s_ref[...], DEC_SEQ)

    w_four = wo_ref[:FOURIER_CH, :].astype(BF16)
    w_sgu = wo_ref[FOURIER_CH:, :].astype(BF16)
    gate = mod_ref[0, 5:6, :]
    for s in range(SEQ_PER_TILE):
        rows = slice(s * SEQ, (s + 1) * SEQ)
        mixed = _dot(four_scr[rows, :], w_four) + _dot(sgu_ref[rows, :], w_sgu)
        o_ref[rows, :] = res_ref[rows, :] + gate * mixed


def _dft_constants():
    def cos_sin(n):
        idx = np.arange(n)
        ang = 2.0 * np.pi * ((idx[:, None] * idx[None, :]) % n) / n
        return np.cos(ang), np.sin(ang)

    cc, sc = cos_sin(FOURIER_GROUP_CH)
    eye = np.eye(FOURIER_GROUPS)
    wch = np.concatenate([np.kron(eye, cc), np.kron(eye, sc)], axis=1)

    def wseq(seq):
        cl, sl = cos_sin(seq)
        return jnp.asarray(np.concatenate([cl, -sl], axis=1), dtype=F32)

    return jnp.asarray(wch, dtype=F32), wseq(SEQ), wseq(DEC_SEQ)


def _fourier_out(x, mod_l, f, sgu, w_out, j):
    wch, wseq_p, wseq_s = _dft_constants()
    tok = lambda n: pl.BlockSpec((TM, n), lambda i: (i, 0))
    const = lambda shape: pl.BlockSpec(shape, lambda i: (0,) * len(shape), pipeline_mode=pl.Buffered(1))
    return pl.pallas_call(
        _fourier_out_kernel,
        grid=(N_TILES,),
        in_specs=[
            tok(FOURIER_CH), const(wch.shape), const(wseq_p.shape), const(wseq_s.shape),
            tok(D_MODEL),
            pl.BlockSpec((1, N_MOD, D_MODEL), lambda i: (_cond_of_tile(i), 0, 0)),
            tok(SGU_CH),
            pl.BlockSpec((None, FOURIER_CH + SGU_CH, D_MODEL), lambda i: (j, 0, 0),
                         pipeline_mode=pl.Buffered(1)),
        ],
        out_specs=tok(D_MODEL),
        out_shape=jax.ShapeDtypeStruct((NTOK, D_MODEL), F32),
        scratch_shapes=[pltpu.VMEM((TM, FOURIER_CH), BF16)],
        compiler_params=_cparams("parallel"),
        name="fourier_out",
    )(f, wch, wseq_p, wseq_s, x, mod_l, sgu, w_out)


def kernel(x_prompt, x_sample, cache_k, cache_v, c, c_ctx, w_mod, b_mod, norm_g, ffn_w_in, ffn_w_out, ev_w_in, ev_w_out, q_norm_g, k_norm_g, conv_w, conv_b, conv_norm_g, conv_norm_b, od_w_in, od_w_out, sgu_norm_g, sgu_norm_b, sgu_w, sgu_b):
    x = jnp.concatenate([x_prompt.reshape(NTOK_P, D_MODEL), x_sample.reshape(NTOK_S, D_MODEL)], axis=0)
    cond = jnp.concatenate([c_ctx[None, :], c_ctx[None, :], c,
                            jnp.zeros((N_COND - COND_LATENT0 - DEC_BATCH, D_MODEL), F32)], axis=0)
    mod = _adaln(cond, w_mod, b_mod).reshape(DEPTH, N_COND, N_MOD, D_MODEL)
    ck = cache_k.reshape(DEC_BATCH, -1, PAST_LEN, ATT_KV)
    cv = cache_v.reshape(DEC_BATCH, -1, PAST_LEN, ATT_KV)

    new_k, new_v = [], []
    for layer in range(DEPTH):
        mod_l = mod[layer]
        j = layer // 2
        x = _ffn(x, mod_l, norm_g, ffn_w_in, ffn_w_out, layer, 0)
        if layer % 2 == 0:
            q, k, v, glu = _even_in(x, mod_l, norm_g, ev_w_in, q_norm_g, k_norm_g, layer, j)
            new_k.append(k[:NTOK_P].reshape(BATCH, SEQ, N_KV_HEADS, HEAD_DIM))
            new_v.append(v[:NTOK_P].reshape(BATCH, SEQ, N_KV_HEADS, HEAD_DIM))
            attn = _attention(q, k, v, ck, cv, j)
            x = _conv_out(x, mod_l, attn, glu, conv_w, conv_b, conv_norm_g, conv_norm_b, ev_w_out, j)
        else:
            f, sgu = _odd_in(x, mod_l, norm_g, od_w_in, sgu_norm_g, sgu_norm_b, sgu_w, sgu_b, layer, j)
            x = _fourier_out(x, mod_l, f, sgu, od_w_out, j)
        x = _ffn(x, mod_l, norm_g, ffn_w_in, ffn_w_out, layer, 1)

    y_prompt = x[:NTOK_P].reshape(BATCH, SEQ, D_MODEL)
    y_sample = x[NTOK_P:].reshape(DEC_BATCH, DEC_SEQ, D_MODEL)
    return y_prompt, y_sample, jnp.stack(new_k, axis=1), jnp.stack(new_v, axis=1)
```

```python
import functools

import numpy as np
import jax
import jax.numpy as jnp
from jax import lax
from jax.experimental import pallas as pl
from jax.experimental.pallas import tpu as pltpu

D_MODEL = 1024
BATCH = 32
SEQ = 256
DEPTH = 4
DEC_BATCH = 4
DEC_SEQ = 1024
PAST_LEN = 512
GRID_W = 64
N_HEADS = 8
N_KV_HEADS = 2
HEAD_DIM = 64
GQA_GROUP = N_HEADS // N_KV_HEADS
AXIS_ROPE_DIM = HEAD_DIM // 2
ROPE_THETA = 10000.0
ATT_Q = N_HEADS * HEAD_DIM
ATT_KV = N_KV_HEADS * HEAD_DIM
CONV_CH = 512
CONV_K = 31
FOURIER_GROUPS = 4
FOURIER_GROUP_CH = 128
FOURIER_CH = FOURIER_GROUPS * FOURIER_GROUP_CH
SGU_GROUPS = 4
SGU_GROUP_CH = 128
SGU_CH = SGU_GROUPS * SGU_GROUP_CH
CHUNK = 128
FFN_DIM = 2816
N_MOD = 9
EVEN_IN = ATT_Q + 2 * ATT_KV + 2 * CONV_CH
ODD_IN = FOURIER_CH + 2 * SGU_CH
RMS_EPS = 1e-6
LN_EPS = 1e-5

BF16 = jnp.bfloat16
F32 = jnp.float32

V7X_SUBLANES = 8
V7X_LANES = 128
NTOK_P = BATCH * SEQ
NTOK_S = DEC_BATCH * DEC_SEQ
NTOK = NTOK_P + NTOK_S
TM = DEC_SEQ
N_TILES = NTOK // TM
N_TILES_P = NTOK_P // TM
SEQ_PER_TILE = TM // SEQ
N_COND = 8
COND_LATENT0 = 2
FFN_TM = 2 * TM
FFN_CHUNK = 256
N_FFN_CHUNKS = FFN_DIM // FFN_CHUNK
EVEN_IN_ROWS = 512
ADALN_TN = 1024
CONV_ROWS = 128
CONV_HALO = 16
ATT_TQ = SEQ
V7X_VMEM_LIMIT_BYTES = 56 * 1024 * 1024


def _cparams(*sem):
    return pltpu.CompilerParams(dimension_semantics=sem, vmem_limit_bytes=V7X_VMEM_LIMIT_BYTES)


def _cond_of_tile(i, rows=TM):
    return jnp.maximum((i * rows - NTOK_P) // DEC_SEQ + COND_LATENT0, 0)


def _dot(a, b):
    return jnp.dot(a, b, preferred_element_type=F32)


def _modulate(x, g, shift, scale):
    ms = jnp.mean(x * x, axis=-1, keepdims=True)
    return (x * lax.rsqrt(ms + RMS_EPS)) * (g * (1.0 + scale)) + shift


def _layer_norm(x, g, b):
    mu = jnp.mean(x, axis=-1, keepdims=True)
    xc = x - mu
    var = jnp.mean(xc * xc, axis=-1, keepdims=True)
    return xc * lax.rsqrt(var + LN_EPS) * g + b


def _adaln_kernel(c_ref, w_ref, b_ref, o_ref):
    c = c_ref[...]
    s = (c * jax.nn.sigmoid(c)).astype(BF16)
    o_ref[...] = _dot(s, w_ref[...].astype(BF16)) + b_ref[...]


def _adaln(cond, w_mod, b_mod):
    n_out = N_MOD * D_MODEL
    return pl.pallas_call(
        _adaln_kernel,
        grid=(DEPTH, n_out // ADALN_TN),
        in_specs=[
            pl.BlockSpec((N_COND, D_MODEL), lambda l, n: (0, 0)),
            pl.BlockSpec((None, D_MODEL, ADALN_TN), lambda l, n: (l, 0, n)),
            pl.BlockSpec((None, 1, ADALN_TN), lambda l, n: (l, 0, n)),
        ],
        out_specs=pl.BlockSpec((None, N_COND, ADALN_TN), lambda l, n: (l, 0, n)),
        out_shape=jax.ShapeDtypeStruct((DEPTH, N_COND, n_out), F32),
        compiler_params=_cparams("parallel", "parallel"),
        name="adaln",
    )(cond, w_mod, b_mod.reshape(DEPTH, 1, n_out))


def _ffn_kernel(sub, x_ref, mod_ref, g_ref, wg_ref, wu_ref, wo_ref, o_ref, h_scr):
    j = pl.program_id(1)
    halves = [(c, slice(c * TM, (c + 1) * TM)) for c in range(FFN_TM // TM)]

    @pl.when(j == 0)
    def _():
        for c, rows in halves:
            h = _modulate(x_ref[rows, :], g_ref[...], mod_ref[c, 3 * sub:3 * sub + 1, :],
                          mod_ref[c, 3 * sub + 1:3 * sub + 2, :])
            h_scr[rows, :] = h.astype(BF16)
        o_ref[...] = jnp.zeros_like(o_ref)

    h = h_scr[...]
    gate = _dot(h, wg_ref[...].astype(BF16))
    up = _dot(h, wu_ref[...].astype(BF16))
    a = (gate * jax.nn.sigmoid(gate) * up).astype(BF16)
    o_ref[...] += _dot(a, wo_ref[...].astype(BF16))

    @pl.when(j == N_FFN_CHUNKS - 1)
    def _():
        for c, rows in halves:
            o_ref[rows, :] = x_ref[rows, :] + (0.5 * mod_ref[c, 3 * sub + 2:3 * sub + 3, :]) * o_ref[rows, :]


def _ffn(x, mod_l, norm_g, ffn_w_in, ffn_w_out, layer, which):
    sub = 2 * which
    per = FFN_TM // TM
    return pl.pallas_call(
        functools.partial(_ffn_kernel, sub),
        grid=(NTOK // FFN_TM, N_FFN_CHUNKS),
        in_specs=[
            pl.BlockSpec((FFN_TM, D_MODEL), lambda i, j: (i, 0)),
            pl.BlockSpec((per, N_MOD, D_MODEL), lambda i, j: (_cond_of_tile(i, FFN_TM) // per, 0, 0)),
            pl.BlockSpec((None, None, 1, D_MODEL), lambda i, j: (layer, sub, 0, 0)),
            pl.BlockSpec((None, None, D_MODEL, FFN_CHUNK), lambda i, j: (layer, which, 0, j)),
            pl.BlockSpec((None, None, D_MODEL, FFN_CHUNK),
                         lambda i, j: (layer, which, 0, N_FFN_CHUNKS + j)),
            pl.BlockSpec((None, None, FFN_CHUNK, D_MODEL), lambda i, j: (layer, which, j, 0)),
        ],
        out_specs=pl.BlockSpec((FFN_TM, D_MODEL), lambda i, j: (i, 0)),
        out_shape=jax.ShapeDtypeStruct((NTOK, D_MODEL), F32),
        scratch_shapes=[pltpu.VMEM((FFN_TM, D_MODEL), BF16)],
        compiler_params=_cparams("parallel", "arbitrary"),
        name="ffn",
    )(x, mod_l, norm_g.reshape(DEPTH, 3, 1, D_MODEL), ffn_w_in, ffn_w_in, ffn_w_out)


def _group_mean_sq(x, width):
    r = lax.broadcasted_iota(jnp.int32, (width, width), 0) // HEAD_DIM
    c = lax.broadcasted_iota(jnp.int32, (width, width), 1) // HEAD_DIM
    ones = (r == c).astype(BF16)
    return _dot((x * x).astype(BF16), ones) * (1.0 / HEAD_DIM)


def _head_norm_rope(x, gain, cos, sin_next, sin_prev):
    width = x.shape[-1]
    xn = x * lax.rsqrt(_group_mean_sq(x, width) + RMS_EPS) * gain
    nxt = pltpu.roll(xn, width - 1, 1)
    prv = pltpu.roll(xn, 1, 1)
    return xn * cos + nxt * sin_next + prv * sin_prev


def _even_in_kernel(x_ref, mod_ref, g_ref, w_ref, qg_ref, kg_ref,
                    cq_ref, snq_ref, spq_ref, ck_ref, snk_ref, spk_ref,
                    q_ref, k_ref, v_ref, glu_ref):
    h = _modulate(x_ref[...], g_ref[...], mod_ref[0, 3:4, :], mod_ref[0, 4:5, :]).astype(BF16)

    def proj(lo, hi):
        return _dot(h, w_ref[:, lo:hi].astype(BF16))

    q = proj(0, ATT_Q)
    q = _head_norm_rope(q, qg_ref[...], cq_ref[...], snq_ref[...], spq_ref[...])
    q_ref[...] = (q * (HEAD_DIM ** -0.5)).astype(BF16)
    k = proj(ATT_Q, ATT_Q + ATT_KV)
    k_ref[...] = _head_norm_rope(k, kg_ref[...], ck_ref[...], snk_ref[...], spk_ref[...])
    v_ref[...] = proj(ATT_Q + ATT_KV, ATT_Q + 2 * ATT_KV)
    a0 = ATT_Q + 2 * ATT_KV
    a = proj(a0, a0 + CONV_CH)
    gt = proj(a0 + CONV_CH, a0 + 2 * CONV_CH)
    glu_ref[...] = a * jax.nn.sigmoid(gt)


def _rope_tables():
    t = np.arange(DEC_SEQ)
    row = (t // GRID_W).astype(np.float64)
    col = (t % GRID_W).astype(np.float64)
    inv_freq = 1.0 / (ROPE_THETA ** (np.arange(0, AXIS_ROPE_DIM, 2, dtype=np.float64) / AXIS_ROPE_DIM))
    ang = np.concatenate([row[:, None] * inv_freq, col[:, None] * inv_freq], axis=-1)
    ang = np.repeat(ang, 2, axis=-1)
    even = (np.arange(HEAD_DIM) % 2 == 0)[None, :]
    cos = np.cos(ang)
    sin_next = np.where(even, -np.sin(ang), 0.0)
    sin_prev = np.where(even, 0.0, np.sin(ang))

    def both(tab):
        ident = np.ones_like(tab) if tab is cos else np.zeros_like(tab)
        return np.stack([np.tile(ident, (1, N_HEADS)), np.tile(tab, (1, N_HEADS))]).astype(np.float32)

    return both(cos), both(sin_next), both(sin_prev)


def _even_in(x, mod_l, norm_g, ev_w_in, q_norm_g, k_norm_g, layer, j):
    cos, sin_next, sin_prev = (jnp.asarray(t) for t in _rope_tables())
    qg = jnp.tile(q_norm_g[j], N_HEADS).reshape(1, ATT_Q)
    kg = jnp.tile(k_norm_g[j], N_KV_HEADS).reshape(1, ATT_KV)
    rows = EVEN_IN_ROWS
    per_seq = DEC_SEQ // rows
    first_latent = NTOK_P // rows
    tab_idx = lambda i: (jnp.minimum(i // first_latent, 1), i % per_seq, 0)
    tab_q = pl.BlockSpec((None, rows, ATT_Q), tab_idx)
    tab_k = pl.BlockSpec((None, rows, ATT_KV), tab_idx)
    tok = lambda n: pl.BlockSpec((rows, n), lambda i: (i, 0))
    return pl.pallas_call(
        _even_in_kernel,
        grid=(NTOK // rows,),
        in_specs=[
            tok(D_MODEL),
            pl.BlockSpec((1, N_MOD, D_MODEL), lambda i: (_cond_of_tile(i, rows), 0, 0)),
            pl.BlockSpec((None, None, 1, D_MODEL), lambda i: (layer, 1, 0, 0)),
            pl.BlockSpec((None, D_MODEL, EVEN_IN), lambda i: (j, 0, 0)),
            pl.BlockSpec((1, ATT_Q), lambda i: (0, 0)),
            pl.BlockSpec((1, ATT_KV), lambda i: (0, 0)),
            tab_q, tab_q, tab_q, tab_k, tab_k, tab_k,
        ],
        out_specs=[tok(ATT_Q), tok(ATT_KV), tok(ATT_KV), tok(CONV_CH)],
        out_shape=[
            jax.ShapeDtypeStruct((NTOK, ATT_Q), BF16),
            jax.ShapeDtypeStruct((NTOK, ATT_KV), F32),
            jax.ShapeDtypeStruct((NTOK, ATT_KV), F32),
            jax.ShapeDtypeStruct((NTOK, CONV_CH), F32),
        ],
        compiler_params=_cparams("parallel"),
        name="even_in",
    )(x, mod_l, norm_g.reshape(DEPTH, 3, 1, D_MODEL), ev_w_in, qg, kg,
      cos, sin_next, sin_prev, cos, sin_next, sin_prev)


ATT_WIDTH = GQA_GROUP * HEAD_DIM


def _expand_kv_head(x, kv):
    lane = lax.broadcasted_iota(jnp.int32, x.shape, 1)
    swapped = pltpu.roll(x, HEAD_DIM, 1)
    pair = jnp.where((lane // HEAD_DIM) == kv, x, swapped)
    return jnp.concatenate([pair, pair], axis=1)


def _attend(q_ref, q_rows, kbig, vbig, n_keys, attn_scr):
    lane_group = lax.broadcasted_iota(jnp.int32, (ATT_TQ, ATT_WIDTH), 1) // HEAD_DIM
    for kv in range(N_KV_HEADS):
        cols = slice(kv * ATT_WIDTH, (kv + 1) * ATT_WIDTH)
        q = q_ref[q_rows, cols]
        kb = kbig[kv, 0:n_keys, :]
        vb = vbig[kv, 0:n_keys, :]
        acc = jnp.zeros((ATT_TQ, ATT_WIDTH), F32)
        for g in range(GQA_GROUP):
            mine = lane_group == g
            qm = jnp.where(mine, q, jnp.zeros_like(q))
            s = lax.dot_general(qm, kb, (((1,), (1,)), ((), ())), preferred_element_type=F32)
            m = jnp.max(s, axis=-1, keepdims=True)
            p = jnp.exp(s - m)
            denom = jnp.sum(p, axis=-1, keepdims=True)
            o = _dot(p.astype(BF16), vb)
            acc = jnp.where(mine, o / denom, acc)
        attn_scr[:, cols] = acc.astype(BF16)


CONV_PAD_SEQ = SEQ + 2 * CONV_HALO
CONV_WIN = ATT_TQ + 2 * CONV_HALO
CONV_SPAN = CONV_WIN - V7X_SUBLANES


def _conv_block(win_scr, w_ref, y_scr):
    off = CONV_HALO - CONV_K // 2
    for r0 in range(0, ATT_TQ, CONV_ROWS):
        for c0 in range(0, CONV_CH, V7X_LANES):
            cols = slice(c0, c0 + V7X_LANES)
            acc = jnp.zeros((CONV_ROWS, V7X_LANES), F32)
            for k in range(CONV_K):
                res = (off + k) % V7X_SUBLANES
                a = r0 + off + k - res
                acc = acc + win_scr[res, a:a + CONV_ROWS, cols] * w_ref[k:k + 1, cols]
            y_scr[r0:r0 + CONV_ROWS, cols] = acc


def _even_mix_kernel(q_ref, k_ref, v_ref, ck_ref, cv_ref, glu_ref, w_ref, b_ref, g_ref, beta_ref,
                     res_ref, mod_ref, wo_ref, o_ref,
                     kbig, vbig, pad_scr, win0_scr, win_scr, y_scr, attn_scr, wo_scr):
    i = pl.program_id(0)
    t = pl.program_id(1)
    rows = pl.ds(pl.multiple_of(t * ATT_TQ, ATT_TQ), ATT_TQ)
    border = jnp.zeros((CONV_HALO, CONV_CH), F32)

    @pl.when((i == 0) & (t == 0))
    def _():
        wo_scr[...] = wo_ref[...].astype(BF16)

    def expand(k, v, n):
        for kv in range(N_KV_HEADS):
            kbig[kv, 0:n, :] = _expand_kv_head(k, kv).astype(BF16)
            vbig[kv, 0:n, :] = _expand_kv_head(v, kv).astype(BF16)

    def mix(n_keys, win_start):
        win0_scr[...] = pad_scr[pl.ds(pl.multiple_of(win_start, V7X_SUBLANES), CONV_WIN), :]
        for res in range(V7X_SUBLANES):
            win_scr[res] = win0_scr[res:res + CONV_SPAN, :]
        _attend(q_ref, rows, kbig, vbig, n_keys, attn_scr)
        _conv_block(win_scr, w_ref, y_scr)
        y = _layer_norm(y_scr[...] + b_ref[...], g_ref[...], beta_ref[...])
        conv = (y * jax.nn.sigmoid(y)).astype(BF16)
        mixed = _dot(attn_scr[...], wo_scr[:ATT_Q, :]) + _dot(conv, wo_scr[ATT_Q:, :])
        o_ref[...] = res_ref[...] + mod_ref[0, 5:6, :] * mixed

    @pl.when(i < N_TILES_P)
    def _():
        @pl.when(t == 0)
        def _():
            for s in range(SEQ_PER_TILE):
                base = s * CONV_PAD_SEQ
                pad_scr[base:base + CONV_HALO, :] = border
                pad_scr[base + CONV_HALO:base + CONV_HALO + SEQ, :] = glu_ref[s * SEQ:(s + 1) * SEQ, :]
                pad_scr[base + CONV_HALO + SEQ:base + CONV_PAD_SEQ, :] = border

        expand(k_ref[rows, :], v_ref[rows, :], SEQ)
        mix(SEQ, t * CONV_PAD_SEQ)

    @pl.when(i >= N_TILES_P)
    def _():
        @pl.when(t == 0)
        def _():
            pad_scr[0:CONV_HALO, :] = border
            pad_scr[CONV_HALO:CONV_HALO + DEC_SEQ, :] = glu_ref[...]
            pad_scr[CONV_HALO + DEC_SEQ:2 * CONV_HALO + DEC_SEQ, :] = border
            expand(jnp.concatenate([ck_ref[...], k_ref[...]], axis=0),
                   jnp.concatenate([cv_ref[...], v_ref[...]], axis=0), PAST_LEN + DEC_SEQ)

        mix(PAST_LEN + DEC_SEQ, t * ATT_TQ)


def _even_mix(x, mod_l, q, k, v, ck, cv, glu, conv_w, conv_b, cn_g, cn_b, w_out, j):
    nq = TM // ATT_TQ
    tok = lambda n: pl.BlockSpec((TM, n), lambda i, t: (i, 0))
    blk = pl.BlockSpec((ATT_TQ, D_MODEL), lambda i, t: (i * nq + t, 0))
    ctx = pl.BlockSpec((None, None, PAST_LEN, ATT_KV), lambda i, t: (jnp.maximum(i - N_TILES_P, 0), j, 0, 0))
    vec = pl.BlockSpec((None, 1, CONV_CH), lambda i, t: (j, 0, 0))
    n_even = conv_b.shape[0]
    n_keys = PAST_LEN + DEC_SEQ
    return pl.pallas_call(
        _even_mix_kernel,
        grid=(N_TILES, nq),
        in_specs=[
            tok(ATT_Q), tok(ATT_KV), tok(ATT_KV), ctx, ctx, tok(CONV_CH),
            pl.BlockSpec((None, CONV_K, CONV_CH), lambda i, t: (j, 0, 0)),
            vec, vec, vec,
            blk,
            pl.BlockSpec((1, N_MOD, D_MODEL), lambda i, t: (_cond_of_tile(i), 0, 0)),
            pl.BlockSpec((None, ATT_Q + CONV_CH, D_MODEL), lambda i, t: (j, 0, 0),
                         pipeline_mode=pl.Buffered(1)),
        ],
        out_specs=blk,
        out_shape=jax.ShapeDtypeStruct((NTOK, D_MODEL), F32),
        scratch_shapes=[
            pltpu.VMEM((N_KV_HEADS, n_keys, ATT_WIDTH), BF16),
            pltpu.VMEM((N_KV_HEADS, n_keys, ATT_WIDTH), BF16),
            pltpu.VMEM((max(SEQ_PER_TILE * CONV_PAD_SEQ, DEC_SEQ + 2 * CONV_HALO), CONV_CH), F32),
            pltpu.VMEM((CONV_WIN, CONV_CH), F32),
            pltpu.VMEM((V7X_SUBLANES, CONV_SPAN, CONV_CH), F32),
            pltpu.VMEM((ATT_TQ, CONV_CH), F32),
            pltpu.VMEM((ATT_TQ, ATT_Q), BF16),
            pltpu.VMEM((ATT_Q + CONV_CH, D_MODEL), BF16),
        ],
        compiler_params=_cparams("arbitrary", "arbitrary"),
        name="even_mix",
    )(q, k, v, ck, cv, glu, conv_w, conv_b.reshape(n_even, 1, CONV_CH),
      cn_g.reshape(n_even, 1, CONV_CH), cn_b.reshape(n_even, 1, CONV_CH), x, mod_l, w_out)


def _odd_in_kernel(x_ref, mod_ref, g_ref, w_ref, sg_ref, sb_ref, ws_ref, bs_ref, f_ref, sgu_ref):
    h = _modulate(x_ref[...], g_ref[...], mod_ref[0, 3:4, :], mod_ref[0, 4:5, :]).astype(BF16)

    def proj(lo, hi):
        return _dot(h, w_ref[:, lo:hi].astype(BF16))

    f_ref[...] = proj(0, FOURIER_CH).astype(BF16)
    u = proj(FOURIER_CH, FOURIER_CH + SGU_CH)
    v = _layer_norm(proj(FOURIER_CH + SGU_CH, ODD_IN), sg_ref[...], sb_ref[...]).astype(BF16)
    for g in range(SGU_GROUPS):
        wg = ws_ref[g].astype(BF16)
        bias = bs_ref[:, g:g + 1]
        cols = slice(g * SGU_GROUP_CH, (g + 1) * SGU_GROUP_CH)
        for n in range(TM // CHUNK):
            rows = slice(n * CHUNK, (n + 1) * CHUNK)
            mixed = _dot(wg, v[rows, cols]) + bias
            sgu_ref[rows, cols] = (u[rows, cols] * mixed).astype(BF16)


def _odd_in(x, mod_l, norm_g, od_w_in, sgu_norm_g, sgu_norm_b, sgu_w, sgu_b, layer, j):
    tok = lambda n: pl.BlockSpec((TM, n), lambda i: (i, 0))
    return pl.pallas_call(
        _odd_in_kernel,
        grid=(N_TILES,),
        in_specs=[
            tok(D_MODEL),
            pl.BlockSpec((1, N_MOD, D_MODEL), lambda i: (_cond_of_tile(i), 0, 0)),
            pl.BlockSpec((None, None, 1, D_MODEL), lambda i: (layer, 1, 0, 0)),
            pl.BlockSpec((None, D_MODEL, ODD_IN), lambda i: (j, 0, 0)),
            pl.BlockSpec((1, SGU_CH), lambda i: (0, 0)),
            pl.BlockSpec((1, SGU_CH), lambda i: (0, 0)),
            pl.BlockSpec((None, SGU_GROUPS, CHUNK, CHUNK), lambda i: (j, 0, 0, 0)),
            pl.BlockSpec((CHUNK, SGU_GROUPS), lambda i: (0, 0)),
        ],
        out_specs=[tok(FOURIER_CH), tok(SGU_CH)],
        out_shape=[jax.ShapeDtypeStruct((NTOK, FOURIER_CH), BF16),
                   jax.ShapeDtypeStruct((NTOK, SGU_CH), BF16)],
        compiler_params=_cparams("parallel"),
        name="odd_in",
    )(x, mod_l, norm_g.reshape(DEPTH, 3, 1, D_MODEL), od_w_in,
      sgu_norm_g[j].reshape(1, SGU_CH), sgu_norm_b[j].reshape(1, SGU_CH), sgu_w, sgu_b[j].T)


def _fourier_out_kernel(f_ref, wch_ref, wseq_p_ref, wseq_s_ref, res_ref, mod_ref, sgu_ref, wo_ref,
                        o_ref, four_scr):
    i = pl.program_id(0)
    pq = _dot(f_ref[...], wch_ref[...].astype(BF16)).astype(BF16)

    def seq_dft(rows, wseq, seq):
        stacked = jnp.concatenate([pq[rows, :FOURIER_CH], pq[rows, FOURIER_CH:]], axis=0)
        scale = 1.0 / np.sqrt(float(seq * FOURIER_GROUP_CH))
        four_scr[rows, :] = (_dot(wseq.astype(BF16), stacked) * scale).astype(BF16)

    @pl.when(i < N_TILES_P)
    def _():
        for s in range(SEQ_PER_TILE):
            seq_dft(slice(s * SEQ, (s + 1) * SEQ), wseq_p_ref[...], SEQ)

    @pl.when(i >= N_TILES_P)
    def _():
        seq_dft(slice(0, DEC_SEQ), wseq_s_ref[...], DEC_SEQ)

    w_four = wo_ref[:FOURIER_CH, :].astype(BF16)
    w_sgu = wo_ref[FOURIER_CH:, :].astype(BF16)
    gate = mod_ref[0, 5:6, :]
    for s in range(SEQ_PER_TILE):
        rows = slice(s * SEQ, (s + 1) * SEQ)
        mixed = _dot(four_scr[rows, :], w_four) + _dot(sgu_ref[rows, :], w_sgu)
        o_ref[rows, :] = res_ref[rows, :] + gate * mixed


def _dft_constants():
    def cos_sin(n):
        idx = np.arange(n)
        ang = 2.0 * np.pi * ((idx[:, None] * idx[None, :]) % n) / n
        return np.cos(ang), np.sin(ang)

    cc, sc = cos_sin(FOURIER_GROUP_CH)
    eye = np.eye(FOURIER_GROUPS)
    wch = np.concatenate([np.kron(eye, cc), np.kron(eye, sc)], axis=1)

    def wseq(seq):
        cl, sl = cos_sin(seq)
        return jnp.asarray(np.concatenate([cl, -sl], axis=1), dtype=F32)

    return jnp.asarray(wch, dtype=F32), wseq(SEQ), wseq(DEC_SEQ)


def _fourier_out(x, mod_l, f, sgu, w_out, j):
    wch, wseq_p, wseq_s = _dft_constants()
    tok = lambda n: pl.BlockSpec((TM, n), lambda i: (i, 0))
    const = lambda shape: pl.BlockSpec(shape, lambda i: (0,) * len(shape), pipeline_mode=pl.Buffered(1))
    return pl.pallas_call(
        _fourier_out_kernel,
        grid=(N_TILES,),
        in_specs=[
            tok(FOURIER_CH), const(wch.shape), const(wseq_p.shape), const(wseq_s.shape),
            tok(D_MODEL),
            pl.BlockSpec((1, N_MOD, D_MODEL), lambda i: (_cond_of_tile(i), 0, 0)),
            tok(SGU_CH),
            pl.BlockSpec((None, FOURIER_CH + SGU_CH, D_MODEL), lambda i: (j, 0, 0),
                         pipeline_mode=pl.Buffered(1)),
        ],
        out_specs=tok(D_MODEL),
        out_shape=jax.ShapeDtypeStruct((NTOK, D_MODEL), F32),
        scratch_shapes=[pltpu.VMEM((TM, FOURIER_CH), BF16)],
        compiler_params=_cparams("parallel"),
        name="fourier_out",
    )(f, wch, wseq_p, wseq_s, x, mod_l, sgu, w_out)


def kernel(x_prompt, x_sample, cache_k, cache_v, c, c_ctx, w_mod, b_mod, norm_g, ffn_w_in, ffn_w_out, ev_w_in, ev_w_out, q_norm_g, k_norm_g, conv_w, conv_b, conv_norm_g, conv_norm_b, od_w_in, od_w_out, sgu_norm_g, sgu_norm_b, sgu_w, sgu_b):
    x = jnp.concatenate([x_prompt.reshape(NTOK_P, D_MODEL), x_sample.reshape(NTOK_S, D_MODEL)], axis=0)
    cond = jnp.concatenate([c_ctx[None, :], c_ctx[None, :], c,
                            jnp.zeros((N_COND - COND_LATENT0 - DEC_BATCH, D_MODEL), F32)], axis=0)
    mod = _adaln(cond, w_mod, b_mod).reshape(DEPTH, N_COND, N_MOD, D_MODEL)
    ck = cache_k.reshape(DEC_BATCH, -1, PAST_LEN, ATT_KV)
    cv = cache_v.reshape(DEC_BATCH, -1, PAST_LEN, ATT_KV)

    new_k, new_v = [], []
    for layer in range(DEPTH):
        mod_l = mod[layer]
        j = layer // 2
        x = _ffn(x, mod_l, norm_g, ffn_w_in, ffn_w_out, layer, 0)
        if layer % 2 == 0:
            q, k, v, glu = _even_in(x, mod_l, norm_g, ev_w_in, q_norm_g, k_norm_g, layer, j)
            new_k.append(k[:NTOK_P].reshape(BATCH, SEQ, N_KV_HEADS, HEAD_DIM))
            new_v.append(v[:NTOK_P].reshape(BATCH, SEQ, N_KV_HEADS, HEAD_DIM))
            x = _even_mix(x, mod_l, q, k, v, ck, cv, glu, conv_w, conv_b, conv_norm_g, conv_norm_b,
                          ev_w_out, j)
        else:
            f, sgu = _odd_in(x, mod_l, norm_g, od_w_in, sgu_norm_g, sgu_norm_b, sgu_w, sgu_b, layer, j)
            x = _fourier_out(x, mod_l, f, sgu, od_w_out, j)
        x = _ffn(x, mod_l, norm_g, ffn_w_in, ffn_w_out, layer, 1)

    y_prompt = x[:NTOK_P].reshape(BATCH, SEQ, D_MODEL)
    y_sample = x[NTOK_P:].reshape(DEC_BATCH, DEC_SEQ, D_MODEL)
    return y_prompt, y_sample, jnp.stack(new_k, axis=1), jnp.stack(new_v, axis=1)
```

```python
import functools

import numpy as np
import jax
import jax.numpy as jnp
from jax import lax
from jax.experimental import pallas as pl
from jax.experimental.pallas import tpu as pltpu

D_MODEL = 1024
BATCH = 32
SEQ = 256
DEPTH = 4
DEC_BATCH = 4
DEC_SEQ = 1024
PAST_LEN = 512
GRID_W = 64
N_HEADS = 8
N_KV_HEADS = 2
HEAD_DIM = 64
GQA_GROUP = N_HEADS // N_KV_HEADS
AXIS_ROPE_DIM = HEAD_DIM // 2
ROPE_THETA = 10000.0
ATT_Q = N_HEADS * HEAD_DIM
ATT_KV = N_KV_HEADS * HEAD_DIM
CONV_CH = 512
CONV_K = 31
FOURIER_GROUPS = 4
FOURIER_GROUP_CH = 128
FOURIER_CH = FOURIER_GROUPS * FOURIER_GROUP_CH
SGU_GROUPS = 4
SGU_GROUP_CH = 128
SGU_CH = SGU_GROUPS * SGU_GROUP_CH
CHUNK = 128
FFN_DIM = 2816
N_MOD = 9
EVEN_IN = ATT_Q + 2 * ATT_KV + 2 * CONV_CH
ODD_IN = FOURIER_CH + 2 * SGU_CH
RMS_EPS = 1e-6
LN_EPS = 1e-5
LOG2_E = 1.4426950408889634

BF16 = jnp.bfloat16
F32 = jnp.float32

V7X_SUBLANES = 8
V7X_LANES = 128
NTOK_P = BATCH * SEQ
NTOK_S = DEC_BATCH * DEC_SEQ
NTOK = NTOK_P + NTOK_S
TM = DEC_SEQ
N_TILES = NTOK // TM
N_TILES_P = NTOK_P // TM
SEQ_PER_TILE = TM // SEQ
N_COND = 8
COND_LATENT0 = 2
FFN_TM = 2 * TM
FFN_CHUNK = 256
N_FFN_CHUNKS = FFN_DIM // FFN_CHUNK
EVEN_IN_ROWS = 512
ADALN_TN = 1024
CONV_ROWS = 128
CONV_HALO = 16
ATT_TQ = SEQ
V7X_VMEM_LIMIT_BYTES = 56 * 1024 * 1024


def _cparams(*sem):
    return pltpu.CompilerParams(dimension_semantics=sem, vmem_limit_bytes=V7X_VMEM_LIMIT_BYTES)


def _cond_of_tile(i, rows=TM):
    return jnp.maximum((i * rows - NTOK_P) // DEC_SEQ + COND_LATENT0, 0)


def _dot(a, b):
    return jnp.dot(a, b, preferred_element_type=F32)


def _modulate(x, g, shift, scale):
    ms = jnp.mean(x * x, axis=-1, keepdims=True)
    return (x * lax.rsqrt(ms + RMS_EPS)) * (g * (1.0 + scale)) + shift


def _layer_norm(x, g, b):
    mu = jnp.mean(x, axis=-1, keepdims=True)
    xc = x - mu
    var = jnp.mean(xc * xc, axis=-1, keepdims=True)
    return xc * lax.rsqrt(var + LN_EPS) * g + b


def _adaln_kernel(c_ref, w_ref, b_ref, o_ref):
    c = c_ref[...]
    s = (c * jax.nn.sigmoid(c)).astype(BF16)
    o_ref[...] = _dot(s, w_ref[...].astype(BF16)) + b_ref[...]


def _adaln(cond, w_mod, b_mod):
    n_out = N_MOD * D_MODEL
    return pl.pallas_call(
        _adaln_kernel,
        grid=(DEPTH, n_out // ADALN_TN),
        in_specs=[
            pl.BlockSpec((N_COND, D_MODEL), lambda l, n: (0, 0)),
            pl.BlockSpec((None, D_MODEL, ADALN_TN), lambda l, n: (l, 0, n)),
            pl.BlockSpec((None, 1, ADALN_TN), lambda l, n: (l, 0, n)),
        ],
        out_specs=pl.BlockSpec((None, N_COND, ADALN_TN), lambda l, n: (l, 0, n)),
        out_shape=jax.ShapeDtypeStruct((DEPTH, N_COND, n_out), F32),
        compiler_params=_cparams("parallel", "parallel"),
        name="adaln",
    )(cond, w_mod, b_mod.reshape(DEPTH, 1, n_out))


def _ffn_kernel(sub, x_ref, mod_ref, g_ref, wg_ref, wu_ref, wo_ref, o_ref, h_scr):
    j = pl.program_id(1)
    halves = [(c, slice(c * TM, (c + 1) * TM)) for c in range(FFN_TM // TM)]

    @pl.when(j == 0)
    def _():
        for c, rows in halves:
            h = _modulate(x_ref[rows, :], g_ref[...], mod_ref[c, 3 * sub:3 * sub + 1, :],
                          mod_ref[c, 3 * sub + 1:3 * sub + 2, :])
            h_scr[rows, :] = h.astype(BF16)
        o_ref[...] = jnp.zeros_like(o_ref)

    h = h_scr[...]
    gate = _dot(h, wg_ref[...].astype(BF16))
    up = _dot(h, wu_ref[...].astype(BF16))
    a = (gate * jax.nn.sigmoid(gate) * up).astype(BF16)
    o_ref[...] += _dot(a, wo_ref[...].astype(BF16))

    @pl.when(j == N_FFN_CHUNKS - 1)
    def _():
        for c, rows in halves:
            o_ref[rows, :] = x_ref[rows, :] + (0.5 * mod_ref[c, 3 * sub + 2:3 * sub + 3, :]) * o_ref[rows, :]


def _ffn(x, mod_l, norm_g, ffn_w_in, ffn_w_out, layer, which, row0=0, n_rows=NTOK):
    sub = 2 * which
    per = FFN_TM // TM
    tile0 = row0 // FFN_TM
    wspec = pl.BlockSpec
    return pl.pallas_call(
        functools.partial(_ffn_kernel, sub),
        grid=(n_rows // FFN_TM, N_FFN_CHUNKS),
        in_specs=[
            pl.BlockSpec((FFN_TM, D_MODEL), lambda i, j: (tile0 + i, 0)),
            pl.BlockSpec((per, N_MOD, D_MODEL),
                         lambda i, j: (_cond_of_tile(tile0 + i, FFN_TM) // per, 0, 0)),
            pl.BlockSpec((None, None, 1, D_MODEL), lambda i, j: (layer, sub, 0, 0)),
            wspec((None, None, D_MODEL, FFN_CHUNK), lambda i, j: (layer, which, 0, j)),
            wspec((None, None, D_MODEL, FFN_CHUNK), lambda i, j: (layer, which, 0, N_FFN_CHUNKS + j)),
            wspec((None, None, FFN_CHUNK, D_MODEL), lambda i, j: (layer, which, j, 0)),
        ],
        out_specs=pl.BlockSpec((FFN_TM, D_MODEL), lambda i, j: (i, 0)),
        out_shape=jax.ShapeDtypeStruct((n_rows, D_MODEL), F32),
        scratch_shapes=[pltpu.VMEM((FFN_TM, D_MODEL), BF16)],
        compiler_params=_cparams("parallel", "arbitrary"),
        name="ffn",
    )(x, mod_l, norm_g.reshape(DEPTH, 3, 1, D_MODEL), ffn_w_in, ffn_w_in, ffn_w_out)


def _group_mean_sq(x, width):
    r = lax.broadcasted_iota(jnp.int32, (width, width), 0) // HEAD_DIM
    c = lax.broadcasted_iota(jnp.int32, (width, width), 1) // HEAD_DIM
    ones = (r == c).astype(BF16)
    return _dot((x * x).astype(BF16), ones) * (1.0 / HEAD_DIM)


def _head_norm_rope(x, gain, cos, sin_next, sin_prev):
    width = x.shape[-1]
    xn = x * lax.rsqrt(_group_mean_sq(x, width) + RMS_EPS) * gain
    nxt = pltpu.roll(xn, width - 1, 1)
    prv = pltpu.roll(xn, 1, 1)
    return xn * cos + nxt * sin_next + prv * sin_prev


def _even_in_kernel(x_ref, mod_ref, g_ref, w_ref, qg_ref, kg_ref,
                    cq_ref, snq_ref, spq_ref, ck_ref, snk_ref, spk_ref,
                    q_ref, k_ref, v_ref, glu_ref, kc_ref, vc_ref, w_scr):
    i = pl.program_id(0)

    @pl.when(i == 0)
    def _():
        w_scr[...] = w_ref[...].astype(BF16)

    h = _modulate(x_ref[...], g_ref[...], mod_ref[0, 3:4, :], mod_ref[0, 4:5, :]).astype(BF16)

    def proj(lo, hi):
        return _dot(h, w_scr[:, lo:hi])

    q = proj(0, ATT_Q)
    q = _head_norm_rope(q, qg_ref[...], cq_ref[...], snq_ref[...], spq_ref[...])
    q_ref[...] = (q * (HEAD_DIM ** -0.5 * LOG2_E)).astype(BF16)
    k = proj(ATT_Q, ATT_Q + ATT_KV)
    k = _head_norm_rope(k, kg_ref[...], ck_ref[...], snk_ref[...], spk_ref[...])
    v = proj(ATT_Q + ATT_KV, ATT_Q + 2 * ATT_KV)
    k_ref[...] = k
    v_ref[...] = v

    @pl.when(i < NTOK_P // EVEN_IN_ROWS)
    def _():
        kc_ref[...] = k
        vc_ref[...] = v

    a0 = ATT_Q + 2 * ATT_KV
    a = proj(a0, a0 + CONV_CH)
    gt = proj(a0 + CONV_CH, a0 + 2 * CONV_CH)
    glu_ref[...] = a * jax.nn.sigmoid(gt)


def _rope_tables():
    t = np.arange(DEC_SEQ)
    row = (t // GRID_W).astype(np.float64)
    col = (t % GRID_W).astype(np.float64)
    inv_freq = 1.0 / (ROPE_THETA ** (np.arange(0, AXIS_ROPE_DIM, 2, dtype=np.float64) / AXIS_ROPE_DIM))
    ang = np.concatenate([row[:, None] * inv_freq, col[:, None] * inv_freq], axis=-1)
    ang = np.repeat(ang, 2, axis=-1)
    even = (np.arange(HEAD_DIM) % 2 == 0)[None, :]
    cos = np.cos(ang)
    sin_next = np.where(even, -np.sin(ang), 0.0)
    sin_prev = np.where(even, 0.0, np.sin(ang))

    def both(tab):
        ident = np.ones_like(tab) if tab is cos else np.zeros_like(tab)
        return np.stack([np.tile(ident, (1, N_HEADS)), np.tile(tab, (1, N_HEADS))]).astype(np.float32)

    return both(cos), both(sin_next), both(sin_prev)


def _even_in(x, mod_l, norm_g, ev_w_in, q_norm_g, k_norm_g, layer, j):
    cos, sin_next, sin_prev = (jnp.asarray(t) for t in _rope_tables())
    qg = jnp.tile(q_norm_g[j], N_HEADS).reshape(1, ATT_Q)
    kg = jnp.tile(k_norm_g[j], N_KV_HEADS).reshape(1, ATT_KV)
    rows = EVEN_IN_ROWS
    per_seq = DEC_SEQ // rows
    first_latent = NTOK_P // rows
    tab_idx = lambda i: (jnp.minimum(i // first_latent, 1), i % per_seq, 0)
    tab_q = pl.BlockSpec((None, rows, ATT_Q), tab_idx)
    tab_k = pl.BlockSpec((None, rows, ATT_KV), tab_idx)
    tok = lambda n: pl.BlockSpec((rows, n), lambda i: (i, 0))
    cache = pl.BlockSpec((rows, ATT_KV), lambda i: (jnp.minimum(i, first_latent - 1), 0))
    return pl.pallas_call(
        _even_in_kernel,
        grid=(NTOK // rows,),
        in_specs=[
            tok(D_MODEL),
            pl.BlockSpec((1, N_MOD, D_MODEL), lambda i: (_cond_of_tile(i, rows), 0, 0)),
            pl.BlockSpec((None, None, 1, D_MODEL), lambda i: (layer, 1, 0, 0)),
            pl.BlockSpec((None, D_MODEL, EVEN_IN), lambda i: (j, 0, 0), pipeline_mode=pl.Buffered(1)),
            pl.BlockSpec((1, ATT_Q), lambda i: (0, 0)),
            pl.BlockSpec((1, ATT_KV), lambda i: (0, 0)),
            tab_q, tab_q, tab_q, tab_k, tab_k, tab_k,
        ],
        out_specs=[tok(ATT_Q), tok(ATT_KV), tok(ATT_KV), tok(CONV_CH), cache, cache],
        out_shape=[
            jax.ShapeDtypeStruct((NTOK, ATT_Q), BF16),
            jax.ShapeDtypeStruct((NTOK, ATT_KV), F32),
            jax.ShapeDtypeStruct((NTOK, ATT_KV), F32),
            jax.ShapeDtypeStruct((NTOK, CONV_CH), F32),
            jax.ShapeDtypeStruct((NTOK_P, ATT_KV), F32),
            jax.ShapeDtypeStruct((NTOK_P, ATT_KV), F32),
        ],
        scratch_shapes=[pltpu.VMEM((D_MODEL, EVEN_IN), BF16)],
        compiler_params=_cparams("arbitrary"),
        name="even_in",
    )(x, mod_l, norm_g.reshape(DEPTH, 3, 1, D_MODEL), ev_w_in, qg, kg,
      cos, sin_next, sin_prev, cos, sin_next, sin_prev)


ATT_WIDTH = GQA_GROUP * HEAD_DIM


def _expand_kv_head(x, kv):
    lane = lax.broadcasted_iota(jnp.int32, x.shape, 1)
    swapped = pltpu.roll(x, HEAD_DIM, 1)
    pair = jnp.where((lane // HEAD_DIM) == kv, x, swapped)
    return jnp.concatenate([pair, pair], axis=1)


def _attend(q_ref, q_rows, kbig, vbig, n_keys, attn_scr):
    lane_group = lax.broadcasted_iota(jnp.int32, (ATT_TQ, ATT_WIDTH), 1) // HEAD_DIM
    for kv in range(N_KV_HEADS):
        cols = slice(kv * ATT_WIDTH, (kv + 1) * ATT_WIDTH)
        q = q_ref[q_rows, cols]
        kb = kbig[kv, 0:n_keys, :]
        vb = vbig[kv, 0:n_keys, :]
        acc = jnp.zeros((ATT_TQ, ATT_WIDTH), F32)
        for g in range(GQA_GROUP):
            mine = lane_group == g
            qm = jnp.where(mine, q, jnp.zeros_like(q))
            s = lax.dot_general(qm, kb, (((1,), (1,)), ((), ())), preferred_element_type=F32)
            m = jnp.max(s, axis=-1, keepdims=True)
            p = jnp.exp2(s - m)
            denom = jnp.sum(p, axis=-1, keepdims=True)
            o = _dot(p.astype(BF16), vb)
            acc = jnp.where(mine, o / denom, acc)
        attn_scr[:, cols] = acc.astype(BF16)


CONV_PAD_SEQ = SEQ + 2 * CONV_HALO
CONV_WIN = ATT_TQ + 2 * CONV_HALO
CONV_SPAN = CONV_WIN - V7X_SUBLANES


def _conv_block(win_scr, w_ref, y_scr):
    off = CONV_HALO - CONV_K // 2
    for r0 in range(0, ATT_TQ, CONV_ROWS):
        for c0 in range(0, CONV_CH, V7X_LANES):
            cols = slice(c0, c0 + V7X_LANES)
            acc = jnp.zeros((CONV_ROWS, V7X_LANES), F32)
            for k in range(CONV_K):
                res = (off + k) % V7X_SUBLANES
                a = r0 + off + k - res
                acc = acc + win_scr[res, a:a + CONV_ROWS, cols] * w_ref[k:k + 1, cols]
            y_scr[r0:r0 + CONV_ROWS, cols] = acc


def _even_mix_kernel(q_ref, k_ref, v_ref, ck_ref, cv_ref, glu_ref, w_ref, b_ref, g_ref, beta_ref,
                     res_ref, mod_ref, wo_ref, o_ref,
                     kbig, vbig, pad_scr, win0_scr, win_scr, y_scr, attn_scr, wo_scr):
    i = pl.program_id(0)
    t = pl.program_id(1)
    rows = pl.ds(pl.multiple_of(t * ATT_TQ, ATT_TQ), ATT_TQ)
    border = jnp.zeros((CONV_HALO, CONV_CH), F32)

    @pl.when((i == 0) & (t == 0))
    def _():
        wo_scr[...] = wo_ref[...].astype(BF16)

    def expand(k, v, n):
        for kv in range(N_KV_HEADS):
            kbig[kv, 0:n, :] = _expand_kv_head(k, kv).astype(BF16)
            vbig[kv, 0:n, :] = _expand_kv_head(v, kv).astype(BF16)

    def mix(n_keys, win_start):
        win0_scr[...] = pad_scr[pl.ds(pl.multiple_of(win_start, V7X_SUBLANES), CONV_WIN), :]
        for res in range(V7X_SUBLANES):
            win_scr[res] = win0_scr[res:res + CONV_SPAN, :]
        _attend(q_ref, rows, kbig, vbig, n_keys, attn_scr)
        _conv_block(win_scr, w_ref, y_scr)
        y = _layer_norm(y_scr[...] + b_ref[...], g_ref[...], beta_ref[...])
        conv = (y * jax.nn.sigmoid(y)).astype(BF16)
        mixed = _dot(attn_scr[...], wo_scr[:ATT_Q, :]) + _dot(conv, wo_scr[ATT_Q:, :])
        o_ref[...] = res_ref[...] + mod_ref[0, 5:6, :] * mixed

    @pl.when(i < N_TILES_P)
    def _():
        @pl.when(t == 0)
        def _():
            for s in range(SEQ_PER_TILE):
                base = s * CONV_PAD_SEQ
                pad_scr[base:base + CONV_HALO, :] = border
                pad_scr[base + CONV_HALO:base + CONV_HALO + SEQ, :] = glu_ref[s * SEQ:(s + 1) * SEQ, :]
                pad_scr[base + CONV_HALO + SEQ:base + CONV_PAD_SEQ, :] = border

        expand(k_ref[rows, :], v_ref[rows, :], SEQ)
        mix(SEQ, t * CONV_PAD_SEQ)

    @pl.when(i >= N_TILES_P)
    def _():
        @pl.when(t == 0)
        def _():
            pad_scr[0:CONV_HALO, :] = border
            pad_scr[CONV_HALO:CONV_HALO + DEC_SEQ, :] = glu_ref[...]
            pad_scr[CONV_HALO + DEC_SEQ:2 * CONV_HALO + DEC_SEQ, :] = border
            expand(jnp.concatenate([ck_ref[...], k_ref[...]], axis=0),
                   jnp.concatenate([cv_ref[...], v_ref[...]], axis=0), PAST_LEN + DEC_SEQ)

        mix(PAST_LEN + DEC_SEQ, t * ATT_TQ)


def _even_mix(x, mod_l, q, k, v, ck, cv, glu, conv_w, conv_b, cn_g, cn_b, w_out, j):
    nq = TM // ATT_TQ
    tok = lambda n: pl.BlockSpec((TM, n), lambda i, t: (i, 0))
    blk = pl.BlockSpec((ATT_TQ, D_MODEL), lambda i, t: (i * nq + t, 0))
    ctx = pl.BlockSpec((None, None, PAST_LEN, ATT_KV), lambda i, t: (jnp.maximum(i - N_TILES_P, 0), j, 0, 0))
    vec = pl.BlockSpec((None, 1, CONV_CH), lambda i, t: (j, 0, 0))
    n_even = conv_b.shape[0]
    n_keys = PAST_LEN + DEC_SEQ
    return pl.pallas_call(
        _even_mix_kernel,
        grid=(N_TILES, nq),
        in_specs=[
            tok(ATT_Q), tok(ATT_KV), tok(ATT_KV), ctx, ctx, tok(CONV_CH),
            pl.BlockSpec((None, CONV_K, CONV_CH), lambda i, t: (j, 0, 0)),
            vec, vec, vec,
            blk,
            pl.BlockSpec((1, N_MOD, D_MODEL), lambda i, t: (_cond_of_tile(i), 0, 0)),
            pl.BlockSpec((None, ATT_Q + CONV_CH, D_MODEL), lambda i, t: (j, 0, 0),
                         pipeline_mode=pl.Buffered(1)),
        ],
        out_specs=blk,
        out_shape=jax.ShapeDtypeStruct((NTOK, D_MODEL), F32),
        scratch_shapes=[
            pltpu.VMEM((N_KV_HEADS, n_keys, ATT_WIDTH), BF16),
            pltpu.VMEM((N_KV_HEADS, n_keys, ATT_WIDTH), BF16),
            pltpu.VMEM((max(SEQ_PER_TILE * CONV_PAD_SEQ, DEC_SEQ + 2 * CONV_HALO), CONV_CH), F32),
            pltpu.VMEM((CONV_WIN, CONV_CH), F32),
            pltpu.VMEM((V7X_SUBLANES, CONV_SPAN, CONV_CH), F32),
            pltpu.VMEM((ATT_TQ, CONV_CH), F32),
            pltpu.VMEM((ATT_TQ, ATT_Q), BF16),
            pltpu.VMEM((ATT_Q + CONV_CH, D_MODEL), BF16),
        ],
        compiler_params=_cparams("arbitrary", "arbitrary"),
        name="even_mix",
    )(q, k, v, ck, cv, glu, conv_w, conv_b.reshape(n_even, 1, CONV_CH),
      cn_g.reshape(n_even, 1, CONV_CH), cn_b.reshape(n_even, 1, CONV_CH), x, mod_l, w_out)


def _odd_in_kernel(x_ref, mod_ref, g_ref, w_ref, sg_ref, sb_ref, ws_ref, bs_ref, f_ref, sgu_ref, w_scr):
    @pl.when(pl.program_id(0) == 0)
    def _():
        w_scr[...] = w_ref[...].astype(BF16)

    h = _modulate(x_ref[...], g_ref[...], mod_ref[0, 3:4, :], mod_ref[0, 4:5, :]).astype(BF16)

    def proj(lo, hi):
        return _dot(h, w_scr[:, lo:hi])

    f_ref[...] = proj(0, FOURIER_CH).astype(BF16)
    u = proj(FOURIER_CH, FOURIER_CH + SGU_CH)
    v = _layer_norm(proj(FOURIER_CH + SGU_CH, ODD_IN), sg_ref[...], sb_ref[...]).astype(BF16)
    for g in range(SGU_GROUPS):
        wg = ws_ref[g].astype(BF16)
        bias = bs_ref[:, g:g + 1]
        cols = slice(g * SGU_GROUP_CH, (g + 1) * SGU_GROUP_CH)
        for n in range(TM // CHUNK):
            rows = slice(n * CHUNK, (n + 1) * CHUNK)
            mixed = _dot(wg, v[rows, cols]) + bias
            sgu_ref[rows, cols] = (u[rows, cols] * mixed).astype(BF16)


def _odd_in(x, mod_l, norm_g, od_w_in, sgu_norm_g, sgu_norm_b, sgu_w, sgu_b, layer, j):
    tok = lambda n: pl.BlockSpec((TM, n), lambda i: (i, 0))
    return pl.pallas_call(
        _odd_in_kernel,
        grid=(N_TILES,),
        in_specs=[
            tok(D_MODEL),
            pl.BlockSpec((1, N_MOD, D_MODEL), lambda i: (_cond_of_tile(i), 0, 0)),
            pl.BlockSpec((None, None, 1, D_MODEL), lambda i: (layer, 1, 0, 0)),
            pl.BlockSpec((None, D_MODEL, ODD_IN), lambda i: (j, 0, 0), pipeline_mode=pl.Buffered(1)),
            pl.BlockSpec((1, SGU_CH), lambda i: (0, 0)),
            pl.BlockSpec((1, SGU_CH), lambda i: (0, 0)),
            pl.BlockSpec((None, SGU_GROUPS, CHUNK, CHUNK), lambda i: (j, 0, 0, 0)),
            pl.BlockSpec((CHUNK, SGU_GROUPS), lambda i: (0, 0)),
        ],
        out_specs=[tok(FOURIER_CH), tok(SGU_CH)],
        out_shape=[jax.ShapeDtypeStruct((NTOK, FOURIER_CH), BF16),
                   jax.ShapeDtypeStruct((NTOK, SGU_CH), BF16)],
        scratch_shapes=[pltpu.VMEM((D_MODEL, ODD_IN), BF16)],
        compiler_params=_cparams("arbitrary"),
        name="odd_in",
    )(x, mod_l, norm_g.reshape(DEPTH, 3, 1, D_MODEL), od_w_in,
      sgu_norm_g[j].reshape(1, SGU_CH), sgu_norm_b[j].reshape(1, SGU_CH), sgu_w, sgu_b[j].T)


def _fourier_out_kernel(f_ref, wch_ref, wseq_p_ref, wseq_s_ref, res_ref, mod_ref, sgu_ref, wo_ref,
                        o_ref, four_scr, wch_scr, wseq_p_scr, wseq_s_scr, wo_scr):
    i = pl.program_id(0)

    @pl.when(i == 0)
    def _():
        wch_scr[...] = wch_ref[...].astype(BF16)
        wseq_p_scr[...] = wseq_p_ref[...].astype(BF16)
        wseq_s_scr[...] = wseq_s_ref[...].astype(BF16)
        wo_scr[...] = wo_ref[...].astype(BF16)

    pq = _dot(f_ref[...], wch_scr[...]).astype(BF16)

    def seq_dft(rows, wseq, seq):
        stacked = jnp.concatenate([pq[rows, :FOURIER_CH], pq[rows, FOURIER_CH:]], axis=0)
        scale = 1.0 / np.sqrt(float(seq * FOURIER_GROUP_CH))
        four_scr[rows, :] = (_dot(wseq, stacked) * scale).astype(BF16)

    @pl.when(i < N_TILES_P)
    def _():
        for s in range(SEQ_PER_TILE):
            seq_dft(slice(s * SEQ, (s + 1) * SEQ), wseq_p_scr[...], SEQ)

    @pl.when(i >= N_TILES_P)
    def _():
        seq_dft(slice(0, DEC_SEQ), wseq_s_scr[...], DEC_SEQ)

    gate = mod_ref[0, 5:6, :]
    for s in range(SEQ_PER_TILE):
        rows = slice(s * SEQ, (s + 1) * SEQ)
        mixed = (_dot(four_scr[rows, :], wo_scr[:FOURIER_CH, :])
                 + _dot(sgu_ref[rows, :], wo_scr[FOURIER_CH:, :]))
        o_ref[rows, :] = res_ref[rows, :] + gate * mixed


def _dft_constants():
    def cos_sin(n):
        idx = np.arange(n)
        ang = 2.0 * np.pi * ((idx[:, None] * idx[None, :]) % n) / n
        return np.cos(ang), np.sin(ang)

    cc, sc = cos_sin(FOURIER_GROUP_CH)
    eye = np.eye(FOURIER_GROUPS)
    wch = np.concatenate([np.kron(eye, cc), np.kron(eye, sc)], axis=1)

    def wseq(seq):
        cl, sl = cos_sin(seq)
        return jnp.asarray(np.concatenate([cl, -sl], axis=1), dtype=F32)

    return jnp.asarray(wch, dtype=F32), wseq(SEQ), wseq(DEC_SEQ)


def _fourier_out(x, mod_l, f, sgu, w_out, j):
    wch, wseq_p, wseq_s = _dft_constants()
    tok = lambda n: pl.BlockSpec((TM, n), lambda i: (i, 0))
    const = lambda shape: pl.BlockSpec(shape, lambda i: (0,) * len(shape), pipeline_mode=pl.Buffered(1))
    return pl.pallas_call(
        _fourier_out_kernel,
        grid=(N_TILES,),
        in_specs=[
            tok(FOURIER_CH), const(wch.shape), const(wseq_p.shape), const(wseq_s.shape),
            tok(D_MODEL),
            pl.BlockSpec((1, N_MOD, D_MODEL), lambda i: (_cond_of_tile(i), 0, 0)),
            tok(SGU_CH),
            pl.BlockSpec((None, FOURIER_CH + SGU_CH, D_MODEL), lambda i: (j, 0, 0),
                         pipeline_mode=pl.Buffered(1)),
        ],
        out_specs=tok(D_MODEL),
        out_shape=jax.ShapeDtypeStruct((NTOK, D_MODEL), F32),
        scratch_shapes=[pltpu.VMEM((TM, FOURIER_CH), BF16), pltpu.VMEM(wch.shape, BF16),
                        pltpu.VMEM(wseq_p.shape, BF16), pltpu.VMEM(wseq_s.shape, BF16),
                        pltpu.VMEM((FOURIER_CH + SGU_CH, D_MODEL), BF16)],
        compiler_params=_cparams("arbitrary"),
        name="fourier_out",
    )(f, wch, wseq_p, wseq_s, x, mod_l, sgu, w_out)


def kernel(x_prompt, x_sample, cache_k, cache_v, c, c_ctx, w_mod, b_mod, norm_g, ffn_w_in, ffn_w_out, ev_w_in, ev_w_out, q_norm_g, k_norm_g, conv_w, conv_b, conv_norm_g, conv_norm_b, od_w_in, od_w_out, sgu_norm_g, sgu_norm_b, sgu_w, sgu_b):
    x = jnp.concatenate([x_prompt.reshape(NTOK_P, D_MODEL), x_sample.reshape(NTOK_S, D_MODEL)], axis=0)
    cond = jnp.concatenate([c_ctx[None, :], c_ctx[None, :], c,
                            jnp.zeros((N_COND - COND_LATENT0 - DEC_BATCH, D_MODEL), F32)], axis=0)
    mod = _adaln(cond, w_mod, b_mod).reshape(DEPTH, N_COND, N_MOD, D_MODEL)
    ck = cache_k.reshape(DEC_BATCH, -1, PAST_LEN, ATT_KV)
    cv = cache_v.reshape(DEC_BATCH, -1, PAST_LEN, ATT_KV)

    new_k, new_v = [], []
    for layer in range(DEPTH):
        mod_l = mod[layer]
        j = layer // 2
        x = _ffn(x, mod_l, norm_g, ffn_w_in, ffn_w_out, layer, 0)
        if layer % 2 == 0:
            q, k, v, glu, kc, vc = _even_in(x, mod_l, norm_g, ev_w_in, q_norm_g, k_norm_g, layer, j)
            new_k.append(kc.reshape(BATCH, SEQ, N_KV_HEADS, HEAD_DIM))
            new_v.append(vc.reshape(BATCH, SEQ, N_KV_HEADS, HEAD_DIM))
            x = _even_mix(x, mod_l, q, k, v, ck, cv, glu, conv_w, conv_b, conv_norm_g, conv_norm_b,
                          ev_w_out, j)
        else:
            f, sgu = _odd_in(x, mod_l, norm_g, od_w_in, sgu_norm_g, sgu_norm_b, sgu_w, sgu_b, layer, j)
            x = _fourier_out(x, mod_l, f, sgu, od_w_out, j)
        if layer < DEPTH - 1:
            x = _ffn(x, mod_l, norm_g, ffn_w_in, ffn_w_out, layer, 1)

    last = DEPTH - 1
    y_prompt = _ffn(x, mod[last], norm_g, ffn_w_in, ffn_w_out, last, 1, 0, NTOK_P)
    y_sample = _ffn(x, mod[last], norm_g, ffn_w_in, ffn_w_out, last, 1, NTOK_P, NTOK_S)
    return (y_prompt.reshape(BATCH, SEQ, D_MODEL), y_sample.reshape(DEC_BATCH, DEC_SEQ, D_MODEL),
            jnp.stack(new_k, axis=1), jnp.stack(new_v, axis=1))
```

```python
import functools

import numpy as np
import jax
import jax.numpy as jnp
from jax import lax
from jax.experimental import pallas as pl
from jax.experimental.pallas import tpu as pltpu

D_MODEL = 1024
BATCH = 32
SEQ = 256
DEPTH = 4
DEC_BATCH = 4
DEC_SEQ = 1024
PAST_LEN = 512
GRID_W = 64
N_HEADS = 8
N_KV_HEADS = 2
HEAD_DIM = 64
GQA_GROUP = N_HEADS // N_KV_HEADS
AXIS_ROPE_DIM = HEAD_DIM // 2
ROPE_THETA = 10000.0
ATT_Q = N_HEADS * HEAD_DIM
ATT_KV = N_KV_HEADS * HEAD_DIM
CONV_CH = 512
CONV_K = 31
FOURIER_GROUPS = 4
FOURIER_GROUP_CH = 128
FOURIER_CH = FOURIER_GROUPS * FOURIER_GROUP_CH
SGU_GROUPS = 4
SGU_GROUP_CH = 128
SGU_CH = SGU_GROUPS * SGU_GROUP_CH
CHUNK = 128
FFN_DIM = 2816
N_MOD = 9
EVEN_IN = ATT_Q + 2 * ATT_KV + 2 * CONV_CH
ODD_IN = FOURIER_CH + 2 * SGU_CH
RMS_EPS = 1e-6
LN_EPS = 1e-5
LOG2_E = 1.4426950408889634

BF16 = jnp.bfloat16
F32 = jnp.float32

V7X_SUBLANES = 8
V7X_LANES = 128
NTOK_P = BATCH * SEQ
NTOK_S = DEC_BATCH * DEC_SEQ
NTOK = NTOK_P + NTOK_S
TM = DEC_SEQ
N_TILES = NTOK // TM
N_TILES_P = NTOK_P // TM
SEQ_PER_TILE = TM // SEQ
N_COND = 8
COND_LATENT0 = 2
FFN_TM = 2 * TM
FFN_CHUNK = 256
N_FFN_CHUNKS = FFN_DIM // FFN_CHUNK
EVEN_IN_ROWS = 512
ADALN_TN = 1024
CONV_ROWS = 128
CONV_HALO = 16
ATT_TQ = SEQ
V7X_VMEM_LIMIT_BYTES = 56 * 1024 * 1024


def _cparams(*sem):
    return pltpu.CompilerParams(dimension_semantics=sem, vmem_limit_bytes=V7X_VMEM_LIMIT_BYTES)


def _cond_of_tile(i, rows=TM):
    return jnp.maximum((i * rows - NTOK_P) // DEC_SEQ + COND_LATENT0, 0)


def _dot(a, b):
    return jnp.dot(a, b, preferred_element_type=F32)


def _modulate(x, g, shift, scale):
    ms = jnp.mean(x * x, axis=-1, keepdims=True)
    return (x * lax.rsqrt(ms + RMS_EPS)) * (g * (1.0 + scale)) + shift


def _layer_norm(x, g, b):
    mu = jnp.mean(x, axis=-1, keepdims=True)
    xc = x - mu
    var = jnp.mean(xc * xc, axis=-1, keepdims=True)
    return xc * lax.rsqrt(var + LN_EPS) * g + b


def _adaln_kernel(c_ref, w_ref, b_ref, o_ref):
    c = c_ref[...]
    s = (c * jax.nn.sigmoid(c)).astype(BF16)
    o_ref[...] = _dot(s, w_ref[...].astype(BF16)) + b_ref[...]


def _adaln(cond, w_mod, b_mod):
    n_out = N_MOD * D_MODEL
    return pl.pallas_call(
        _adaln_kernel,
        grid=(DEPTH, n_out // ADALN_TN),
        in_specs=[
            pl.BlockSpec((N_COND, D_MODEL), lambda l, n: (0, 0)),
            pl.BlockSpec((None, D_MODEL, ADALN_TN), lambda l, n: (l, 0, n)),
            pl.BlockSpec((None, 1, ADALN_TN), lambda l, n: (l, 0, n)),
        ],
        out_specs=pl.BlockSpec((None, N_COND, ADALN_TN), lambda l, n: (l, 0, n)),
        out_shape=jax.ShapeDtypeStruct((DEPTH, N_COND, n_out), F32),
        compiler_params=_cparams("parallel", "parallel"),
        name="adaln",
    )(cond, w_mod, b_mod.reshape(DEPTH, 1, n_out))


def _ffn_kernel(subs, n_x, x_split, *refs):
    n_ph = len(subs)
    x_refs = refs[:n_x]
    mod_refs = refs[n_x:n_x + n_ph]
    g_refs = refs[n_x + n_ph:n_x + 2 * n_ph]
    wg_ref, wu_ref, wo_ref, o_ref, h_scr = refs[n_x + 2 * n_ph:]
    i = pl.program_id(0)
    j = pl.program_id(1)
    halves = [(c, slice(c * TM, (c + 1) * TM)) for c in range(FFN_TM // TM)]

    if n_x == 1:
        @pl.when(j == 0)
        def _():
            o_ref[...] = x_refs[0][...]
    else:
        @pl.when((j == 0) & (i < x_split))
        def _():
            o_ref[...] = x_refs[0][...]

        @pl.when((j == 0) & (i >= x_split))
        def _():
            o_ref[...] = x_refs[1][...]

    for p, sub in enumerate(subs):
        @pl.when(j == p * N_FFN_CHUNKS)
        def _(p=p, sub=sub):
            for c, rows in halves:
                h = _modulate(o_ref[rows, :], g_refs[p][...], mod_refs[p][c, 3 * sub:3 * sub + 1, :],
                              mod_refs[p][c, 3 * sub + 1:3 * sub + 2, :])
                h_scr[rows, :] = h.astype(BF16)

    h = h_scr[...]
    gate = _dot(h, wg_ref[...].astype(BF16))
    up = _dot(h, wu_ref[...].astype(BF16))
    a = (gate * jax.nn.sigmoid(gate) * up).astype(BF16)
    wo = wo_ref[...]
    for c, rows in halves:
        out_gate = mod_refs[0][c, 3 * subs[0] + 2:3 * subs[0] + 3, :]
        if n_ph == 2:
            out_gate = jnp.where(j < N_FFN_CHUNKS, out_gate,
                                 mod_refs[1][c, 3 * subs[1] + 2:3 * subs[1] + 3, :])
        o_ref[rows, :] += _dot(a[rows, :], (wo * (0.5 * out_gate)).astype(BF16))


def _ffn(xs, mod, norm_g, ffn_w_in, ffn_w_out, phases, row0=0, n_rows=NTOK):
    per = FFN_TM // TM
    tile0 = row0 // FFN_TM
    n_ph = len(phases)
    layers = [p[0] for p in phases]
    whichs = [p[1] for p in phases]

    def by_phase(vals):
        if n_ph == 1:
            return lambda j: vals[0]
        return lambda j: vals[0] + (j // N_FFN_CHUNKS) * (vals[1] - vals[0])

    layer_of, which_of = by_phase(layers), by_phase(whichs)
    chunk_of = lambda j: j % N_FFN_CHUNKS
    if len(xs) == 1:
        x_specs = [pl.BlockSpec((FFN_TM, D_MODEL), lambda i, j: (tile0 + i, 0))]
        x_split = 0
    else:
        x_split = xs[0].shape[0] // FFN_TM
        x_specs = [
            pl.BlockSpec((FFN_TM, D_MODEL), lambda i, j: (jnp.minimum(i, x_split - 1), 0),
                         pipeline_mode=pl.Buffered(1)),
            pl.BlockSpec((FFN_TM, D_MODEL), lambda i, j: (jnp.maximum(i - x_split, 0), 0),
                         pipeline_mode=pl.Buffered(1)),
        ]
    mod_specs = [pl.BlockSpec((None, per, N_MOD, D_MODEL),
                              lambda i, j, l=l: (l, _cond_of_tile(tile0 + i, FFN_TM) // per, 0, 0))
                 for l in layers]
    g_specs = [pl.BlockSpec((None, None, 1, D_MODEL), lambda i, j, l=l, w=w: (l, 2 * w, 0, 0))
               for l, w in phases]
    return pl.pallas_call(
        functools.partial(_ffn_kernel, tuple(2 * w for w in whichs), len(xs), x_split),
        grid=(n_rows // FFN_TM, n_ph * N_FFN_CHUNKS),
        in_specs=x_specs + mod_specs + g_specs + [
            pl.BlockSpec((None, None, D_MODEL, FFN_CHUNK),
                         lambda i, j: (layer_of(j), which_of(j), 0, chunk_of(j))),
            pl.BlockSpec((None, None, D_MODEL, FFN_CHUNK),
                         lambda i, j: (layer_of(j), which_of(j), 0, N_FFN_CHUNKS + chunk_of(j))),
            pl.BlockSpec((None, None, FFN_CHUNK, D_MODEL),
                         lambda i, j: (layer_of(j), which_of(j), chunk_of(j), 0)),
        ],
        out_specs=pl.BlockSpec((FFN_TM, D_MODEL), lambda i, j: (i, 0)),
        out_shape=jax.ShapeDtypeStruct((n_rows, D_MODEL), F32),
        scratch_shapes=[pltpu.VMEM((FFN_TM, D_MODEL), BF16)],
        compiler_params=_cparams("parallel", "arbitrary"),
        name="ffn",
    )(*xs, *([mod] * n_ph), *([norm_g.reshape(DEPTH, 3, 1, D_MODEL)] * n_ph),
      ffn_w_in, ffn_w_in, ffn_w_out)


def _group_mean_sq(x, width):
    r = lax.broadcasted_iota(jnp.int32, (width, width), 0) // HEAD_DIM
    c = lax.broadcasted_iota(jnp.int32, (width, width), 1) // HEAD_DIM
    ones = (r == c).astype(BF16)
    return _dot((x * x).astype(BF16), ones) * (1.0 / HEAD_DIM)


def _head_norm_rope(x, gain, cos, sin_next, sin_prev):
    width = x.shape[-1]
    xn = x * lax.rsqrt(_group_mean_sq(x, width) + RMS_EPS) * gain
    nxt = pltpu.roll(xn, width - 1, 1)
    prv = pltpu.roll(xn, 1, 1)
    return xn * cos + nxt * sin_next + prv * sin_prev


def _even_in_kernel(x_ref, mod_ref, g_ref, w_ref, qg_ref, kg_ref,
                    cq_ref, snq_ref, spq_ref, ck_ref, snk_ref, spk_ref,
                    q_ref, k_ref, v_ref, glu_ref, kc_ref, vc_ref, w_scr):
    i = pl.program_id(0)

    @pl.when(i == 0)
    def _():
        w_scr[...] = w_ref[...].astype(BF16)

    h = _modulate(x_ref[...], g_ref[...], mod_ref[0, 3:4, :], mod_ref[0, 4:5, :]).astype(BF16)

    def proj(lo, hi):
        return _dot(h, w_scr[:, lo:hi])

    q = proj(0, ATT_Q)
    q = _head_norm_rope(q, qg_ref[...], cq_ref[...], snq_ref[...], spq_ref[...])
    q_ref[...] = (q * (HEAD_DIM ** -0.5 * LOG2_E)).astype(BF16)
    k = proj(ATT_Q, ATT_Q + ATT_KV)
    k = _head_norm_rope(k, kg_ref[...], ck_ref[...], snk_ref[...], spk_ref[...])
    v = proj(ATT_Q + ATT_KV, ATT_Q + 2 * ATT_KV)
    k_ref[...] = k
    v_ref[...] = v

    @pl.when(i < NTOK_P // EVEN_IN_ROWS)
    def _():
        kc_ref[...] = k
        vc_ref[...] = v

    a0 = ATT_Q + 2 * ATT_KV
    a = proj(a0, a0 + CONV_CH)
    gt = proj(a0 + CONV_CH, a0 + 2 * CONV_CH)
    glu_ref[...] = a * jax.nn.sigmoid(gt)


def _rope_tables():
    t = np.arange(DEC_SEQ)
    row = (t // GRID_W).astype(np.float64)
    col = (t % GRID_W).astype(np.float64)
    inv_freq = 1.0 / (ROPE_THETA ** (np.arange(0, AXIS_ROPE_DIM, 2, dtype=np.float64) / AXIS_ROPE_DIM))
    ang = np.concatenate([row[:, None] * inv_freq, col[:, None] * inv_freq], axis=-1)
    ang = np.repeat(ang, 2, axis=-1)
    even = (np.arange(HEAD_DIM) % 2 == 0)[None, :]
    cos = np.cos(ang)
    sin_next = np.where(even, -np.sin(ang), 0.0)
    sin_prev = np.where(even, 0.0, np.sin(ang))

    def both(tab):
        ident = np.ones_like(tab) if tab is cos else np.zeros_like(tab)
        return np.stack([np.tile(ident, (1, N_HEADS)), np.tile(tab, (1, N_HEADS))]).astype(np.float32)

    return both(cos), both(sin_next), both(sin_prev)


def _even_in(x, mod_l, norm_g, ev_w_in, q_norm_g, k_norm_g, layer, j):
    cos, sin_next, sin_prev = (jnp.asarray(t) for t in _rope_tables())
    qg = jnp.tile(q_norm_g[j], N_HEADS).reshape(1, ATT_Q)
    kg = jnp.tile(k_norm_g[j], N_KV_HEADS).reshape(1, ATT_KV)
    rows = EVEN_IN_ROWS
    per_seq = DEC_SEQ // rows
    first_latent = NTOK_P // rows
    tab_idx = lambda i: (jnp.minimum(i // first_latent, 1), i % per_seq, 0)
    tab_q = pl.BlockSpec((None, rows, ATT_Q), tab_idx)
    tab_k = pl.BlockSpec((None, rows, ATT_KV), tab_idx)
    tok = lambda n: pl.BlockSpec((rows, n), lambda i: (i, 0))
    cache = pl.BlockSpec((rows, ATT_KV), lambda i: (jnp.minimum(i, first_latent - 1), 0))
    return pl.pallas_call(
        _even_in_kernel,
        grid=(NTOK // rows,),
        in_specs=[
            tok(D_MODEL),
            pl.BlockSpec((1, N_MOD, D_MODEL), lambda i: (_cond_of_tile(i, rows), 0, 0)),
            pl.BlockSpec((None, None, 1, D_MODEL), lambda i: (layer, 1, 0, 0)),
            pl.BlockSpec((None, D_MODEL, EVEN_IN), lambda i: (j, 0, 0), pipeline_mode=pl.Buffered(1)),
            pl.BlockSpec((1, ATT_Q), lambda i: (0, 0)),
            pl.BlockSpec((1, ATT_KV), lambda i: (0, 0)),
            tab_q, tab_q, tab_q, tab_k, tab_k, tab_k,
        ],
        out_specs=[tok(ATT_Q), tok(ATT_KV), tok(ATT_KV), tok(CONV_CH), cache, cache],
        out_shape=[
            jax.ShapeDtypeStruct((NTOK, ATT_Q), BF16),
            jax.ShapeDtypeStruct((NTOK, ATT_KV), F32),
            jax.ShapeDtypeStruct((NTOK, ATT_KV), F32),
            jax.ShapeDtypeStruct((NTOK, CONV_CH), F32),
            jax.ShapeDtypeStruct((NTOK_P, ATT_KV), F32),
            jax.ShapeDtypeStruct((NTOK_P, ATT_KV), F32),
        ],
        scratch_shapes=[pltpu.VMEM((D_MODEL, EVEN_IN), BF16)],
        compiler_params=_cparams("arbitrary"),
        name="even_in",
    )(x, mod_l, norm_g.reshape(DEPTH, 3, 1, D_MODEL), ev_w_in, qg, kg,
      cos, sin_next, sin_prev, cos, sin_next, sin_prev)


ATT_WIDTH = GQA_GROUP * HEAD_DIM


def _expand_kv_head(x, kv):
    lane = lax.broadcasted_iota(jnp.int32, x.shape, 1)
    swapped = pltpu.roll(x, HEAD_DIM, 1)
    pair = jnp.where((lane // HEAD_DIM) == kv, x, swapped)
    return jnp.concatenate([pair, pair], axis=1)


def _attend(q_ref, q_rows, kbig, vbig, n_keys, attn_scr):
    lane_group = lax.broadcasted_iota(jnp.int32, (ATT_TQ, ATT_WIDTH), 1) // HEAD_DIM
    for kv in range(N_KV_HEADS):
        cols = slice(kv * ATT_WIDTH, (kv + 1) * ATT_WIDTH)
        q = q_ref[q_rows, cols]
        kb = kbig[kv, 0:n_keys, :]
        vb = vbig[kv, 0:n_keys, :]
        acc = jnp.zeros((ATT_TQ, ATT_WIDTH), F32)
        for g in range(GQA_GROUP):
            mine = lane_group == g
            qm = jnp.where(mine, q, jnp.zeros_like(q))
            s = lax.dot_general(qm, kb, (((1,), (1,)), ((), ())), preferred_element_type=F32)
            m = jnp.max(s, axis=-1, keepdims=True)
            p = jnp.exp2(s - m)
            denom = jnp.sum(p, axis=-1, keepdims=True)
            o = _dot(p.astype(BF16), vb)
            acc = jnp.where(mine, o / denom, acc)
        attn_scr[:, cols] = acc.astype(BF16)


CONV_PAD_SEQ = SEQ + 2 * CONV_HALO
CONV_WIN = ATT_TQ + 2 * CONV_HALO
CONV_SPAN = CONV_WIN - V7X_SUBLANES


def _conv_block(win_scr, w_ref, y_scr):
    off = CONV_HALO - CONV_K // 2
    for r0 in range(0, ATT_TQ, CONV_ROWS):
        for c0 in range(0, CONV_CH, V7X_LANES):
            cols = slice(c0, c0 + V7X_LANES)
            acc = jnp.zeros((CONV_ROWS, V7X_LANES), F32)
            for k in range(CONV_K):
                res = (off + k) % V7X_SUBLANES
                a = r0 + off + k - res
                acc = acc + win_scr[res, a:a + CONV_ROWS, cols] * w_ref[k:k + 1, cols]
            y_scr[r0:r0 + CONV_ROWS, cols] = acc


def _even_mix_kernel(q_ref, k_ref, v_ref, ck_ref, cv_ref, glu_ref, w_ref, b_ref, g_ref, beta_ref,
                     res_ref, mod_ref, wo_ref, o_ref,
                     kbig, vbig, pad_scr, win0_scr, win_scr, y_scr, attn_scr, wo_scr):
    i = pl.program_id(0)
    t = pl.program_id(1)
    rows = pl.ds(pl.multiple_of(t * ATT_TQ, ATT_TQ), ATT_TQ)
    border = jnp.zeros((CONV_HALO, CONV_CH), F32)

    @pl.when((i == 0) & (t == 0))
    def _():
        wo_scr[...] = wo_ref[...].astype(BF16)

    def expand(k, v, n):
        for kv in range(N_KV_HEADS):
            kbig[kv, 0:n, :] = _expand_kv_head(k, kv).astype(BF16)
            vbig[kv, 0:n, :] = _expand_kv_head(v, kv).astype(BF16)

    def mix(n_keys, win_start):
        win0_scr[...] = pad_scr[pl.ds(pl.multiple_of(win_start, V7X_SUBLANES), CONV_WIN), :]
        for res in range(V7X_SUBLANES):
            win_scr[res] = win0_scr[res:res + CONV_SPAN, :]
        _attend(q_ref, rows, kbig, vbig, n_keys, attn_scr)
        _conv_block(win_scr, w_ref, y_scr)
        y = _layer_norm(y_scr[...] + b_ref[...], g_ref[...], beta_ref[...])
        conv = (y * jax.nn.sigmoid(y)).astype(BF16)
        mixed = _dot(attn_scr[...], wo_scr[:ATT_Q, :]) + _dot(conv, wo_scr[ATT_Q:, :])
        o_ref[...] = res_ref[...] + mod_ref[0, 5:6, :] * mixed

    @pl.when(i < N_TILES_P)
    def _():
        @pl.when(t == 0)
        def _():
            for s in range(SEQ_PER_TILE):
                base = s * CONV_PAD_SEQ
                pad_scr[base:base + CONV_HALO, :] = border
                pad_scr[base + CONV_HALO:base + CONV_HALO + SEQ, :] = glu_ref[s * SEQ:(s + 1) * SEQ, :]
                pad_scr[base + CONV_HALO + SEQ:base + CONV_PAD_SEQ, :] = border

        expand(k_ref[rows, :], v_ref[rows, :], SEQ)
        mix(SEQ, t * CONV_PAD_SEQ)

    @pl.when(i >= N_TILES_P)
    def _():
        @pl.when(t == 0)
        def _():
            pad_scr[0:CONV_HALO, :] = border
            pad_scr[CONV_HALO:CONV_HALO + DEC_SEQ, :] = glu_ref[...]
            pad_scr[CONV_HALO + DEC_SEQ:2 * CONV_HALO + DEC_SEQ, :] = border
            expand(jnp.concatenate([ck_ref[...], k_ref[...]], axis=0),
                   jnp.concatenate([cv_ref[...], v_ref[...]], axis=0), PAST_LEN + DEC_SEQ)

        mix(PAST_LEN + DEC_SEQ, t * ATT_TQ)


def _even_mix(x, mod_l, q, k, v, ck, cv, glu, conv_w, conv_b, cn_g, cn_b, w_out, j):
    nq = TM // ATT_TQ
    tok = lambda n: pl.BlockSpec((TM, n), lambda i, t: (i, 0))
    blk = pl.BlockSpec((ATT_TQ, D_MODEL), lambda i, t: (i * nq + t, 0))
    ctx = pl.BlockSpec((None, None, PAST_LEN, ATT_KV), lambda i, t: (jnp.maximum(i - N_TILES_P, 0), j, 0, 0))
    vec = pl.BlockSpec((None, 1, CONV_CH), lambda i, t: (j, 0, 0))
    n_even = conv_b.shape[0]
    n_keys = PAST_LEN + DEC_SEQ
    return pl.pallas_call(
        _even_mix_kernel,
        grid=(N_TILES, nq),
        in_specs=[
            tok(ATT_Q), tok(ATT_KV), tok(ATT_KV), ctx, ctx, tok(CONV_CH),
            pl.BlockSpec((None, CONV_K, CONV_CH), lambda i, t: (j, 0, 0)),
            vec, vec, vec,
            blk,
            pl.BlockSpec((1, N_MOD, D_MODEL), lambda i, t: (_cond_of_tile(i), 0, 0)),
            pl.BlockSpec((None, ATT_Q + CONV_CH, D_MODEL), lambda i, t: (j, 0, 0),
                         pipeline_mode=pl.Buffered(1)),
        ],
        out_specs=blk,
        out_shape=jax.ShapeDtypeStruct((NTOK, D_MODEL), F32),
        scratch_shapes=[
            pltpu.VMEM((N_KV_HEADS, n_keys, ATT_WIDTH), BF16),
            pltpu.VMEM((N_KV_HEADS, n_keys, ATT_WIDTH), BF16),
            pltpu.VMEM((max(SEQ_PER_TILE * CONV_PAD_SEQ, DEC_SEQ + 2 * CONV_HALO), CONV_CH), F32),
            pltpu.VMEM((CONV_WIN, CONV_CH), F32),
            pltpu.VMEM((V7X_SUBLANES, CONV_SPAN, CONV_CH), F32),
            pltpu.VMEM((ATT_TQ, CONV_CH), F32),
            pltpu.VMEM((ATT_TQ, ATT_Q), BF16),
            pltpu.VMEM((ATT_Q + CONV_CH, D_MODEL), BF16),
        ],
        compiler_params=_cparams("arbitrary", "arbitrary"),
        name="even_mix",
    )(q, k, v, ck, cv, glu, conv_w, conv_b.reshape(n_even, 1, CONV_CH),
      cn_g.reshape(n_even, 1, CONV_CH), cn_b.reshape(n_even, 1, CONV_CH), x, mod_l, w_out)


def _odd_in_kernel(x_ref, mod_ref, g_ref, w_ref, sg_ref, sb_ref, ws_ref, bs_ref, f_ref, sgu_ref, w_scr):
    @pl.when(pl.program_id(0) == 0)
    def _():
        w_scr[...] = w_ref[...].astype(BF16)

    h = _modulate(x_ref[...], g_ref[...], mod_ref[0, 3:4, :], mod_ref[0, 4:5, :]).astype(BF16)

    def proj(lo, hi):
        return _dot(h, w_scr[:, lo:hi])

    f_ref[...] = proj(0, FOURIER_CH).astype(BF16)
    u = proj(FOURIER_CH, FOURIER_CH + SGU_CH)
    v = _layer_norm(proj(FOURIER_CH + SGU_CH, ODD_IN), sg_ref[...], sb_ref[...]).astype(BF16)
    for g in range(SGU_GROUPS):
        wg = ws_ref[g].astype(BF16)
        bias = bs_ref[:, g:g + 1]
        cols = slice(g * SGU_GROUP_CH, (g + 1) * SGU_GROUP_CH)
        for n in range(TM // CHUNK):
            rows = slice(n * CHUNK, (n + 1) * CHUNK)
            mixed = _dot(wg, v[rows, cols]) + bias
            sgu_ref[rows, cols] = (u[rows, cols] * mixed).astype(BF16)


def _odd_in(x, mod_l, norm_g, od_w_in, sgu_norm_g, sgu_norm_b, sgu_w, sgu_b, layer, j):
    tok = lambda n: pl.BlockSpec((TM, n), lambda i: (i, 0))
    return pl.pallas_call(
        _odd_in_kernel,
        grid=(N_TILES,),
        in_specs=[
            tok(D_MODEL),
            pl.BlockSpec((1, N_MOD, D_MODEL), lambda i: (_cond_of_tile(i), 0, 0)),
            pl.BlockSpec((None, None, 1, D_MODEL), lambda i: (layer, 1, 0, 0)),
            pl.BlockSpec((None, D_MODEL, ODD_IN), lambda i: (j, 0, 0), pipeline_mode=pl.Buffered(1)),
            pl.BlockSpec((1, SGU_CH), lambda i: (0, 0)),
            pl.BlockSpec((1, SGU_CH), lambda i: (0, 0)),
            pl.BlockSpec((None, SGU_GROUPS, CHUNK, CHUNK), lambda i: (j, 0, 0, 0)),
            pl.BlockSpec((CHUNK, SGU_GROUPS), lambda i: (0, 0)),
        ],
        out_specs=[tok(FOURIER_CH), tok(SGU_CH)],
        out_shape=[jax.ShapeDtypeStruct((NTOK, FOURIER_CH), BF16),
                   jax.ShapeDtypeStruct((NTOK, SGU_CH), BF16)],
        scratch_shapes=[pltpu.VMEM((D_MODEL, ODD_IN), BF16)],
        compiler_params=_cparams("arbitrary"),
        name="odd_in",
    )(x, mod_l, norm_g.reshape(DEPTH, 3, 1, D_MODEL), od_w_in,
      sgu_norm_g[j].reshape(1, SGU_CH), sgu_norm_b[j].reshape(1, SGU_CH), sgu_w, sgu_b[j].T)


def _fourier_out_kernel(f_ref, wch_ref, wseq_p_ref, wseq_s_ref, res_ref, mod_ref, sgu_ref, wo_ref,
                        o_ref, four_scr, wch_scr, wseq_p_scr, wseq_s_scr, wo_scr):
    i = pl.program_id(0)

    @pl.when(i == 0)
    def _():
        wch_scr[...] = wch_ref[...].astype(BF16)
        wseq_p_scr[...] = wseq_p_ref[...].astype(BF16)
        wseq_s_scr[...] = wseq_s_ref[...].astype(BF16)
        wo_scr[...] = wo_ref[...].astype(BF16)

    pq = _dot(f_ref[...], wch_scr[...]).astype(BF16)

    def seq_dft(rows, wseq, seq):
        stacked = jnp.concatenate([pq[rows, :FOURIER_CH], pq[rows, FOURIER_CH:]], axis=0)
        scale = 1.0 / np.sqrt(float(seq * FOURIER_GROUP_CH))
        four_scr[rows, :] = (_dot(wseq, stacked) * scale).astype(BF16)

    @pl.when(i < N_TILES_P)
    def _():
        for s in range(SEQ_PER_TILE):
            seq_dft(slice(s * SEQ, (s + 1) * SEQ), wseq_p_scr[...], SEQ)

    @pl.when(i >= N_TILES_P)
    def _():
        seq_dft(slice(0, DEC_SEQ), wseq_s_scr[...], DEC_SEQ)

    gate = mod_ref[0, 5:6, :]
    for s in range(SEQ_PER_TILE):
        rows = slice(s * SEQ, (s + 1) * SEQ)
        mixed = (_dot(four_scr[rows, :], wo_scr[:FOURIER_CH, :])
                 + _dot(sgu_ref[rows, :], wo_scr[FOURIER_CH:, :]))
        o_ref[rows, :] = res_ref[rows, :] + gate * mixed


def _dft_constants():
    def cos_sin(n):
        idx = np.arange(n)
        ang = 2.0 * np.pi * ((idx[:, None] * idx[None, :]) % n) / n
        return np.cos(ang), np.sin(ang)

    cc, sc = cos_sin(FOURIER_GROUP_CH)
    eye = np.eye(FOURIER_GROUPS)
    wch = np.concatenate([np.kron(eye, cc), np.kron(eye, sc)], axis=1)

    def wseq(seq):
        cl, sl = cos_sin(seq)
        return jnp.asarray(np.concatenate([cl, -sl], axis=1), dtype=F32)

    return jnp.asarray(wch, dtype=F32), wseq(SEQ), wseq(DEC_SEQ)


def _fourier_out(x, mod_l, f, sgu, w_out, j):
    wch, wseq_p, wseq_s = _dft_constants()
    tok = lambda n: pl.BlockSpec((TM, n), lambda i: (i, 0))
    const = lambda shape: pl.BlockSpec(shape, lambda i: (0,) * len(shape), pipeline_mode=pl.Buffered(1))
    return pl.pallas_call(
        _fourier_out_kernel,
        grid=(N_TILES,),
        in_specs=[
            tok(FOURIER_CH), const(wch.shape), const(wseq_p.shape), const(wseq_s.shape),
            tok(D_MODEL),
            pl.BlockSpec((1, N_MOD, D_MODEL), lambda i: (_cond_of_tile(i), 0, 0)),
            tok(SGU_CH),
            pl.BlockSpec((None, FOURIER_CH + SGU_CH, D_MODEL), lambda i: (j, 0, 0),
                         pipeline_mode=pl.Buffered(1)),
        ],
        out_specs=tok(D_MODEL),
        out_shape=jax.ShapeDtypeStruct((NTOK, D_MODEL), F32),
        scratch_shapes=[pltpu.VMEM((TM, FOURIER_CH), BF16), pltpu.VMEM(wch.shape, BF16),
                        pltpu.VMEM(wseq_p.shape, BF16), pltpu.VMEM(wseq_s.shape, BF16),
                        pltpu.VMEM((FOURIER_CH + SGU_CH, D_MODEL), BF16)],
        compiler_params=_cparams("arbitrary"),
        name="fourier_out",
    )(f, wch, wseq_p, wseq_s, x, mod_l, sgu, w_out)


def kernel(x_prompt, x_sample, cache_k, cache_v, c, c_ctx, w_mod, b_mod, norm_g, ffn_w_in, ffn_w_out, ev_w_in, ev_w_out, q_norm_g, k_norm_g, conv_w, conv_b, conv_norm_g, conv_norm_b, od_w_in, od_w_out, sgu_norm_g, sgu_norm_b, sgu_w, sgu_b):
    cond = jnp.concatenate([c_ctx[None, :], c_ctx[None, :], c,
                            jnp.zeros((N_COND - COND_LATENT0 - DEC_BATCH, D_MODEL), F32)], axis=0)
    mod = _adaln(cond, w_mod, b_mod).reshape(DEPTH, N_COND, N_MOD, D_MODEL)
    ck = cache_k.reshape(DEC_BATCH, -1, PAST_LEN, ATT_KV)
    cv = cache_v.reshape(DEC_BATCH, -1, PAST_LEN, ATT_KV)

    ffn = functools.partial(_ffn, mod=mod, norm_g=norm_g, ffn_w_in=ffn_w_in, ffn_w_out=ffn_w_out)
    x = ffn((x_prompt.reshape(NTOK_P, D_MODEL), x_sample.reshape(NTOK_S, D_MODEL)), phases=[(0, 0)])
    new_k, new_v = [], []
    for layer in range(DEPTH):
        mod_l = mod[layer]
        j = layer // 2
        if layer % 2 == 0:
            q, k, v, glu, kc, vc = _even_in(x, mod_l, norm_g, ev_w_in, q_norm_g, k_norm_g, layer, j)
            new_k.append(kc.reshape(BATCH, SEQ, N_KV_HEADS, HEAD_DIM))
            new_v.append(vc.reshape(BATCH, SEQ, N_KV_HEADS, HEAD_DIM))
            x = _even_mix(x, mod_l, q, k, v, ck, cv, glu, conv_w, conv_b, conv_norm_g, conv_norm_b,
                          ev_w_out, j)
        else:
            f, sgu = _odd_in(x, mod_l, norm_g, od_w_in, sgu_norm_g, sgu_norm_b, sgu_w, sgu_b, layer, j)
            x = _fourier_out(x, mod_l, f, sgu, od_w_out, j)
        if layer < DEPTH - 1:
            x = ffn((x,), phases=[(layer, 1), (layer + 1, 0)])

    last = [(DEPTH - 1, 1)]
    y_prompt = ffn((x,), phases=last, row0=0, n_rows=NTOK_P)
    y_sample = ffn((x,), phases=last, row0=NTOK_P, n_rows=NTOK_S)
    return (y_prompt.reshape(BATCH, SEQ, D_MODEL), y_sample.reshape(DEC_BATCH, DEC_SEQ, D_MODEL),
            jnp.stack(new_k, axis=1), jnp.stack(new_v, axis=1))
```

```python
import functools

import numpy as np
import jax
import jax.numpy as jnp
from jax import lax
from jax.experimental import pallas as pl
from jax.experimental.pallas import tpu as pltpu

D_MODEL = 1024
BATCH = 32
SEQ = 256
DEPTH = 4
DEC_BATCH = 4
DEC_SEQ = 1024
PAST_LEN = 512
GRID_W = 64
N_HEADS = 8
N_KV_HEADS = 2
HEAD_DIM = 64
GQA_GROUP = N_HEADS // N_KV_HEADS
AXIS_ROPE_DIM = HEAD_DIM // 2
ROPE_THETA = 10000.0
ATT_Q = N_HEADS * HEAD_DIM
ATT_KV = N_KV_HEADS * HEAD_DIM
CONV_CH = 512
CONV_K = 31
FOURIER_GROUPS = 4
FOURIER_GROUP_CH = 128
FOURIER_CH = FOURIER_GROUPS * FOURIER_GROUP_CH
SGU_GROUPS = 4
SGU_GROUP_CH = 128
SGU_CH = SGU_GROUPS * SGU_GROUP_CH
CHUNK = 128
FFN_DIM = 2816
N_MOD = 9
EVEN_IN = ATT_Q + 2 * ATT_KV + 2 * CONV_CH
ODD_IN = FOURIER_CH + 2 * SGU_CH
RMS_EPS = 1e-6
LN_EPS = 1e-5
LOG2_E = 1.4426950408889634

BF16 = jnp.bfloat16
F32 = jnp.float32

V7X_SUBLANES = 8
V7X_LANES = 128
NTOK_P = BATCH * SEQ
NTOK_S = DEC_BATCH * DEC_SEQ
NTOK = NTOK_P + NTOK_S
TM = DEC_SEQ
N_TILES = NTOK // TM
N_TILES_P = NTOK_P // TM
SEQ_PER_TILE = TM // SEQ
N_COND = 8
COND_LATENT0 = 2
FFN_TM = 2 * TM
FFN_CHUNK = 256
N_FFN_CHUNKS = FFN_DIM // FFN_CHUNK
EVEN_IN_ROWS = 1024
ADALN_TN = 2304
CONV_ROWS = 128
CONV_HALO = 16
ATT_TQ = SEQ
V7X_VMEM_LIMIT_BYTES = 56 * 1024 * 1024


def _cparams(*sem):
    return pltpu.CompilerParams(dimension_semantics=sem, vmem_limit_bytes=V7X_VMEM_LIMIT_BYTES)


def _cond_of_tile(i, rows=TM):
    return jnp.maximum((i * rows - NTOK_P) // DEC_SEQ + COND_LATENT0, 0)


def _dot(a, b):
    return jnp.dot(a, b, preferred_element_type=F32)


def _modulate(x, g, shift, scale):
    ms = jnp.mean(x * x, axis=-1, keepdims=True)
    return (x * lax.rsqrt(ms + RMS_EPS)) * (g * (1.0 + scale)) + shift


def _layer_norm(x, g, b):
    mu = jnp.mean(x, axis=-1, keepdims=True)
    xc = x - mu
    var = jnp.mean(xc * xc, axis=-1, keepdims=True)
    return xc * lax.rsqrt(var + LN_EPS) * g + b


def _adaln_kernel(c_ref, w_ref, b_ref, o_ref):
    c = c_ref[...]
    s = (c * jax.nn.sigmoid(c)).astype(BF16)
    o_ref[...] = _dot(s, w_ref[...].astype(BF16)) + b_ref[...]


def _adaln(cond, w_mod, b_mod):
    n_out = N_MOD * D_MODEL
    return pl.pallas_call(
        _adaln_kernel,
        grid=(DEPTH, n_out // ADALN_TN),
        in_specs=[
            pl.BlockSpec((N_COND, D_MODEL), lambda l, n: (0, 0)),
            pl.BlockSpec((None, D_MODEL, ADALN_TN), lambda l, n: (l, 0, n)),
            pl.BlockSpec((None, 1, ADALN_TN), lambda l, n: (l, 0, n)),
        ],
        out_specs=pl.BlockSpec((None, N_COND, ADALN_TN), lambda l, n: (l, 0, n)),
        out_shape=jax.ShapeDtypeStruct((DEPTH, N_COND, n_out), F32),
        compiler_params=_cparams("parallel", "parallel"),
        name="adaln",
    )(cond, w_mod, b_mod.reshape(DEPTH, 1, n_out))


def _ffn_kernel(subs, n_x, x_split, *refs):
    n_ph = len(subs)
    x_refs = refs[:n_x]
    mod_refs = refs[n_x:n_x + n_ph]
    g_refs = refs[n_x + n_ph:n_x + 2 * n_ph]
    wg_ref, wu_ref, wo_ref, o_ref, h_scr = refs[n_x + 2 * n_ph:]
    i = pl.program_id(0)
    j = pl.program_id(1)
    halves = [(c, slice(c * TM, (c + 1) * TM)) for c in range(FFN_TM // TM)]

    if n_x == 1:
        @pl.when(j == 0)
        def _():
            o_ref[...] = x_refs[0][...]
    else:
        @pl.when((j == 0) & (i < x_split))
        def _():
            o_ref[...] = x_refs[0][...]

        @pl.when((j == 0) & (i >= x_split))
        def _():
            o_ref[...] = x_refs[1][...]

    for p, sub in enumerate(subs):
        @pl.when(j == p * N_FFN_CHUNKS)
        def _(p=p, sub=sub):
            for c, rows in halves:
                h = _modulate(o_ref[rows, :], g_refs[p][...], mod_refs[p][c, 3 * sub:3 * sub + 1, :],
                              mod_refs[p][c, 3 * sub + 1:3 * sub + 2, :])
                h_scr[rows, :] = h.astype(BF16)

    h = h_scr[...]
    gate = _dot(h, wg_ref[...].astype(BF16))
    up = _dot(h, wu_ref[...].astype(BF16))
    a = (gate * jax.nn.sigmoid(gate) * up).astype(BF16)
    wo = wo_ref[...]
    for c, rows in halves:
        out_gate = mod_refs[0][c, 3 * subs[0] + 2:3 * subs[0] + 3, :]
        if n_ph == 2:
            out_gate = jnp.where(j < N_FFN_CHUNKS, out_gate,
                                 mod_refs[1][c, 3 * subs[1] + 2:3 * subs[1] + 3, :])
        o_ref[rows, :] += _dot(a[rows, :], (wo * (0.5 * out_gate)).astype(BF16))


def _ffn(xs, mod, norm_g, ffn_w_in, ffn_w_out, phases, row0=0, n_rows=NTOK):
    per = FFN_TM // TM
    tile0 = row0 // FFN_TM
    n_ph = len(phases)
    layers = [p[0] for p in phases]
    whichs = [p[1] for p in phases]

    def by_phase(vals):
        if n_ph == 1:
            return lambda j: vals[0]
        return lambda j: vals[0] + (j // N_FFN_CHUNKS) * (vals[1] - vals[0])

    layer_of, which_of = by_phase(layers), by_phase(whichs)
    chunk_of = lambda j: j % N_FFN_CHUNKS
    if len(xs) == 1:
        x_specs = [pl.BlockSpec((FFN_TM, D_MODEL), lambda i, j: (tile0 + i, 0))]
        x_split = 0
    else:
        x_split = xs[0].shape[0] // FFN_TM
        x_specs = [
            pl.BlockSpec((FFN_TM, D_MODEL), lambda i, j: (jnp.minimum(i, x_split - 1), 0),
                         pipeline_mode=pl.Buffered(1)),
            pl.BlockSpec((FFN_TM, D_MODEL), lambda i, j: (jnp.maximum(i - x_split, 0), 0),
                         pipeline_mode=pl.Buffered(1)),
        ]
    mod_specs = [pl.BlockSpec((None, per, N_MOD, D_MODEL),
                              lambda i, j, l=l: (l, _cond_of_tile(tile0 + i, FFN_TM) // per, 0, 0))
                 for l in layers]
    g_specs = [pl.BlockSpec((None, None, 1, D_MODEL), lambda i, j, l=l, w=w: (l, 2 * w, 0, 0))
               for l, w in phases]
    return pl.pallas_call(
        functools.partial(_ffn_kernel, tuple(2 * w for w in whichs), len(xs), x_split),
        grid=(n_rows // FFN_TM, n_ph * N_FFN_CHUNKS),
        in_specs=x_specs + mod_specs + g_specs + [
            pl.BlockSpec((None, None, D_MODEL, FFN_CHUNK),
                         lambda i, j: (layer_of(j), which_of(j), 0, chunk_of(j))),
            pl.BlockSpec((None, None, D_MODEL, FFN_CHUNK),
                         lambda i, j: (layer_of(j), which_of(j), 0, N_FFN_CHUNKS + chunk_of(j))),
            pl.BlockSpec((None, None, FFN_CHUNK, D_MODEL),
                         lambda i, j: (layer_of(j), which_of(j), chunk_of(j), 0)),
        ],
        out_specs=pl.BlockSpec((FFN_TM, D_MODEL), lambda i, j: (i, 0)),
        out_shape=jax.ShapeDtypeStruct((n_rows, D_MODEL), F32),
        scratch_shapes=[pltpu.VMEM((FFN_TM, D_MODEL), BF16)],
        compiler_params=_cparams("parallel", "arbitrary"),
        name="ffn",
    )(*xs, *([mod] * n_ph), *([norm_g.reshape(DEPTH, 3, 1, D_MODEL)] * n_ph),
      ffn_w_in, ffn_w_in, ffn_w_out)


def _group_mean_sq(x, width):
    r = lax.broadcasted_iota(jnp.int32, (width, width), 0) // HEAD_DIM
    c = lax.broadcasted_iota(jnp.int32, (width, width), 1) // HEAD_DIM
    ones = (r == c).astype(BF16)
    return _dot((x * x).astype(BF16), ones) * (1.0 / HEAD_DIM)


def _head_norm_rope(x, gain, cos, sin_next, sin_prev):
    width = x.shape[-1]
    xn = x * lax.rsqrt(_group_mean_sq(x, width) + RMS_EPS) * gain
    nxt = pltpu.roll(xn, width - 1, 1)
    prv = pltpu.roll(xn, 1, 1)
    return xn * cos + nxt * sin_next + prv * sin_prev


def _even_in_kernel(x_ref, mod_ref, g_ref, w_ref, qg_ref, kg_ref,
                    cq_ref, snq_ref, spq_ref, ck_ref, snk_ref, spk_ref,
                    q_ref, k_ref, v_ref, glu_ref, kc_ref, vc_ref, w_scr):
    i = pl.program_id(0)

    @pl.when(i == 0)
    def _():
        w_scr[...] = w_ref[...].astype(BF16)

    h = _modulate(x_ref[...], g_ref[...], mod_ref[0, 3:4, :], mod_ref[0, 4:5, :]).astype(BF16)

    def proj(lo, hi):
        return _dot(h, w_scr[:, lo:hi])

    q = proj(0, ATT_Q)
    q = _head_norm_rope(q, qg_ref[...], cq_ref[...], snq_ref[...], spq_ref[...])
    q_ref[...] = (q * (HEAD_DIM ** -0.5 * LOG2_E)).astype(BF16)
    k = proj(ATT_Q, ATT_Q + ATT_KV)
    k = _head_norm_rope(k, kg_ref[...], ck_ref[...], snk_ref[...], spk_ref[...])
    v = proj(ATT_Q + ATT_KV, ATT_Q + 2 * ATT_KV)
    k_ref[...] = k
    v_ref[...] = v

    @pl.when(i < NTOK_P // EVEN_IN_ROWS)
    def _():
        kc_ref[...] = k
        vc_ref[...] = v

    a0 = ATT_Q + 2 * ATT_KV
    a = proj(a0, a0 + CONV_CH)
    gt = proj(a0 + CONV_CH, a0 + 2 * CONV_CH)
    glu_ref[...] = a * jax.nn.sigmoid(gt)


def _rope_tables():
    t = np.arange(DEC_SEQ)
    row = (t // GRID_W).astype(np.float64)
    col = (t % GRID_W).astype(np.float64)
    inv_freq = 1.0 / (ROPE_THETA ** (np.arange(0, AXIS_ROPE_DIM, 2, dtype=np.float64) / AXIS_ROPE_DIM))
    ang = np.concatenate([row[:, None] * inv_freq, col[:, None] * inv_freq], axis=-1)
    ang = np.repeat(ang, 2, axis=-1)
    even = (np.arange(HEAD_DIM) % 2 == 0)[None, :]
    cos = np.cos(ang)
    sin_next = np.where(even, -np.sin(ang), 0.0)
    sin_prev = np.where(even, 0.0, np.sin(ang))

    def both(tab):
        ident = np.ones_like(tab) if tab is cos else np.zeros_like(tab)
        return np.stack([np.tile(ident, (1, N_HEADS)), np.tile(tab, (1, N_HEADS))]).astype(np.float32)

    return both(cos), both(sin_next), both(sin_prev)


def _even_in(x, mod_l, norm_g, ev_w_in, q_norm_g, k_norm_g, layer, j):
    cos, sin_next, sin_prev = (jnp.asarray(t) for t in _rope_tables())
    qg = jnp.tile(q_norm_g[j], N_HEADS).reshape(1, ATT_Q)
    kg = jnp.tile(k_norm_g[j], N_KV_HEADS).reshape(1, ATT_KV)
    rows = EVEN_IN_ROWS
    per_seq = DEC_SEQ // rows
    first_latent = NTOK_P // rows
    tab_idx = lambda i: (jnp.minimum(i // first_latent, 1), i % per_seq, 0)
    tab_q = pl.BlockSpec((None, rows, ATT_Q), tab_idx)
    tab_k = pl.BlockSpec((None, rows, ATT_KV), tab_idx)
    tok = lambda n: pl.BlockSpec((rows, n), lambda i: (i, 0))
    cache = pl.BlockSpec((rows, ATT_KV), lambda i: (jnp.minimum(i, first_latent - 1), 0))
    return pl.pallas_call(
        _even_in_kernel,
        grid=(NTOK // rows,),
        in_specs=[
            tok(D_MODEL),
            pl.BlockSpec((1, N_MOD, D_MODEL), lambda i: (_cond_of_tile(i, rows), 0, 0)),
            pl.BlockSpec((None, None, 1, D_MODEL), lambda i: (layer, 1, 0, 0)),
            pl.BlockSpec((None, D_MODEL, EVEN_IN), lambda i: (j, 0, 0), pipeline_mode=pl.Buffered(1)),
            pl.BlockSpec((1, ATT_Q), lambda i: (0, 0)),
            pl.BlockSpec((1, ATT_KV), lambda i: (0, 0)),
            tab_q, tab_q, tab_q, tab_k, tab_k, tab_k,
        ],
        out_specs=[tok(ATT_Q), tok(ATT_KV), tok(ATT_KV), tok(CONV_CH), cache, cache],
        out_shape=[
            jax.ShapeDtypeStruct((NTOK, ATT_Q), BF16),
            jax.ShapeDtypeStruct((NTOK, ATT_KV), F32),
            jax.ShapeDtypeStruct((NTOK, ATT_KV), F32),
            jax.ShapeDtypeStruct((NTOK, CONV_CH), F32),
            jax.ShapeDtypeStruct((NTOK_P, ATT_KV), F32),
            jax.ShapeDtypeStruct((NTOK_P, ATT_KV), F32),
        ],
        scratch_shapes=[pltpu.VMEM((D_MODEL, EVEN_IN), BF16)],
        compiler_params=_cparams("arbitrary"),
        name="even_in",
    )(x, mod_l, norm_g.reshape(DEPTH, 3, 1, D_MODEL), ev_w_in, qg, kg,
      cos, sin_next, sin_prev, cos, sin_next, sin_prev)


ATT_WIDTH = GQA_GROUP * HEAD_DIM


def _expand_kv_head(x, kv):
    lane = lax.broadcasted_iota(jnp.int32, x.shape, 1)
    swapped = pltpu.roll(x, HEAD_DIM, 1)
    pair = jnp.where((lane // HEAD_DIM) == kv, x, swapped)
    return jnp.concatenate([pair, pair], axis=1)


def _attend(q_ref, q_rows, kbig, vbig, n_keys, attn_scr):
    lane_group = lax.broadcasted_iota(jnp.int32, (ATT_TQ, ATT_WIDTH), 1) // HEAD_DIM
    for kv in range(N_KV_HEADS):
        cols = slice(kv * ATT_WIDTH, (kv + 1) * ATT_WIDTH)
        q = q_ref[q_rows, cols]
        kb = kbig[kv, 0:n_keys, :]
        vb = vbig[kv, 0:n_keys, :]
        acc = jnp.zeros((ATT_TQ, ATT_WIDTH), F32)
        for g in range(GQA_GROUP):
            mine = lane_group == g
            qm = jnp.where(mine, q, jnp.zeros_like(q))
            s = lax.dot_general(qm, kb, (((1,), (1,)), ((), ())), preferred_element_type=F32)
            m = jnp.max(s, axis=-1, keepdims=True)
            p = jnp.exp2(s - m)
            denom = jnp.sum(p, axis=-1, keepdims=True)
            o = _dot(p.astype(BF16), vb)
            acc = jnp.where(mine, o / denom, acc)
        attn_scr[:, cols] = acc.astype(BF16)


CONV_PAD_SEQ = SEQ + 2 * CONV_HALO
CONV_WIN = ATT_TQ + 2 * CONV_HALO
CONV_SPAN = CONV_WIN - V7X_SUBLANES


def _conv_block(win0_scr, win_scr, w_ref, y_scr):
    off = CONV_HALO - CONV_K // 2
    for r0 in range(0, ATT_TQ, CONV_ROWS):
        for c0 in range(0, CONV_CH, V7X_LANES):
            cols = slice(c0, c0 + V7X_LANES)
            acc = jnp.zeros((CONV_ROWS, V7X_LANES), F32)
            for k in range(CONV_K):
                res = (off + k) % V7X_SUBLANES
                a = r0 + off + k - res
                if res == 0:
                    tap = win0_scr[a:a + CONV_ROWS, cols]
                else:
                    tap = win_scr[res - 1, a:a + CONV_ROWS, cols]
                acc = acc + tap * w_ref[k:k + 1, cols]
            y_scr[r0:r0 + CONV_ROWS, cols] = acc


def _even_mix_kernel(q_ref, k_ref, v_ref, ck_ref, cv_ref, glu_ref, w_ref, b_ref, g_ref, beta_ref,
                     res_ref, mod_ref, wo_ref, o_ref,
                     kbig, vbig, pad_scr, win0_scr, win_scr, y_scr, attn_scr, wo_scr):
    i = pl.program_id(0)
    t = pl.program_id(1)
    rows = pl.ds(pl.multiple_of(t * ATT_TQ, ATT_TQ), ATT_TQ)
    border = jnp.zeros((CONV_HALO, CONV_CH), F32)

    @pl.when((i == 0) & (t == 0))
    def _():
        wo_scr[...] = wo_ref[...].astype(BF16)

    def expand(k, v, n):
        for kv in range(N_KV_HEADS):
            kbig[kv, 0:n, :] = _expand_kv_head(k, kv).astype(BF16)
            vbig[kv, 0:n, :] = _expand_kv_head(v, kv).astype(BF16)

    def mix(n_keys, win_start):
        win0_scr[...] = pad_scr[pl.ds(pl.multiple_of(win_start, V7X_SUBLANES), CONV_WIN), :]
        for res in range(1, V7X_SUBLANES):
            win_scr[res - 1] = win0_scr[res:res + CONV_SPAN, :]
        _attend(q_ref, rows, kbig, vbig, n_keys, attn_scr)
        _conv_block(win0_scr, win_scr, w_ref, y_scr)
        y = _layer_norm(y_scr[...] + b_ref[...], g_ref[...], beta_ref[...])
        conv = (y * jax.nn.sigmoid(y)).astype(BF16)
        mixed = _dot(attn_scr[...], wo_scr[:ATT_Q, :]) + _dot(conv, wo_scr[ATT_Q:, :])
        o_ref[...] = res_ref[...] + mod_ref[0, 5:6, :] * mixed

    @pl.when(i < N_TILES_P)
    def _():
        @pl.when(t == 0)
        def _():
            for s in range(SEQ_PER_TILE):
                base = s * CONV_PAD_SEQ
                pad_scr[base:base + CONV_HALO, :] = border
                pad_scr[base + CONV_HALO:base + CONV_HALO + SEQ, :] = glu_ref[s * SEQ:(s + 1) * SEQ, :]
                pad_scr[base + CONV_HALO + SEQ:base + CONV_PAD_SEQ, :] = border

        expand(k_ref[rows, :], v_ref[rows, :], SEQ)
        mix(SEQ, t * CONV_PAD_SEQ)

    @pl.when(i >= N_TILES_P)
    def _():
        @pl.when(t == 0)
        def _():
            pad_scr[0:CONV_HALO, :] = border
            pad_scr[CONV_HALO:CONV_HALO + DEC_SEQ, :] = glu_ref[...]
            pad_scr[CONV_HALO + DEC_SEQ:2 * CONV_HALO + DEC_SEQ, :] = border
            expand(jnp.concatenate([ck_ref[...], k_ref[...]], axis=0),
                   jnp.concatenate([cv_ref[...], v_ref[...]], axis=0), PAST_LEN + DEC_SEQ)

        mix(PAST_LEN + DEC_SEQ, t * ATT_TQ)


def _even_mix(x, mod_l, q, k, v, ck, cv, glu, conv_w, conv_b, cn_g, cn_b, w_out, j):
    nq = TM // ATT_TQ
    tok = lambda n: pl.BlockSpec((TM, n), lambda i, t: (i, 0))
    blk = pl.BlockSpec((ATT_TQ, D_MODEL), lambda i, t: (i * nq + t, 0))
    ctx = pl.BlockSpec((None, None, PAST_LEN, ATT_KV), lambda i, t: (jnp.maximum(i - N_TILES_P, 0), j, 0, 0))
    vec = pl.BlockSpec((None, 1, CONV_CH), lambda i, t: (j, 0, 0))
    n_even = conv_b.shape[0]
    n_keys = PAST_LEN + DEC_SEQ
    return pl.pallas_call(
        _even_mix_kernel,
        grid=(N_TILES, nq),
        in_specs=[
            tok(ATT_Q), tok(ATT_KV), tok(ATT_KV), ctx, ctx, tok(CONV_CH),
            pl.BlockSpec((None, CONV_K, CONV_CH), lambda i, t: (j, 0, 0)),
            vec, vec, vec,
            blk,
            pl.BlockSpec((1, N_MOD, D_MODEL), lambda i, t: (_cond_of_tile(i), 0, 0)),
            pl.BlockSpec((None, ATT_Q + CONV_CH, D_MODEL), lambda i, t: (j, 0, 0),
                         pipeline_mode=pl.Buffered(1)),
        ],
        out_specs=blk,
        out_shape=jax.ShapeDtypeStruct((NTOK, D_MODEL), F32),
        scratch_shapes=[
            pltpu.VMEM((N_KV_HEADS, n_keys, ATT_WIDTH), BF16),
            pltpu.VMEM((N_KV_HEADS, n_keys, ATT_WIDTH), BF16),
            pltpu.VMEM((max(SEQ_PER_TILE * CONV_PAD_SEQ, DEC_SEQ + 2 * CONV_HALO), CONV_CH), F32),
            pltpu.VMEM((CONV_WIN, CONV_CH), F32),
            pltpu.VMEM((V7X_SUBLANES - 1, CONV_SPAN, CONV_CH), F32),
            pltpu.VMEM((ATT_TQ, CONV_CH), F32),
            pltpu.VMEM((ATT_TQ, ATT_Q), BF16),
            pltpu.VMEM((ATT_Q + CONV_CH, D_MODEL), BF16),
        ],
        compiler_params=_cparams("arbitrary", "arbitrary"),
        name="even_mix",
    )(q, k, v, ck, cv, glu, conv_w, conv_b.reshape(n_even, 1, CONV_CH),
      cn_g.reshape(n_even, 1, CONV_CH), cn_b.reshape(n_even, 1, CONV_CH), x, mod_l, w_out)


def _odd_in_kernel(x_ref, mod_ref, g_ref, w_ref, sg_ref, sb_ref, ws_ref, bs_ref, f_ref, sgu_ref, w_scr):
    @pl.when(pl.program_id(0) == 0)
    def _():
        w_scr[...] = w_ref[...].astype(BF16)

    h = _modulate(x_ref[...], g_ref[...], mod_ref[0, 3:4, :], mod_ref[0, 4:5, :]).astype(BF16)

    def proj(lo, hi):
        return _dot(h, w_scr[:, lo:hi])

    f_ref[...] = proj(0, FOURIER_CH).astype(BF16)
    u = proj(FOURIER_CH, FOURIER_CH + SGU_CH)
    v = _layer_norm(proj(FOURIER_CH + SGU_CH, ODD_IN), sg_ref[...], sb_ref[...]).astype(BF16)
    for g in range(SGU_GROUPS):
        wg = ws_ref[g].astype(BF16)
        bias = bs_ref[:, g:g + 1]
        cols = slice(g * SGU_GROUP_CH, (g + 1) * SGU_GROUP_CH)
        for n in range(TM // CHUNK):
            rows = slice(n * CHUNK, (n + 1) * CHUNK)
            mixed = _dot(wg, v[rows, cols]) + bias
            sgu_ref[rows, cols] = (u[rows, cols] * mixed).astype(BF16)


def _odd_in(x, mod_l, norm_g, od_w_in, sgu_norm_g, sgu_norm_b, sgu_w, sgu_b, layer, j):
    tok = lambda n: pl.BlockSpec((TM, n), lambda i: (i, 0))
    return pl.pallas_call(
        _odd_in_kernel,
        grid=(N_TILES,),
        in_specs=[
            tok(D_MODEL),
            pl.BlockSpec((1, N_MOD, D_MODEL), lambda i: (_cond_of_tile(i), 0, 0)),
            pl.BlockSpec((None, None, 1, D_MODEL), lambda i: (layer, 1, 0, 0)),
            pl.BlockSpec((None, D_MODEL, ODD_IN), lambda i: (j, 0, 0), pipeline_mode=pl.Buffered(1)),
            pl.BlockSpec((1, SGU_CH), lambda i: (0, 0)),
            pl.BlockSpec((1, SGU_CH), lambda i: (0, 0)),
            pl.BlockSpec((None, SGU_GROUPS, CHUNK, CHUNK), lambda i: (j, 0, 0, 0)),
            pl.BlockSpec((CHUNK, SGU_GROUPS), lambda i: (0, 0)),
        ],
        out_specs=[tok(FOURIER_CH), tok(SGU_CH)],
        out_shape=[jax.ShapeDtypeStruct((NTOK, FOURIER_CH), BF16),
                   jax.ShapeDtypeStruct((NTOK, SGU_CH), BF16)],
        scratch_shapes=[pltpu.VMEM((D_MODEL, ODD_IN), BF16)],
        compiler_params=_cparams("arbitrary"),
        name="odd_in",
    )(x, mod_l, norm_g.reshape(DEPTH, 3, 1, D_MODEL), od_w_in,
      sgu_norm_g[j].reshape(1, SGU_CH), sgu_norm_b[j].reshape(1, SGU_CH), sgu_w, sgu_b[j].T)


def _fourier_out_kernel(f_ref, wch_ref, wseq_p_ref, wseq_s_ref, res_ref, mod_ref, sgu_ref, wo_ref,
                        o_ref, four_scr, wch_scr, wseq_p_scr, wseq_s_scr, wo_scr):
    i = pl.program_id(0)

    @pl.when(i == 0)
    def _():
        wch_scr[...] = wch_ref[...].astype(BF16)
        wseq_p_scr[...] = wseq_p_ref[...].astype(BF16)
        wseq_s_scr[...] = wseq_s_ref[...].astype(BF16)
        wo_scr[...] = wo_ref[...].astype(BF16)

    pq = _dot(f_ref[...], wch_scr[...]).astype(BF16)

    def seq_dft(rows, wseq, seq):
        stacked = jnp.concatenate([pq[rows, :FOURIER_CH], pq[rows, FOURIER_CH:]], axis=0)
        scale = 1.0 / np.sqrt(float(seq * FOURIER_GROUP_CH))
        four_scr[rows, :] = (_dot(wseq, stacked) * scale).astype(BF16)

    @pl.when(i < N_TILES_P)
    def _():
        for s in range(SEQ_PER_TILE):
            seq_dft(slice(s * SEQ, (s + 1) * SEQ), wseq_p_scr[...], SEQ)

    @pl.when(i >= N_TILES_P)
    def _():
        seq_dft(slice(0, DEC_SEQ), wseq_s_scr[...], DEC_SEQ)

    gate = mod_ref[0, 5:6, :]
    for s in range(SEQ_PER_TILE):
        rows = slice(s * SEQ, (s + 1) * SEQ)
        mixed = (_dot(four_scr[rows, :], wo_scr[:FOURIER_CH, :])
                 + _dot(sgu_ref[rows, :], wo_scr[FOURIER_CH:, :]))
        o_ref[rows, :] = res_ref[rows, :] + gate * mixed


def _dft_constants():
    def cos_sin(n):
        idx = np.arange(n)
        ang = 2.0 * np.pi * ((idx[:, None] * idx[None, :]) % n) / n
        return np.cos(ang), np.sin(ang)

    cc, sc = cos_sin(FOURIER_GROUP_CH)
    eye = np.eye(FOURIER_GROUPS)
    wch = np.concatenate([np.kron(eye, cc), np.kron(eye, sc)], axis=1)

    def wseq(seq):
        cl, sl = cos_sin(seq)
        return jnp.asarray(np.concatenate([cl, -sl], axis=1), dtype=F32)

    return jnp.asarray(wch, dtype=F32), wseq(SEQ), wseq(DEC_SEQ)


def _fourier_out(x, mod_l, f, sgu, w_out, j):
    wch, wseq_p, wseq_s = _dft_constants()
    tok = lambda n: pl.BlockSpec((TM, n), lambda i: (i, 0))
    const = lambda shape: pl.BlockSpec(shape, lambda i: (0,) * len(shape), pipeline_mode=pl.Buffered(1))
    return pl.pallas_call(
        _fourier_out_kernel,
        grid=(N_TILES,),
        in_specs=[
            tok(FOURIER_CH), const(wch.shape), const(wseq_p.shape), const(wseq_s.shape),
            tok(D_MODEL),
            pl.BlockSpec((1, N_MOD, D_MODEL), lambda i: (_cond_of_tile(i), 0, 0)),
            tok(SGU_CH),
            pl.BlockSpec((None, FOURIER_CH + SGU_CH, D_MODEL), lambda i: (j, 0, 0),
                         pipeline_mode=pl.Buffered(1)),
        ],
        out_specs=tok(D_MODEL),
        out_shape=jax.ShapeDtypeStruct((NTOK, D_MODEL), F32),
        scratch_shapes=[pltpu.VMEM((TM, FOURIER_CH), BF16), pltpu.VMEM(wch.shape, BF16),
                        pltpu.VMEM(wseq_p.shape, BF16), pltpu.VMEM(wseq_s.shape, BF16),
                        pltpu.VMEM((FOURIER_CH + SGU_CH, D_MODEL), BF16)],
        compiler_params=_cparams("arbitrary"),
        name="fourier_out",
    )(f, wch, wseq_p, wseq_s, x, mod_l, sgu, w_out)


def kernel(x_prompt, x_sample, cache_k, cache_v, c, c_ctx, w_mod, b_mod, norm_g, ffn_w_in, ffn_w_out, ev_w_in, ev_w_out, q_norm_g, k_norm_g, conv_w, conv_b, conv_norm_g, conv_norm_b, od_w_in, od_w_out, sgu_norm_g, sgu_norm_b, sgu_w, sgu_b):
    cond = jnp.concatenate([c_ctx[None, :], c_ctx[None, :], c,
                            jnp.zeros((N_COND - COND_LATENT0 - DEC_BATCH, D_MODEL), F32)], axis=0)
    mod = _adaln(cond, w_mod, b_mod).reshape(DEPTH, N_COND, N_MOD, D_MODEL)
    ck = cache_k.reshape(DEC_BATCH, -1, PAST_LEN, ATT_KV)
    cv = cache_v.reshape(DEC_BATCH, -1, PAST_LEN, ATT_KV)

    ffn = functools.partial(_ffn, mod=mod, norm_g=norm_g, ffn_w_in=ffn_w_in, ffn_w_out=ffn_w_out)
    x = ffn((x_prompt.reshape(NTOK_P, D_MODEL), x_sample.reshape(NTOK_S, D_MODEL)), phases=[(0, 0)])
    new_k, new_v = [], []
    for layer in range(DEPTH):
        mod_l = mod[layer]
        j = layer // 2
        if layer % 2 == 0:
            q, k, v, glu, kc, vc = _even_in(x, mod_l, norm_g, ev_w_in, q_norm_g, k_norm_g, layer, j)
            new_k.append(kc.reshape(BATCH, SEQ, N_KV_HEADS, HEAD_DIM))
            new_v.append(vc.reshape(BATCH, SEQ, N_KV_HEADS, HEAD_DIM))
            x = _even_mix(x, mod_l, q, k, v, ck, cv, glu, conv_w, conv_b, conv_norm_g, conv_norm_b,
                          ev_w_out, j)
        else:
            f, sgu = _odd_in(x, mod_l, norm_g, od_w_in, sgu_norm_g, sgu_norm_b, sgu_w, sgu_b, layer, j)
            x = _fourier_out(x, mod_l, f, sgu, od_w_out, j)
        if layer < DEPTH - 1:
            x = ffn((x,), phases=[(layer, 1), (layer + 1, 0)])

    last = [(DEPTH - 1, 1)]
    y_prompt = ffn((x,), phases=last, row0=0, n_rows=NTOK_P)
    y_sample = ffn((x,), phases=last, row0=NTOK_P, n_rows=NTOK_S)
    return (y_prompt.reshape(BATCH, SEQ, D_MODEL), y_sample.reshape(DEC_BATCH, DEC_SEQ, D_MODEL),
            jnp.stack(new_k, axis=1), jnp.stack(new_v, axis=1))
```

```python
import functools

import numpy as np
import jax
import jax.numpy as jnp
from jax import lax
from jax.experimental import pallas as pl
from jax.experimental.pallas import tpu as pltpu

D_MODEL = 1024
BATCH = 32
SEQ = 256
DEPTH = 4
DEC_BATCH = 4
DEC_SEQ = 1024
PAST_LEN = 512
GRID_W = 64
N_HEADS = 8
N_KV_HEADS = 2
HEAD_DIM = 64
GQA_GROUP = N_HEADS // N_KV_HEADS
AXIS_ROPE_DIM = HEAD_DIM // 2
ROPE_THETA = 10000.0
ATT_Q = N_HEADS * HEAD_DIM
ATT_KV = N_KV_HEADS * HEAD_DIM
CONV_CH = 512
CONV_K = 31
FOURIER_GROUPS = 4
FOURIER_GROUP_CH = 128
FOURIER_CH = FOURIER_GROUPS * FOURIER_GROUP_CH
SGU_GROUPS = 4
SGU_GROUP_CH = 128
SGU_CH = SGU_GROUPS * SGU_GROUP_CH
CHUNK = 128
FFN_DIM = 2816
N_MOD = 9
EVEN_IN = ATT_Q + 2 * ATT_KV + 2 * CONV_CH
ODD_IN = FOURIER_CH + 2 * SGU_CH
RMS_EPS = 1e-6
LN_EPS = 1e-5
LOG2_E = 1.4426950408889634

BF16 = jnp.bfloat16
F32 = jnp.float32

V7X_SUBLANES = 8
V7X_LANES = 128
NTOK_P = BATCH * SEQ
NTOK_S = DEC_BATCH * DEC_SEQ
NTOK = NTOK_P + NTOK_S
TM = DEC_SEQ
N_TILES = NTOK // TM
N_TILES_P = NTOK_P // TM
SEQ_PER_TILE = TM // SEQ
N_COND = 8
COND_LATENT0 = 2
FFN_TM = 2 * TM
FFN_CHUNK = 256
N_FFN_CHUNKS = FFN_DIM // FFN_CHUNK
EVEN_IN_ROWS = 1024
ADALN_TN = 2304
CONV_ROWS = 128
CONV_HALO = 16
ATT_TQ = SEQ
V7X_VMEM_LIMIT_BYTES = 56 * 1024 * 1024


def _cparams(*sem):
    return pltpu.CompilerParams(dimension_semantics=sem, vmem_limit_bytes=V7X_VMEM_LIMIT_BYTES)


def _cond_of_tile(i, rows=TM):
    return jnp.maximum((i * rows - NTOK_P) // DEC_SEQ + COND_LATENT0, 0)


def _dot(a, b):
    return jnp.dot(a, b, preferred_element_type=F32)


def _modulate(x, g, shift, scale):
    ms = jnp.mean(x * x, axis=-1, keepdims=True)
    return (x * lax.rsqrt(ms + RMS_EPS)) * (g * (1.0 + scale)) + shift


def _layer_norm(x, g, b):
    mu = jnp.mean(x, axis=-1, keepdims=True)
    xc = x - mu
    var = jnp.mean(xc * xc, axis=-1, keepdims=True)
    return xc * lax.rsqrt(var + LN_EPS) * g + b


def _adaln_kernel(c_ref, w_ref, b_ref, o_ref):
    c = c_ref[...]
    s = (c * jax.nn.sigmoid(c)).astype(BF16)
    o_ref[...] = _dot(s, w_ref[...].astype(BF16)) + b_ref[...]


def _adaln(cond, w_mod, b_mod):
    n_out = N_MOD * D_MODEL
    return pl.pallas_call(
        _adaln_kernel,
        grid=(DEPTH, n_out // ADALN_TN),
        in_specs=[
            pl.BlockSpec((N_COND, D_MODEL), lambda l, n: (0, 0)),
            pl.BlockSpec((None, D_MODEL, ADALN_TN), lambda l, n: (l, 0, n)),
            pl.BlockSpec((None, 1, ADALN_TN), lambda l, n: (l, 0, n)),
        ],
        out_specs=pl.BlockSpec((None, N_COND, ADALN_TN), lambda l, n: (l, 0, n)),
        out_shape=jax.ShapeDtypeStruct((DEPTH, N_COND, n_out), F32),
        compiler_params=_cparams("parallel", "parallel"),
        name="adaln",
    )(cond, w_mod, b_mod.reshape(DEPTH, 1, n_out))


def _ffn_kernel(subs, n_x, x_split, *refs):
    n_ph = len(subs)
    x_refs = refs[:n_x]
    mod_refs = refs[n_x:n_x + n_ph]
    g_refs = refs[n_x + n_ph:n_x + 2 * n_ph]
    wg_ref, wu_ref, wo_ref, o_ref, h_scr = refs[n_x + 2 * n_ph:]
    i = pl.program_id(0)
    j = pl.program_id(1)
    halves = [(c, slice(c * TM, (c + 1) * TM)) for c in range(FFN_TM // TM)]

    if n_x == 1:
        @pl.when(j == 0)
        def _():
            o_ref[...] = x_refs[0][...]
    else:
        @pl.when((j == 0) & (i < x_split))
        def _():
            o_ref[...] = x_refs[0][...]

        @pl.when((j == 0) & (i >= x_split))
        def _():
            o_ref[...] = x_refs[1][...]

    def chunk(h, rows, out_gate):
        gate = _dot(h, wg_ref[...].astype(BF16))
        up = _dot(h, wu_ref[...].astype(BF16))
        a = (gate * jax.nn.sigmoid(gate) * up).astype(BF16)
        o_ref[rows, :] += _dot(a, (wo_ref[...] * (0.5 * out_gate)).astype(BF16))

    for p, sub in enumerate(subs):
        @pl.when(j == p * N_FFN_CHUNKS)
        def _(p=p, sub=sub):
            for c, rows in halves:
                h = _modulate(o_ref[rows, :], g_refs[p][...], mod_refs[p][c, 3 * sub:3 * sub + 1, :],
                              mod_refs[p][c, 3 * sub + 1:3 * sub + 2, :]).astype(BF16)
                h_scr[rows, :] = h
                chunk(h, rows, mod_refs[p][c, 3 * sub + 2:3 * sub + 3, :])

    @pl.when(j % N_FFN_CHUNKS != 0)
    def _():
        for c, rows in halves:
            out_gate = mod_refs[0][c, 3 * subs[0] + 2:3 * subs[0] + 3, :]
            if n_ph == 2:
                out_gate = jnp.where(j < N_FFN_CHUNKS, out_gate,
                                     mod_refs[1][c, 3 * subs[1] + 2:3 * subs[1] + 3, :])
            chunk(h_scr[rows, :], rows, out_gate)


def _ffn(xs, mod, norm_g, ffn_w_in, ffn_w_out, phases, row0=0, n_rows=NTOK):
    per = FFN_TM // TM
    tile0 = row0 // FFN_TM
    n_ph = len(phases)
    layers = [p[0] for p in phases]
    whichs = [p[1] for p in phases]

    def by_phase(vals):
        if n_ph == 1:
            return lambda j: vals[0]
        return lambda j: vals[0] + (j // N_FFN_CHUNKS) * (vals[1] - vals[0])

    layer_of, which_of = by_phase(layers), by_phase(whichs)
    chunk_of = lambda j: j % N_FFN_CHUNKS
    if len(xs) == 1:
        x_specs = [pl.BlockSpec((FFN_TM, D_MODEL), lambda i, j: (tile0 + i, 0))]
        x_split = 0
    else:
        x_split = xs[0].shape[0] // FFN_TM
        x_specs = [
            pl.BlockSpec((FFN_TM, D_MODEL), lambda i, j: (jnp.minimum(i, x_split - 1), 0),
                         pipeline_mode=pl.Buffered(1)),
            pl.BlockSpec((FFN_TM, D_MODEL), lambda i, j: (jnp.maximum(i - x_split, 0), 0),
                         pipeline_mode=pl.Buffered(1)),
        ]
    mod_specs = [pl.BlockSpec((None, per, N_MOD, D_MODEL),
                              lambda i, j, l=l: (l, _cond_of_tile(tile0 + i, FFN_TM) // per, 0, 0))
                 for l in layers]
    g_specs = [pl.BlockSpec((None, None, 1, D_MODEL), lambda i, j, l=l, w=w: (l, 2 * w, 0, 0))
               for l, w in phases]
    return pl.pallas_call(
        functools.partial(_ffn_kernel, tuple(2 * w for w in whichs), len(xs), x_split),
        grid=(n_rows // FFN_TM, n_ph * N_FFN_CHUNKS),
        in_specs=x_specs + mod_specs + g_specs + [
            pl.BlockSpec((None, None, D_MODEL, FFN_CHUNK),
                         lambda i, j: (layer_of(j), which_of(j), 0, chunk_of(j))),
            pl.BlockSpec((None, None, D_MODEL, FFN_CHUNK),
                         lambda i, j: (layer_of(j), which_of(j), 0, N_FFN_CHUNKS + chunk_of(j))),
            pl.BlockSpec((None, None, FFN_CHUNK, D_MODEL),
                         lambda i, j: (layer_of(j), which_of(j), chunk_of(j), 0)),
        ],
        out_specs=pl.BlockSpec((FFN_TM, D_MODEL), lambda i, j: (i, 0)),
        out_shape=jax.ShapeDtypeStruct((n_rows, D_MODEL), F32),
        scratch_shapes=[pltpu.VMEM((FFN_TM, D_MODEL), BF16)],
        compiler_params=_cparams("parallel", "arbitrary"),
        name="ffn",
    )(*xs, *([mod] * n_ph), *([norm_g.reshape(DEPTH, 3, 1, D_MODEL)] * n_ph),
      ffn_w_in, ffn_w_in, ffn_w_out)


def _group_mean_sq(x, width):
    r = lax.broadcasted_iota(jnp.int32, (width, width), 0) // HEAD_DIM
    c = lax.broadcasted_iota(jnp.int32, (width, width), 1) // HEAD_DIM
    ones = (r == c).astype(BF16)
    return _dot((x * x).astype(BF16), ones) * (1.0 / HEAD_DIM)


def _head_norm_rope(x, gain, cos, sin_next, sin_prev):
    width = x.shape[-1]
    xn = x * lax.rsqrt(_group_mean_sq(x, width) + RMS_EPS) * gain
    nxt = pltpu.roll(xn, width - 1, 1)
    prv = pltpu.roll(xn, 1, 1)
    return xn * cos + nxt * sin_next + prv * sin_prev


def _even_in_kernel(x_ref, mod_ref, g_ref, w_ref, qg_ref, kg_ref,
                    cq_ref, snq_ref, spq_ref, ck_ref, snk_ref, spk_ref,
                    q_ref, k_ref, v_ref, glu_ref, kc_ref, vc_ref, w_scr):
    i = pl.program_id(0)

    @pl.when(i == 0)
    def _():
        w_scr[...] = w_ref[...].astype(BF16)

    h = _modulate(x_ref[...], g_ref[...], mod_ref[0, 3:4, :], mod_ref[0, 4:5, :]).astype(BF16)

    def proj(lo, hi):
        return _dot(h, w_scr[:, lo:hi])

    q = proj(0, ATT_Q)
    q = _head_norm_rope(q, qg_ref[...], cq_ref[...], snq_ref[...], spq_ref[...])
    q_ref[...] = (q * (HEAD_DIM ** -0.5 * LOG2_E)).astype(BF16)
    k = proj(ATT_Q, ATT_Q + ATT_KV)
    k = _head_norm_rope(k, kg_ref[...], ck_ref[...], snk_ref[...], spk_ref[...])
    v = proj(ATT_Q + ATT_KV, ATT_Q + 2 * ATT_KV)
    k_ref[...] = k
    v_ref[...] = v

    @pl.when(i < NTOK_P // EVEN_IN_ROWS)
    def _():
        kc_ref[...] = k
        vc_ref[...] = v

    a0 = ATT_Q + 2 * ATT_KV
    a = proj(a0, a0 + CONV_CH)
    gt = proj(a0 + CONV_CH, a0 + 2 * CONV_CH)
    glu_ref[...] = a * jax.nn.sigmoid(gt)


def _rope_tables():
    t = np.arange(DEC_SEQ)
    row = (t // GRID_W).astype(np.float64)
    col = (t % GRID_W).astype(np.float64)
    inv_freq = 1.0 / (ROPE_THETA ** (np.arange(0, AXIS_ROPE_DIM, 2, dtype=np.float64) / AXIS_ROPE_DIM))
    ang = np.concatenate([row[:, None] * inv_freq, col[:, None] * inv_freq], axis=-1)
    ang = np.repeat(ang, 2, axis=-1)
    even = (np.arange(HEAD_DIM) % 2 == 0)[None, :]
    cos = np.cos(ang)
    sin_next = np.where(even, -np.sin(ang), 0.0)
    sin_prev = np.where(even, 0.0, np.sin(ang))

    def both(tab):
        ident = np.ones_like(tab) if tab is cos else np.zeros_like(tab)
        return np.stack([np.tile(ident, (1, N_HEADS)), np.tile(tab, (1, N_HEADS))]).astype(np.float32)

    return both(cos), both(sin_next), both(sin_prev)


def _even_in(x, mod_l, norm_g, ev_w_in, q_norm_g, k_norm_g, layer, j):
    cos, sin_next, sin_prev = (jnp.asarray(t) for t in _rope_tables())
    qg = jnp.tile(q_norm_g[j], N_HEADS).reshape(1, ATT_Q)
    kg = jnp.tile(k_norm_g[j], N_KV_HEADS).reshape(1, ATT_KV)
    rows = EVEN_IN_ROWS
    per_seq = DEC_SEQ // rows
    first_latent = NTOK_P // rows
    tab_idx = lambda i: (jnp.minimum(i // first_latent, 1), i % per_seq, 0)
    tab_q = pl.BlockSpec((None, rows, ATT_Q), tab_idx)
    tab_k = pl.BlockSpec((None, rows, ATT_KV), tab_idx)
    tok = lambda n: pl.BlockSpec((rows, n), lambda i: (i, 0))
    cache = pl.BlockSpec((rows, ATT_KV), lambda i: (jnp.minimum(i, first_latent - 1), 0))
    return pl.pallas_call(
        _even_in_kernel,
        grid=(NTOK // rows,),
        in_specs=[
            tok(D_MODEL),
            pl.BlockSpec((1, N_MOD, D_MODEL), lambda i: (_cond_of_tile(i, rows), 0, 0)),
            pl.BlockSpec((None, None, 1, D_MODEL), lambda i: (layer, 1, 0, 0)),
            pl.BlockSpec((None, D_MODEL, EVEN_IN), lambda i: (j, 0, 0), pipeline_mode=pl.Buffered(1)),
            pl.BlockSpec((1, ATT_Q), lambda i: (0, 0)),
            pl.BlockSpec((1, ATT_KV), lambda i: (0, 0)),
            tab_q, tab_q, tab_q, tab_k, tab_k, tab_k,
        ],
        out_specs=[tok(ATT_Q), tok(ATT_KV), tok(ATT_KV), tok(CONV_CH), cache, cache],
        out_shape=[
            jax.ShapeDtypeStruct((NTOK, ATT_Q), BF16),
            jax.ShapeDtypeStruct((NTOK, ATT_KV), F32),
            jax.ShapeDtypeStruct((NTOK, ATT_KV), F32),
            jax.ShapeDtypeStruct((NTOK, CONV_CH), F32),
            jax.ShapeDtypeStruct((NTOK_P, ATT_KV), F32),
            jax.ShapeDtypeStruct((NTOK_P, ATT_KV), F32),
        ],
        scratch_shapes=[pltpu.VMEM((D_MODEL, EVEN_IN), BF16)],
        compiler_params=_cparams("arbitrary"),
        name="even_in",
    )(x, mod_l, norm_g.reshape(DEPTH, 3, 1, D_MODEL), ev_w_in, qg, kg,
      cos, sin_next, sin_prev, cos, sin_next, sin_prev)


ATT_WIDTH = GQA_GROUP * HEAD_DIM


def _expand_kv_head(x, kv):
    lane = lax.broadcasted_iota(jnp.int32, x.shape, 1)
    swapped = pltpu.roll(x, HEAD_DIM, 1)
    pair = jnp.where((lane // HEAD_DIM) == kv, x, swapped)
    return jnp.concatenate([pair, pair], axis=1)


def _attend(q_ref, q_rows, kbig, vbig, n_keys, attn_scr):
    lane_group = lax.broadcasted_iota(jnp.int32, (ATT_TQ, ATT_WIDTH), 1) // HEAD_DIM
    for kv in range(N_KV_HEADS):
        cols = slice(kv * ATT_WIDTH, (kv + 1) * ATT_WIDTH)
        q = q_ref[q_rows, cols]
        kb = kbig[kv, 0:n_keys, :]
        vb = vbig[kv, 0:n_keys, :]
        acc = jnp.zeros((ATT_TQ, ATT_WIDTH), F32)
        for g in range(GQA_GROUP):
            mine = lane_group == g
            qm = jnp.where(mine, q, jnp.zeros_like(q))
            s = lax.dot_general(qm, kb, (((1,), (1,)), ((), ())), preferred_element_type=F32)
            m = jnp.max(s, axis=-1, keepdims=True)
            p = jnp.exp2(s - m)
            denom = jnp.sum(p, axis=-1, keepdims=True)
            o = _dot(p.astype(BF16), vb)
            acc = jnp.where(mine, o / denom, acc)
        attn_scr[:, cols] = acc.astype(BF16)


CONV_PAD_SEQ = SEQ + 2 * CONV_HALO
CONV_WIN = ATT_TQ + 2 * CONV_HALO
CONV_SPAN = CONV_WIN - V7X_SUBLANES


def _conv_block(win0_scr, win_scr, w_ref, y_scr):
    off = CONV_HALO - CONV_K // 2
    for r0 in range(0, ATT_TQ, CONV_ROWS):
        for c0 in range(0, CONV_CH, V7X_LANES):
            cols = slice(c0, c0 + V7X_LANES)
            acc = jnp.zeros((CONV_ROWS, V7X_LANES), F32)
            for k in range(CONV_K):
                res = (off + k) % V7X_SUBLANES
                a = r0 + off + k - res
                if res == 0:
                    tap = win0_scr[a:a + CONV_ROWS, cols]
                else:
                    tap = win_scr[res - 1, a:a + CONV_ROWS, cols]
                acc = acc + tap * w_ref[k:k + 1, cols]
            y_scr[r0:r0 + CONV_ROWS, cols] = acc


def _even_mix_kernel(q_ref, k_ref, v_ref, ck_ref, cv_ref, glu_ref, w_ref, b_ref, g_ref, beta_ref,
                     res_ref, mod_ref, wo_ref, o_ref,
                     kbig, vbig, pad_scr, win0_scr, win_scr, y_scr, attn_scr, wo_scr):
    i = pl.program_id(0)
    t = pl.program_id(1)
    rows = pl.ds(pl.multiple_of(t * ATT_TQ, ATT_TQ), ATT_TQ)
    border = jnp.zeros((CONV_HALO, CONV_CH), F32)

    @pl.when((i == 0) & (t == 0))
    def _():
        wo_scr[...] = wo_ref[...].astype(BF16)

    def expand(k, v, n):
        for kv in range(N_KV_HEADS):
            kbig[kv, 0:n, :] = _expand_kv_head(k, kv).astype(BF16)
            vbig[kv, 0:n, :] = _expand_kv_head(v, kv).astype(BF16)

    def mix(n_keys, win_start):
        win0_scr[...] = pad_scr[pl.ds(pl.multiple_of(win_start, V7X_SUBLANES), CONV_WIN), :]
        for res in range(1, V7X_SUBLANES):
            win_scr[res - 1] = win0_scr[res:res + CONV_SPAN, :]
        _attend(q_ref, rows, kbig, vbig, n_keys, attn_scr)
        _conv_block(win0_scr, win_scr, w_ref, y_scr)
        y = _layer_norm(y_scr[...] + b_ref[...], g_ref[...], beta_ref[...])
        conv = (y * jax.nn.sigmoid(y)).astype(BF16)
        mixed = _dot(attn_scr[...], wo_scr[:ATT_Q, :]) + _dot(conv, wo_scr[ATT_Q:, :])
        o_ref[...] = res_ref[...] + mod_ref[0, 5:6, :] * mixed

    @pl.when(i < N_TILES_P)
    def _():
        @pl.when(t == 0)
        def _():
            for s in range(SEQ_PER_TILE):
                base = s * CONV_PAD_SEQ
                pad_scr[base:base + CONV_HALO, :] = border
                pad_scr[base + CONV_HALO:base + CONV_HALO + SEQ, :] = glu_ref[s * SEQ:(s + 1) * SEQ, :]
                pad_scr[base + CONV_HALO + SEQ:base + CONV_PAD_SEQ, :] = border

        expand(k_ref[rows, :], v_ref[rows, :], SEQ)
        mix(SEQ, t * CONV_PAD_SEQ)

    @pl.when(i >= N_TILES_P)
    def _():
        @pl.when(t == 0)
        def _():
            pad_scr[0:CONV_HALO, :] = border
            pad_scr[CONV_HALO:CONV_HALO + DEC_SEQ, :] = glu_ref[...]
            pad_scr[CONV_HALO + DEC_SEQ:2 * CONV_HALO + DEC_SEQ, :] = border
            expand(jnp.concatenate([ck_ref[...], k_ref[...]], axis=0),
                   jnp.concatenate([cv_ref[...], v_ref[...]], axis=0), PAST_LEN + DEC_SEQ)

        mix(PAST_LEN + DEC_SEQ, t * ATT_TQ)


def _even_mix(x, mod_l, q, k, v, ck, cv, glu, conv_w, conv_b, cn_g, cn_b, w_out, j):
    nq = TM // ATT_TQ
    tok = lambda n: pl.BlockSpec((TM, n), lambda i, t: (i, 0))
    blk = pl.BlockSpec((ATT_TQ, D_MODEL), lambda i, t: (i * nq + t, 0))
    ctx = pl.BlockSpec((None, None, PAST_LEN, ATT_KV), lambda i, t: (jnp.maximum(i - N_TILES_P, 0), j, 0, 0))
    vec = pl.BlockSpec((None, 1, CONV_CH), lambda i, t: (j, 0, 0))
    n_even = conv_b.shape[0]
    n_keys = PAST_LEN + DEC_SEQ
    return pl.pallas_call(
        _even_mix_kernel,
        grid=(N_TILES, nq),
        in_specs=[
            tok(ATT_Q), tok(ATT_KV), tok(ATT_KV), ctx, ctx, tok(CONV_CH),
            pl.BlockSpec((None, CONV_K, CONV_CH), lambda i, t: (j, 0, 0)),
            vec, vec, vec,
            blk,
            pl.BlockSpec((1, N_MOD, D_MODEL), lambda i, t: (_cond_of_tile(i), 0, 0)),
            pl.BlockSpec((None, ATT_Q + CONV_CH, D_MODEL), lambda i, t: (j, 0, 0),
                         pipeline_mode=pl.Buffered(1)),
        ],
        out_specs=blk,
        out_shape=jax.ShapeDtypeStruct((NTOK, D_MODEL), F32),
        scratch_shapes=[
            pltpu.VMEM((N_KV_HEADS, n_keys, ATT_WIDTH), BF16),
            pltpu.VMEM((N_KV_HEADS, n_keys, ATT_WIDTH), BF16),
            pltpu.VMEM((max(SEQ_PER_TILE * CONV_PAD_SEQ, DEC_SEQ + 2 * CONV_HALO), CONV_CH), F32),
            pltpu.VMEM((CONV_WIN, CONV_CH), F32),
            pltpu.VMEM((V7X_SUBLANES - 1, CONV_SPAN, CONV_CH), F32),
            pltpu.VMEM((ATT_TQ, CONV_CH), F32),
            pltpu.VMEM((ATT_TQ, ATT_Q), BF16),
            pltpu.VMEM((ATT_Q + CONV_CH, D_MODEL), BF16),
        ],
        compiler_params=_cparams("arbitrary", "arbitrary"),
        name="even_mix",
    )(q, k, v, ck, cv, glu, conv_w, conv_b.reshape(n_even, 1, CONV_CH),
      cn_g.reshape(n_even, 1, CONV_CH), cn_b.reshape(n_even, 1, CONV_CH), x, mod_l, w_out)


def _odd_in_kernel(x_ref, mod_ref, g_ref, w_ref, sg_ref, sb_ref, ws_ref, bs_ref, f_ref, sgu_ref, w_scr):
    @pl.when(pl.program_id(0) == 0)
    def _():
        w_scr[...] = w_ref[...].astype(BF16)

    h = _modulate(x_ref[...], g_ref[...], mod_ref[0, 3:4, :], mod_ref[0, 4:5, :]).astype(BF16)

    def proj(lo, hi):
        return _dot(h, w_scr[:, lo:hi])

    f_ref[...] = proj(0, FOURIER_CH).astype(BF16)
    u = proj(FOURIER_CH, FOURIER_CH + SGU_CH)
    v = _layer_norm(proj(FOURIER_CH + SGU_CH, ODD_IN), sg_ref[...], sb_ref[...]).astype(BF16)
    for g in range(SGU_GROUPS):
        wg = ws_ref[g].astype(BF16)
        bias = bs_ref[:, g:g + 1]
        cols = slice(g * SGU_GROUP_CH, (g + 1) * SGU_GROUP_CH)
        for n in range(TM // CHUNK):
            rows = slice(n * CHUNK, (n + 1) * CHUNK)
            mixed = _dot(wg, v[rows, cols]) + bias
            sgu_ref[rows, cols] = (u[rows, cols] * mixed).astype(BF16)


def _odd_in(x, mod_l, norm_g, od_w_in, sgu_norm_g, sgu_norm_b, sgu_w, sgu_b, layer, j):
    tok = lambda n: pl.BlockSpec((TM, n), lambda i: (i, 0))
    return pl.pallas_call(
        _odd_in_kernel,
        grid=(N_TILES,),
        in_specs=[
            tok(D_MODEL),
            pl.BlockSpec((1, N_MOD, D_MODEL), lambda i: (_cond_of_tile(i), 0, 0)),
            pl.BlockSpec((None, None, 1, D_MODEL), lambda i: (layer, 1, 0, 0)),
            pl.BlockSpec((None, D_MODEL, ODD_IN), lambda i: (j, 0, 0), pipeline_mode=pl.Buffered(1)),
            pl.BlockSpec((1, SGU_CH), lambda i: (0, 0)),
            pl.BlockSpec((1, SGU_CH), lambda i: (0, 0)),
            pl.BlockSpec((None, SGU_GROUPS, CHUNK, CHUNK), lambda i: (j, 0, 0, 0)),
            pl.BlockSpec((CHUNK, SGU_GROUPS), lambda i: (0, 0)),
        ],
        out_specs=[tok(FOURIER_CH), tok(SGU_CH)],
        out_shape=[jax.ShapeDtypeStruct((NTOK, FOURIER_CH), BF16),
                   jax.ShapeDtypeStruct((NTOK, SGU_CH), BF16)],
        scratch_shapes=[pltpu.VMEM((D_MODEL, ODD_IN), BF16)],
        compiler_params=_cparams("arbitrary"),
        name="odd_in",
    )(x, mod_l, norm_g.reshape(DEPTH, 3, 1, D_MODEL), od_w_in,
      sgu_norm_g[j].reshape(1, SGU_CH), sgu_norm_b[j].reshape(1, SGU_CH), sgu_w, sgu_b[j].T)


def _fourier_out_kernel(f_ref, wch_ref, wseq_p_ref, wseq_s_ref, res_ref, mod_ref, sgu_ref, wo_ref,
                        o_ref, four_scr, wch_scr, wseq_p_scr, wseq_s_scr, wo_scr):
    i = pl.program_id(0)

    @pl.when(i == 0)
    def _():
        wch_scr[...] = wch_ref[...].astype(BF16)
        wseq_p_scr[...] = wseq_p_ref[...].astype(BF16)
        wseq_s_scr[...] = wseq_s_ref[...].astype(BF16)
        wo_scr[...] = wo_ref[...].astype(BF16)

    pq = _dot(f_ref[...], wch_scr[...]).astype(BF16)

    def seq_dft(rows, wseq, seq):
        stacked = jnp.concatenate([pq[rows, :FOURIER_CH], pq[rows, FOURIER_CH:]], axis=0)
        scale = 1.0 / np.sqrt(float(seq * FOURIER_GROUP_CH))
        four_scr[rows, :] = (_dot(wseq, stacked) * scale).astype(BF16)

    @pl.when(i < N_TILES_P)
    def _():
        for s in range(SEQ_PER_TILE):
            seq_dft(slice(s * SEQ, (s + 1) * SEQ), wseq_p_scr[...], SEQ)

    @pl.when(i >= N_TILES_P)
    def _():
        seq_dft(slice(0, DEC_SEQ), wseq_s_scr[...], DEC_SEQ)

    gate = mod_ref[0, 5:6, :]
    for s in range(SEQ_PER_TILE):
        rows = slice(s * SEQ, (s + 1) * SEQ)
        mixed = (_dot(four_scr[rows, :], wo_scr[:FOURIER_CH, :])
                 + _dot(sgu_ref[rows, :], wo_scr[FOURIER_CH:, :]))
        o_ref[rows, :] = res_ref[rows, :] + gate * mixed


def _dft_constants():
    def cos_sin(n):
        idx = np.arange(n)
        ang = 2.0 * np.pi * ((idx[:, None] * idx[None, :]) % n) / n
        return np.cos(ang), np.sin(ang)

    cc, sc = cos_sin(FOURIER_GROUP_CH)
    eye = np.eye(FOURIER_GROUPS)
    wch = np.concatenate([np.kron(eye, cc), np.kron(eye, sc)], axis=1)

    def wseq(seq):
        cl, sl = cos_sin(seq)
        return jnp.asarray(np.concatenate([cl, -sl], axis=1), dtype=F32)

    return jnp.asarray(wch, dtype=F32), wseq(SEQ), wseq(DEC_SEQ)


def _fourier_out(x, mod_l, f, sgu, w_out, j):
    wch, wseq_p, wseq_s = _dft_constants()
    tok = lambda n: pl.BlockSpec((TM, n), lambda i: (i, 0))
    const = lambda shape: pl.BlockSpec(shape, lambda i: (0,) * len(shape), pipeline_mode=pl.Buffered(1))
    return pl.pallas_call(
        _fourier_out_kernel,
        grid=(N_TILES,),
        in_specs=[
            tok(FOURIER_CH), const(wch.shape), const(wseq_p.shape), const(wseq_s.shape),
            tok(D_MODEL),
            pl.BlockSpec((1, N_MOD, D_MODEL), lambda i: (_cond_of_tile(i), 0, 0)),
            tok(SGU_CH),
            pl.BlockSpec((None, FOURIER_CH + SGU_CH, D_MODEL), lambda i: (j, 0, 0),
                         pipeline_mode=pl.Buffered(1)),
        ],
        out_specs=tok(D_MODEL),
        out_shape=jax.ShapeDtypeStruct((NTOK, D_MODEL), F32),
        scratch_shapes=[pltpu.VMEM((TM, FOURIER_CH), BF16), pltpu.VMEM(wch.shape, BF16),
                        pltpu.VMEM(wseq_p.shape, BF16), pltpu.VMEM(wseq_s.shape, BF16),
                        pltpu.VMEM((FOURIER_CH + SGU_CH, D_MODEL), BF16)],
        compiler_params=_cparams("arbitrary"),
        name="fourier_out",
    )(f, wch, wseq_p, wseq_s, x, mod_l, sgu, w_out)


def kernel(x_prompt, x_sample, cache_k, cache_v, c, c_ctx, w_mod, b_mod, norm_g, ffn_w_in, ffn_w_out, ev_w_in, ev_w_out, q_norm_g, k_norm_g, conv_w, conv_b, conv_norm_g, conv_norm_b, od_w_in, od_w_out, sgu_norm_g, sgu_norm_b, sgu_w, sgu_b):
    cond = jnp.concatenate([c_ctx[None, :], c_ctx[None, :], c,
                            jnp.zeros((N_COND - COND_LATENT0 - DEC_BATCH, D_MODEL), F32)], axis=0)
    mod = _adaln(cond, w_mod, b_mod).reshape(DEPTH, N_COND, N_MOD, D_MODEL)
    ck = cache_k.reshape(DEC_BATCH, -1, PAST_LEN, ATT_KV)
    cv = cache_v.reshape(DEC_BATCH, -1, PAST_LEN, ATT_KV)

    ffn = functools.partial(_ffn, mod=mod, norm_g=norm_g, ffn_w_in=ffn_w_in, ffn_w_out=ffn_w_out)
    x = ffn((x_prompt.reshape(NTOK_P, D_MODEL), x_sample.reshape(NTOK_S, D_MODEL)), phases=[(0, 0)])
    new_k, new_v = [], []
    for layer in range(DEPTH):
        mod_l = mod[layer]
        j = layer // 2
        if layer % 2 == 0:
            q, k, v, glu, kc, vc = _even_in(x, mod_l, norm_g, ev_w_in, q_norm_g, k_norm_g, layer, j)
            new_k.append(kc.reshape(BATCH, SEQ, N_KV_HEADS, HEAD_DIM))
            new_v.append(vc.reshape(BATCH, SEQ, N_KV_HEADS, HEAD_DIM))
            x = _even_mix(x, mod_l, q, k, v, ck, cv, glu, conv_w, conv_b, conv_norm_g, conv_norm_b,
                          ev_w_out, j)
        else:
            f, sgu = _odd_in(x, mod_l, norm_g, od_w_in, sgu_norm_g, sgu_norm_b, sgu_w, sgu_b, layer, j)
            x = _fourier_out(x, mod_l, f, sgu, od_w_out, j)
        if layer < DEPTH - 1:
            x = ffn((x,), phases=[(layer, 1), (layer + 1, 0)])

    last = [(DEPTH - 1, 1)]
    y_prompt = ffn((x,), phases=last, row0=0, n_rows=NTOK_P)
    y_sample = ffn((x,), phases=last, row0=NTOK_P, n_rows=NTOK_S)
    return (y_prompt.reshape(BATCH, SEQ, D_MODEL), y_sample.reshape(DEC_BATCH, DEC_SEQ, D_MODEL),
            jnp.stack(new_k, axis=1), jnp.stack(new_v, axis=1))
```

```python
import functools

import numpy as np
import jax
import jax.numpy as jnp
from jax import lax
from jax.experimental import pallas as pl
from jax.experimental.pallas import tpu as pltpu

D_MODEL = 1024
BATCH = 32
SEQ = 256
DEPTH = 4
DEC_BATCH = 4
DEC_SEQ = 1024
PAST_LEN = 512
GRID_W = 64
N_HEADS = 8
N_KV_HEADS = 2
HEAD_DIM = 64
GQA_GROUP = N_HEADS // N_KV_HEADS
AXIS_ROPE_DIM = HEAD_DIM // 2
ROPE_THETA = 10000.0
ATT_Q = N_HEADS * HEAD_DIM
ATT_KV = N_KV_HEADS * HEAD_DIM
CONV_CH = 512
CONV_K = 31
FOURIER_GROUPS = 4
FOURIER_GROUP_CH = 128
FOURIER_CH = FOURIER_GROUPS * FOURIER_GROUP_CH
SGU_GROUPS = 4
SGU_GROUP_CH = 128
SGU_CH = SGU_GROUPS * SGU_GROUP_CH
CHUNK = 128
FFN_DIM = 2816
N_MOD = 9
EVEN_IN = ATT_Q + 2 * ATT_KV + 2 * CONV_CH
ODD_IN = FOURIER_CH + 2 * SGU_CH
RMS_EPS = 1e-6
LN_EPS = 1e-5
LOG2_E = 1.4426950408889634

BF16 = jnp.bfloat16
F32 = jnp.float32

V7X_SUBLANES = 8
V7X_LANES = 128
NTOK_P = BATCH * SEQ
NTOK_S = DEC_BATCH * DEC_SEQ
NTOK = NTOK_P + NTOK_S
TM = DEC_SEQ
N_TILES = NTOK // TM
N_TILES_P = NTOK_P // TM
SEQ_PER_TILE = TM // SEQ
N_COND = 8
COND_LATENT0 = 2
FFN_TM = 2 * TM
FFN_PIECE = TM
FFN_CHUNK = 256
N_FFN_CHUNKS = FFN_DIM // FFN_CHUNK
EVEN_IN_ROWS = 1024
EVEN_IN_PIECE = 256
ODD_IN_PIECE = 1024
ADALN_TN = 2304
CONV_ROWS = 128
CONV_HALO = 16
ATT_TQ = SEQ
V7X_VMEM_LIMIT_BYTES = 56 * 1024 * 1024


def _cparams(*sem):
    return pltpu.CompilerParams(dimension_semantics=sem, vmem_limit_bytes=V7X_VMEM_LIMIT_BYTES)


def _cond_of_tile(i, rows=TM):
    return jnp.maximum((i * rows - NTOK_P) // DEC_SEQ + COND_LATENT0, 0)


def _dot(a, b):
    return jnp.dot(a, b, preferred_element_type=F32)


def _modulate(x, g, shift, scale):
    ms = jnp.mean(x * x, axis=-1, keepdims=True)
    return (x * lax.rsqrt(ms + RMS_EPS)) * (g * (1.0 + scale)) + shift


def _layer_norm(x, g, b):
    mu = jnp.mean(x, axis=-1, keepdims=True)
    xc = x - mu
    var = jnp.mean(xc * xc, axis=-1, keepdims=True)
    return xc * lax.rsqrt(var + LN_EPS) * g + b


def _adaln_kernel(c_ref, w_ref, b_ref, o_ref):
    c = c_ref[...]
    s = (c * jax.nn.sigmoid(c)).astype(BF16)
    o_ref[...] = _dot(s, w_ref[...].astype(BF16)) + b_ref[...]


def _adaln(cond, w_mod, b_mod):
    n_out = N_MOD * D_MODEL
    return pl.pallas_call(
        _adaln_kernel,
        grid=(DEPTH, n_out // ADALN_TN),
        in_specs=[
            pl.BlockSpec((N_COND, D_MODEL), lambda l, n: (0, 0)),
            pl.BlockSpec((None, D_MODEL, ADALN_TN), lambda l, n: (l, 0, n)),
            pl.BlockSpec((None, 1, ADALN_TN), lambda l, n: (l, 0, n)),
        ],
        out_specs=pl.BlockSpec((None, N_COND, ADALN_TN), lambda l, n: (l, 0, n)),
        out_shape=jax.ShapeDtypeStruct((DEPTH, N_COND, n_out), F32),
        compiler_params=_cparams("parallel", "parallel"),
        name="adaln",
    )(cond, w_mod, b_mod.reshape(DEPTH, 1, n_out))


def _ffn_kernel(subs, n_x, x_split, *refs):
    n_ph = len(subs)
    x_refs = refs[:n_x]
    mod_refs = refs[n_x:n_x + n_ph]
    g_refs = refs[n_x + n_ph:n_x + 2 * n_ph]
    wg_ref, wu_ref, wo_ref, o_ref, h_scr = refs[n_x + 2 * n_ph:]
    i = pl.program_id(0)
    j = pl.program_id(1)
    halves = [(r0 // TM, slice(r0, r0 + FFN_PIECE)) for r0 in range(0, FFN_TM, FFN_PIECE)]

    if n_x == 1:
        @pl.when(j == 0)
        def _():
            o_ref[...] = x_refs[0][...]
    else:
        @pl.when((j == 0) & (i < x_split))
        def _():
            o_ref[...] = x_refs[0][...]

        @pl.when((j == 0) & (i >= x_split))
        def _():
            o_ref[...] = x_refs[1][...]

    def chunk(h, rows, out_gate):
        gate = _dot(h, wg_ref[...].astype(BF16))
        up = _dot(h, wu_ref[...].astype(BF16))
        a = (gate * jax.nn.sigmoid(gate) * up).astype(BF16)
        o_ref[rows, :] += _dot(a, (wo_ref[...] * (0.5 * out_gate)).astype(BF16))

    for p, sub in enumerate(subs):
        @pl.when(j == p * N_FFN_CHUNKS)
        def _(p=p, sub=sub):
            for c, rows in halves:
                h = _modulate(o_ref[rows, :], g_refs[p][...], mod_refs[p][c, 3 * sub:3 * sub + 1, :],
                              mod_refs[p][c, 3 * sub + 1:3 * sub + 2, :]).astype(BF16)
                h_scr[rows, :] = h
                chunk(h, rows, mod_refs[p][c, 3 * sub + 2:3 * sub + 3, :])

    @pl.when(j % N_FFN_CHUNKS != 0)
    def _():
        for c, rows in halves:
            out_gate = mod_refs[0][c, 3 * subs[0] + 2:3 * subs[0] + 3, :]
            if n_ph == 2:
                out_gate = jnp.where(j < N_FFN_CHUNKS, out_gate,
                                     mod_refs[1][c, 3 * subs[1] + 2:3 * subs[1] + 3, :])
            chunk(h_scr[rows, :], rows, out_gate)


def _ffn(xs, mod, norm_g, ffn_w_in, ffn_w_out, phases, row0=0, n_rows=NTOK):
    per = FFN_TM // TM
    tile0 = row0 // FFN_TM
    n_ph = len(phases)
    layers = [p[0] for p in phases]
    whichs = [p[1] for p in phases]

    def by_phase(vals):
        if n_ph == 1:
            return lambda j: vals[0]
        return lambda j: vals[0] + (j // N_FFN_CHUNKS) * (vals[1] - vals[0])

    layer_of, which_of = by_phase(layers), by_phase(whichs)
    chunk_of = lambda j: j % N_FFN_CHUNKS
    if len(xs) == 1:
        x_specs = [pl.BlockSpec((FFN_TM, D_MODEL), lambda i, j: (tile0 + i, 0))]
        x_split = 0
    else:
        x_split = xs[0].shape[0] // FFN_TM
        x_specs = [
            pl.BlockSpec((FFN_TM, D_MODEL), lambda i, j: (jnp.minimum(i, x_split - 1), 0),
                         pipeline_mode=pl.Buffered(1)),
            pl.BlockSpec((FFN_TM, D_MODEL), lambda i, j: (jnp.maximum(i - x_split, 0), 0),
                         pipeline_mode=pl.Buffered(1)),
        ]
    mod_specs = [pl.BlockSpec((None, per, N_MOD, D_MODEL),
                              lambda i, j, l=l: (l, _cond_of_tile(tile0 + i, FFN_TM) // per, 0, 0))
                 for l in layers]
    g_specs = [pl.BlockSpec((None, None, 1, D_MODEL), lambda i, j, l=l, w=w: (l, 2 * w, 0, 0))
               for l, w in phases]
    return pl.pallas_call(
        functools.partial(_ffn_kernel, tuple(2 * w for w in whichs), len(xs), x_split),
        grid=(n_rows // FFN_TM, n_ph * N_FFN_CHUNKS),
        in_specs=x_specs + mod_specs + g_specs + [
            pl.BlockSpec((None, None, D_MODEL, FFN_CHUNK),
                         lambda i, j: (layer_of(j), which_of(j), 0, chunk_of(j))),
            pl.BlockSpec((None, None, D_MODEL, FFN_CHUNK),
                         lambda i, j: (layer_of(j), which_of(j), 0, N_FFN_CHUNKS + chunk_of(j))),
            pl.BlockSpec((None, None, FFN_CHUNK, D_MODEL),
                         lambda i, j: (layer_of(j), which_of(j), chunk_of(j), 0)),
        ],
        out_specs=pl.BlockSpec((FFN_TM, D_MODEL), lambda i, j: (i, 0)),
        out_shape=jax.ShapeDtypeStruct((n_rows, D_MODEL), F32),
        scratch_shapes=[pltpu.VMEM((FFN_TM, D_MODEL), BF16)],
        compiler_params=_cparams("parallel", "arbitrary"),
        name="ffn",
    )(*xs, *([mod] * n_ph), *([norm_g.reshape(DEPTH, 3, 1, D_MODEL)] * n_ph),
      ffn_w_in, ffn_w_in, ffn_w_out)


def _group_mean_sq(x, width):
    r = lax.broadcasted_iota(jnp.int32, (width, width), 0) // HEAD_DIM
    c = lax.broadcasted_iota(jnp.int32, (width, width), 1) // HEAD_DIM
    ones = (r == c).astype(BF16)
    return _dot((x * x).astype(BF16), ones) * (1.0 / HEAD_DIM)


def _head_norm_rope(x, gain, cos, sin_next, sin_prev):
    width = x.shape[-1]
    xn = x * lax.rsqrt(_group_mean_sq(x, width) + RMS_EPS) * gain
    nxt = pltpu.roll(xn, width - 1, 1)
    prv = pltpu.roll(xn, 1, 1)
    return xn * cos + nxt * sin_next + prv * sin_prev


def _even_in_kernel(x_ref, mod_ref, g_ref, w_ref, qg_ref, kg_ref,
                    cq_ref, snq_ref, spq_ref, ck_ref, snk_ref, spk_ref,
                    q_ref, k_ref, v_ref, glu_ref, kc_ref, vc_ref, w_scr):
    i = pl.program_id(0)

    @pl.when(i == 0)
    def _():
        w_scr[...] = w_ref[...].astype(BF16)

    a0 = ATT_Q + 2 * ATT_KV
    for r0 in range(0, EVEN_IN_ROWS, EVEN_IN_PIECE):
        rows = slice(r0, r0 + EVEN_IN_PIECE)
        h = _modulate(x_ref[rows, :], g_ref[...], mod_ref[0, 3:4, :], mod_ref[0, 4:5, :]).astype(BF16)

        def proj(lo, hi):
            return _dot(h, w_scr[:, lo:hi])

        q = _head_norm_rope(proj(0, ATT_Q), qg_ref[...], cq_ref[rows, :], snq_ref[rows, :], spq_ref[rows, :])
        q_ref[rows, :] = (q * (HEAD_DIM ** -0.5 * LOG2_E)).astype(BF16)
        k_ref[rows, :] = _head_norm_rope(proj(ATT_Q, ATT_Q + ATT_KV), kg_ref[...],
                                         ck_ref[rows, :], snk_ref[rows, :], spk_ref[rows, :])
        v_ref[rows, :] = proj(ATT_Q + ATT_KV, a0)
        glu_ref[rows, :] = proj(a0, a0 + CONV_CH) * jax.nn.sigmoid(proj(a0 + CONV_CH, a0 + 2 * CONV_CH))

    @pl.when(i < NTOK_P // EVEN_IN_ROWS)
    def _():
        kc_ref[...] = k_ref[...]
        vc_ref[...] = v_ref[...]


def _rope_tables():
    t = np.arange(DEC_SEQ)
    row = (t // GRID_W).astype(np.float64)
    col = (t % GRID_W).astype(np.float64)
    inv_freq = 1.0 / (ROPE_THETA ** (np.arange(0, AXIS_ROPE_DIM, 2, dtype=np.float64) / AXIS_ROPE_DIM))
    ang = np.concatenate([row[:, None] * inv_freq, col[:, None] * inv_freq], axis=-1)
    ang = np.repeat(ang, 2, axis=-1)
    even = (np.arange(HEAD_DIM) % 2 == 0)[None, :]
    cos = np.cos(ang)
    sin_next = np.where(even, -np.sin(ang), 0.0)
    sin_prev = np.where(even, 0.0, np.sin(ang))

    def both(tab):
        ident = np.ones_like(tab) if tab is cos else np.zeros_like(tab)
        return np.stack([np.tile(ident, (1, N_HEADS)), np.tile(tab, (1, N_HEADS))]).astype(np.float32)

    return both(cos), both(sin_next), both(sin_prev)


def _even_in(x, mod_l, norm_g, ev_w_in, q_norm_g, k_norm_g, layer, j):
    cos, sin_next, sin_prev = (jnp.asarray(t) for t in _rope_tables())
    qg = jnp.tile(q_norm_g[j], N_HEADS).reshape(1, ATT_Q)
    kg = jnp.tile(k_norm_g[j], N_KV_HEADS).reshape(1, ATT_KV)
    rows = EVEN_IN_ROWS
    per_seq = DEC_SEQ // rows
    first_latent = NTOK_P // rows
    tab_idx = lambda i: (jnp.minimum(i // first_latent, 1), i % per_seq, 0)
    tab_q = pl.BlockSpec((None, rows, ATT_Q), tab_idx)
    tab_k = pl.BlockSpec((None, rows, ATT_KV), tab_idx)
    tok = lambda n: pl.BlockSpec((rows, n), lambda i: (i, 0))
    cache = pl.BlockSpec((rows, ATT_KV), lambda i: (jnp.minimum(i, first_latent - 1), 0))
    return pl.pallas_call(
        _even_in_kernel,
        grid=(NTOK // rows,),
        in_specs=[
            tok(D_MODEL),
            pl.BlockSpec((1, N_MOD, D_MODEL), lambda i: (_cond_of_tile(i, rows), 0, 0)),
            pl.BlockSpec((None, None, 1, D_MODEL), lambda i: (layer, 1, 0, 0)),
            pl.BlockSpec((None, D_MODEL, EVEN_IN), lambda i: (j, 0, 0), pipeline_mode=pl.Buffered(1)),
            pl.BlockSpec((1, ATT_Q), lambda i: (0, 0)),
            pl.BlockSpec((1, ATT_KV), lambda i: (0, 0)),
            tab_q, tab_q, tab_q, tab_k, tab_k, tab_k,
        ],
        out_specs=[tok(ATT_Q), tok(ATT_KV), tok(ATT_KV), tok(CONV_CH), cache, cache],
        out_shape=[
            jax.ShapeDtypeStruct((NTOK, ATT_Q), BF16),
            jax.ShapeDtypeStruct((NTOK, ATT_KV), F32),
            jax.ShapeDtypeStruct((NTOK, ATT_KV), F32),
            jax.ShapeDtypeStruct((NTOK, CONV_CH), F32),
            jax.ShapeDtypeStruct((NTOK_P, ATT_KV), F32),
            jax.ShapeDtypeStruct((NTOK_P, ATT_KV), F32),
        ],
        scratch_shapes=[pltpu.VMEM((D_MODEL, EVEN_IN), BF16)],
        compiler_params=_cparams("arbitrary"),
        name="even_in",
    )(x, mod_l, norm_g.reshape(DEPTH, 3, 1, D_MODEL), ev_w_in, qg, kg,
      cos, sin_next, sin_prev, cos, sin_next, sin_prev)


ATT_WIDTH = GQA_GROUP * HEAD_DIM


def _expand_kv_head(x, kv):
    lane = lax.broadcasted_iota(jnp.int32, x.shape, 1)
    swapped = pltpu.roll(x, HEAD_DIM, 1)
    pair = jnp.where((lane // HEAD_DIM) == kv, x, swapped)
    return jnp.concatenate([pair, pair], axis=1)


def _attend(q_ref, q_rows, kbig, vbig, n_keys, attn_scr):
    lane_group = lax.broadcasted_iota(jnp.int32, (ATT_TQ, ATT_WIDTH), 1) // HEAD_DIM
    for kv in range(N_KV_HEADS):
        cols = slice(kv * ATT_WIDTH, (kv + 1) * ATT_WIDTH)
        q = q_ref[q_rows, cols]
        kb = kbig[kv, 0:n_keys, :]
        vb = vbig[kv, 0:n_keys, :]
        acc = jnp.zeros((ATT_TQ, ATT_WIDTH), F32)
        for g in range(GQA_GROUP):
            mine = lane_group == g
            qm = jnp.where(mine, q, jnp.zeros_like(q))
            s = lax.dot_general(qm, kb, (((1,), (1,)), ((), ())), preferred_element_type=F32)
            m = jnp.max(s, axis=-1, keepdims=True)
            p = jnp.exp2(s - m)
            denom = jnp.sum(p, axis=-1, keepdims=True)
            o = _dot(p.astype(BF16), vb)
            acc = jnp.where(mine, o / denom, acc)
        attn_scr[:, cols] = acc.astype(BF16)


CONV_PAD_SEQ = SEQ + 2 * CONV_HALO
CONV_WIN = ATT_TQ + 2 * CONV_HALO
CONV_SPAN = CONV_WIN - V7X_SUBLANES


def _conv_block(win0_scr, win_scr, w_ref, y_scr):
    off = CONV_HALO - CONV_K // 2
    for r0 in range(0, ATT_TQ, CONV_ROWS):
        for c0 in range(0, CONV_CH, V7X_LANES):
            cols = slice(c0, c0 + V7X_LANES)
            acc = jnp.zeros((CONV_ROWS, V7X_LANES), F32)
            for k in range(CONV_K):
                res = (off + k) % V7X_SUBLANES
                a = r0 + off + k - res
                if res == 0:
                    tap = win0_scr[a:a + CONV_ROWS, cols]
                else:
                    tap = win_scr[res - 1, a:a + CONV_ROWS, cols]
                acc = acc + tap * w_ref[k:k + 1, cols]
            y_scr[r0:r0 + CONV_ROWS, cols] = acc


def _even_mix_kernel(q_ref, k_ref, v_ref, ck_ref, cv_ref, glu_ref, w_ref, b_ref, g_ref, beta_ref,
                     res_ref, mod_ref, wo_ref, o_ref,
                     kbig, vbig, pad_scr, win0_scr, win_scr, y_scr, attn_scr, wo_scr):
    i = pl.program_id(0)
    t = pl.program_id(1)
    rows = pl.ds(pl.multiple_of(t * ATT_TQ, ATT_TQ), ATT_TQ)
    border = jnp.zeros((CONV_HALO, CONV_CH), F32)

    @pl.when((i == 0) & (t == 0))
    def _():
        wo_scr[...] = wo_ref[...].astype(BF16)

    def expand(k, v, n):
        for kv in range(N_KV_HEADS):
            kbig[kv, 0:n, :] = _expand_kv_head(k, kv).astype(BF16)
            vbig[kv, 0:n, :] = _expand_kv_head(v, kv).astype(BF16)

    def mix(n_keys, win_start):
        win0_scr[...] = pad_scr[pl.ds(pl.multiple_of(win_start, V7X_SUBLANES), CONV_WIN), :]
        for res in range(1, V7X_SUBLANES):
            win_scr[res - 1] = win0_scr[res:res + CONV_SPAN, :]
        _attend(q_ref, rows, kbig, vbig, n_keys, attn_scr)
        _conv_block(win0_scr, win_scr, w_ref, y_scr)
        y = _layer_norm(y_scr[...] + b_ref[...], g_ref[...], beta_ref[...])
        conv = (y * jax.nn.sigmoid(y)).astype(BF16)
        mixed = _dot(attn_scr[...], wo_scr[:ATT_Q, :]) + _dot(conv, wo_scr[ATT_Q:, :])
        o_ref[...] = res_ref[...] + mod_ref[0, 5:6, :] * mixed

    @pl.when(i < N_TILES_P)
    def _():
        @pl.when(t == 0)
        def _():
            for s in range(SEQ_PER_TILE):
                base = s * CONV_PAD_SEQ
                pad_scr[base:base + CONV_HALO, :] = border
                pad_scr[base + CONV_HALO:base + CONV_HALO + SEQ, :] = glu_ref[s * SEQ:(s + 1) * SEQ, :]
                pad_scr[base + CONV_HALO + SEQ:base + CONV_PAD_SEQ, :] = border

        expand(k_ref[rows, :], v_ref[rows, :], SEQ)
        mix(SEQ, t * CONV_PAD_SEQ)

    @pl.when(i >= N_TILES_P)
    def _():
        @pl.when(t == 0)
        def _():
            pad_scr[0:CONV_HALO, :] = border
            pad_scr[CONV_HALO:CONV_HALO + DEC_SEQ, :] = glu_ref[...]
            pad_scr[CONV_HALO + DEC_SEQ:2 * CONV_HALO + DEC_SEQ, :] = border
            expand(jnp.concatenate([ck_ref[...], k_ref[...]], axis=0),
                   jnp.concatenate([cv_ref[...], v_ref[...]], axis=0), PAST_LEN + DEC_SEQ)

        mix(PAST_LEN + DEC_SEQ, t * ATT_TQ)


def _even_mix(x, mod_l, q, k, v, ck, cv, glu, conv_w, conv_b, cn_g, cn_b, w_out, j):
    nq = TM // ATT_TQ
    tok = lambda n: pl.BlockSpec((TM, n), lambda i, t: (i, 0))
    blk = pl.BlockSpec((ATT_TQ, D_MODEL), lambda i, t: (i * nq + t, 0))
    ctx = pl.BlockSpec((None, None, PAST_LEN, ATT_KV), lambda i, t: (jnp.maximum(i - N_TILES_P, 0), j, 0, 0))
    vec = pl.BlockSpec((None, 1, CONV_CH), lambda i, t: (j, 0, 0))
    n_even = conv_b.shape[0]
    n_keys = PAST_LEN + DEC_SEQ
    return pl.pallas_call(
        _even_mix_kernel,
        grid=(N_TILES, nq),
        in_specs=[
            tok(ATT_Q), tok(ATT_KV), tok(ATT_KV), ctx, ctx, tok(CONV_CH),
            pl.BlockSpec((None, CONV_K, CONV_CH), lambda i, t: (j, 0, 0)),
            vec, vec, vec,
            blk,
            pl.BlockSpec((1, N_MOD, D_MODEL), lambda i, t: (_cond_of_tile(i), 0, 0)),
            pl.BlockSpec((None, ATT_Q + CONV_CH, D_MODEL), lambda i, t: (j, 0, 0),
                         pipeline_mode=pl.Buffered(1)),
        ],
        out_specs=blk,
        out_shape=jax.ShapeDtypeStruct((NTOK, D_MODEL), F32),
        scratch_shapes=[
            pltpu.VMEM((N_KV_HEADS, n_keys, ATT_WIDTH), BF16),
            pltpu.VMEM((N_KV_HEADS, n_keys, ATT_WIDTH), BF16),
            pltpu.VMEM((max(SEQ_PER_TILE * CONV_PAD_SEQ, DEC_SEQ + 2 * CONV_HALO), CONV_CH), F32),
            pltpu.VMEM((CONV_WIN, CONV_CH), F32),
            pltpu.VMEM((V7X_SUBLANES - 1, CONV_SPAN, CONV_CH), F32),
            pltpu.VMEM((ATT_TQ, CONV_CH), F32),
            pltpu.VMEM((ATT_TQ, ATT_Q), BF16),
            pltpu.VMEM((ATT_Q + CONV_CH, D_MODEL), BF16),
        ],
        compiler_params=_cparams("arbitrary", "arbitrary"),
        name="even_mix",
    )(q, k, v, ck, cv, glu, conv_w, conv_b.reshape(n_even, 1, CONV_CH),
      cn_g.reshape(n_even, 1, CONV_CH), cn_b.reshape(n_even, 1, CONV_CH), x, mod_l, w_out)


def _odd_in_kernel(x_ref, mod_ref, g_ref, w_ref, sg_ref, sb_ref, ws_ref, bs_ref, f_ref, sgu_ref, w_scr):
    @pl.when(pl.program_id(0) == 0)
    def _():
        w_scr[...] = w_ref[...].astype(BF16)

    ws = [ws_ref[g].astype(BF16) for g in range(SGU_GROUPS)]
    for r0 in range(0, TM, ODD_IN_PIECE):
        piece = slice(r0, r0 + ODD_IN_PIECE)
        h = _modulate(x_ref[piece, :], g_ref[...], mod_ref[0, 3:4, :], mod_ref[0, 4:5, :]).astype(BF16)

        def proj(lo, hi):
            return _dot(h, w_scr[:, lo:hi])

        f_ref[piece, :] = proj(0, FOURIER_CH).astype(BF16)
        u = proj(FOURIER_CH, FOURIER_CH + SGU_CH)
        v = _layer_norm(proj(FOURIER_CH + SGU_CH, ODD_IN), sg_ref[...], sb_ref[...]).astype(BF16)
        for g in range(SGU_GROUPS):
            bias = bs_ref[:, g:g + 1]
            cols = slice(g * SGU_GROUP_CH, (g + 1) * SGU_GROUP_CH)
            for n in range(ODD_IN_PIECE // CHUNK):
                rows = slice(n * CHUNK, (n + 1) * CHUNK)
                mixed = _dot(ws[g], v[rows, cols]) + bias
                sgu_ref[r0 + n * CHUNK:r0 + (n + 1) * CHUNK, cols] = (u[rows, cols] * mixed).astype(BF16)


def _odd_in(x, mod_l, norm_g, od_w_in, sgu_norm_g, sgu_norm_b, sgu_w, sgu_b, layer, j):
    tok = lambda n: pl.BlockSpec((TM, n), lambda i: (i, 0))
    return pl.pallas_call(
        _odd_in_kernel,
        grid=(N_TILES,),
        in_specs=[
            tok(D_MODEL),
            pl.BlockSpec((1, N_MOD, D_MODEL), lambda i: (_cond_of_tile(i), 0, 0)),
            pl.BlockSpec((None, None, 1, D_MODEL), lambda i: (layer, 1, 0, 0)),
            pl.BlockSpec((None, D_MODEL, ODD_IN), lambda i: (j, 0, 0), pipeline_mode=pl.Buffered(1)),
            pl.BlockSpec((1, SGU_CH), lambda i: (0, 0)),
            pl.BlockSpec((1, SGU_CH), lambda i: (0, 0)),
            pl.BlockSpec((None, SGU_GROUPS, CHUNK, CHUNK), lambda i: (j, 0, 0, 0)),
            pl.BlockSpec((CHUNK, SGU_GROUPS), lambda i: (0, 0)),
        ],
        out_specs=[tok(FOURIER_CH), tok(SGU_CH)],
        out_shape=[jax.ShapeDtypeStruct((NTOK, FOURIER_CH), BF16),
                   jax.ShapeDtypeStruct((NTOK, SGU_CH), BF16)],
        scratch_shapes=[pltpu.VMEM((D_MODEL, ODD_IN), BF16)],
        compiler_params=_cparams("arbitrary"),
        name="odd_in",
    )(x, mod_l, norm_g.reshape(DEPTH, 3, 1, D_MODEL), od_w_in,
      sgu_norm_g[j].reshape(1, SGU_CH), sgu_norm_b[j].reshape(1, SGU_CH), sgu_w, sgu_b[j].T)


def _fourier_out_kernel(f_ref, wch_ref, wseq_p_ref, wseq_s_ref, res_ref, mod_ref, sgu_ref, wo_ref,
                        o_ref, four_scr, wch_scr, wseq_p_scr, wseq_s_scr, wo_scr):
    i = pl.program_id(0)

    @pl.when(i == 0)
    def _():
        wch_scr[...] = wch_ref[...].astype(BF16)
        wseq_p_scr[...] = wseq_p_ref[...].astype(BF16)
        wseq_s_scr[...] = wseq_s_ref[...].astype(BF16)
        wo_scr[...] = wo_ref[...].astype(BF16)

    pq = _dot(f_ref[...], wch_scr[...]).astype(BF16)

    def seq_dft(rows, wseq, seq):
        stacked = jnp.concatenate([pq[rows, :FOURIER_CH], pq[rows, FOURIER_CH:]], axis=0)
        scale = 1.0 / np.sqrt(float(seq * FOURIER_GROUP_CH))
        four_scr[rows, :] = (_dot(wseq, stacked) * scale).astype(BF16)

    @pl.when(i < N_TILES_P)
    def _():
        for s in range(SEQ_PER_TILE):
            seq_dft(slice(s * SEQ, (s + 1) * SEQ), wseq_p_scr[...], SEQ)

    @pl.when(i >= N_TILES_P)
    def _():
        seq_dft(slice(0, DEC_SEQ), wseq_s_scr[...], DEC_SEQ)

    gate = mod_ref[0, 5:6, :]
    for s in range(SEQ_PER_TILE):
        rows = slice(s * SEQ, (s + 1) * SEQ)
        mixed = (_dot(four_scr[rows, :], wo_scr[:FOURIER_CH, :])
                 + _dot(sgu_ref[rows, :], wo_scr[FOURIER_CH:, :]))
        o_ref[rows, :] = res_ref[rows, :] + gate * mixed


def _dft_constants():
    def cos_sin(n):
        idx = np.arange(n)
        ang = 2.0 * np.pi * ((idx[:, None] * idx[None, :]) % n) / n
        return np.cos(ang), np.sin(ang)

    cc, sc = cos_sin(FOURIER_GROUP_CH)
    eye = np.eye(FOURIER_GROUPS)
    wch = np.concatenate([np.kron(eye, cc), np.kron(eye, sc)], axis=1)

    def wseq(seq):
        cl, sl = cos_sin(seq)
        return jnp.asarray(np.concatenate([cl, -sl], axis=1), dtype=F32)

    return jnp.asarray(wch, dtype=F32), wseq(SEQ), wseq(DEC_SEQ)


def _fourier_out(x, mod_l, f, sgu, w_out, j):
    wch, wseq_p, wseq_s = _dft_constants()
    tok = lambda n: pl.BlockSpec((TM, n), lambda i: (i, 0))
    const = lambda shape: pl.BlockSpec(shape, lambda i: (0,) * len(shape), pipeline_mode=pl.Buffered(1))
    return pl.pallas_call(
        _fourier_out_kernel,
        grid=(N_TILES,),
        in_specs=[
            tok(FOURIER_CH), const(wch.shape), const(wseq_p.shape), const(wseq_s.shape),
            tok(D_MODEL),
            pl.BlockSpec((1, N_MOD, D_MODEL), lambda i: (_cond_of_tile(i), 0, 0)),
            tok(SGU_CH),
            pl.BlockSpec((None, FOURIER_CH + SGU_CH, D_MODEL), lambda i: (j, 0, 0),
                         pipeline_mode=pl.Buffered(1)),
        ],
        out_specs=tok(D_MODEL),
        out_shape=jax.ShapeDtypeStruct((NTOK, D_MODEL), F32),
        scratch_shapes=[pltpu.VMEM((TM, FOURIER_CH), BF16), pltpu.VMEM(wch.shape, BF16),
                        pltpu.VMEM(wseq_p.shape, BF16), pltpu.VMEM(wseq_s.shape, BF16),
                        pltpu.VMEM((FOURIER_CH + SGU_CH, D_MODEL), BF16)],
        compiler_params=_cparams("arbitrary"),
        name="fourier_out",
    )(f, wch, wseq_p, wseq_s, x, mod_l, sgu, w_out)


def kernel(x_prompt, x_sample, cache_k, cache_v, c, c_ctx, w_mod, b_mod, norm_g, ffn_w_in, ffn_w_out, ev_w_in, ev_w_out, q_norm_g, k_norm_g, conv_w, conv_b, conv_norm_g, conv_norm_b, od_w_in, od_w_out, sgu_norm_g, sgu_norm_b, sgu_w, sgu_b):
    cond = jnp.concatenate([c_ctx[None, :], c_ctx[None, :], c,
                            jnp.zeros((N_COND - COND_LATENT0 - DEC_BATCH, D_MODEL), F32)], axis=0)
    mod = _adaln(cond, w_mod, b_mod).reshape(DEPTH, N_COND, N_MOD, D_MODEL)
    ck = cache_k.reshape(DEC_BATCH, -1, PAST_LEN, ATT_KV)
    cv = cache_v.reshape(DEC_BATCH, -1, PAST_LEN, ATT_KV)

    ffn = functools.partial(_ffn, mod=mod, norm_g=norm_g, ffn_w_in=ffn_w_in, ffn_w_out=ffn_w_out)
    x = ffn((x_prompt.reshape(NTOK_P, D_MODEL), x_sample.reshape(NTOK_S, D_MODEL)), phases=[(0, 0)])
    new_k, new_v = [], []
    for layer in range(DEPTH):
        mod_l = mod[layer]
        j = layer // 2
        if layer % 2 == 0:
            q, k, v, glu, kc, vc = _even_in(x, mod_l, norm_g, ev_w_in, q_norm_g, k_norm_g, layer, j)
            new_k.append(kc.reshape(BATCH, SEQ, N_KV_HEADS, HEAD_DIM))
            new_v.append(vc.reshape(BATCH, SEQ, N_KV_HEADS, HEAD_DIM))
            x = _even_mix(x, mod_l, q, k, v, ck, cv, glu, conv_w, conv_b, conv_norm_g, conv_norm_b,
                          ev_w_out, j)
        else:
            f, sgu = _odd_in(x, mod_l, norm_g, od_w_in, sgu_norm_g, sgu_norm_b, sgu_w, sgu_b, layer, j)
            x = _fourier_out(x, mod_l, f, sgu, od_w_out, j)
        if layer < DEPTH - 1:
            x = ffn((x,), phases=[(layer, 1), (layer + 1, 0)])

    last = [(DEPTH - 1, 1)]
    y_prompt = ffn((x,), phases=last, row0=0, n_rows=NTOK_P)
    y_sample = ffn((x,), phases=last, row0=NTOK_P, n_rows=NTOK_S)
    return (y_prompt.reshape(BATCH, SEQ, D_MODEL), y_sample.reshape(DEC_BATCH, DEC_SEQ, D_MODEL),
            jnp.stack(new_k, axis=1), jnp.stack(new_v, axis=1))
```

```python
import functools

import numpy as np
import jax
import jax.numpy as jnp
from jax import lax
from jax.experimental import pallas as pl
from jax.experimental.pallas import tpu as pltpu

D_MODEL = 1024
BATCH = 32
SEQ = 256
DEPTH = 4
DEC_BATCH = 4
DEC_SEQ = 1024
PAST_LEN = 512
GRID_W = 64
N_HEADS = 8
N_KV_HEADS = 2
HEAD_DIM = 64
GQA_GROUP = N_HEADS // N_KV_HEADS
AXIS_ROPE_DIM = HEAD_DIM // 2
ROPE_THETA = 10000.0
ATT_Q = N_HEADS * HEAD_DIM
ATT_KV = N_KV_HEADS * HEAD_DIM
CONV_CH = 512
CONV_K = 31
FOURIER_GROUPS = 4
FOURIER_GROUP_CH = 128
FOURIER_CH = FOURIER_GROUPS * FOURIER_GROUP_CH
SGU_GROUPS = 4
SGU_GROUP_CH = 128
SGU_CH = SGU_GROUPS * SGU_GROUP_CH
CHUNK = 128
FFN_DIM = 2816
N_MOD = 9
EVEN_IN = ATT_Q + 2 * ATT_KV + 2 * CONV_CH
ODD_IN = FOURIER_CH + 2 * SGU_CH
RMS_EPS = 1e-6
LN_EPS = 1e-5
LOG2_E = 1.4426950408889634

BF16 = jnp.bfloat16
F32 = jnp.float32

V7X_SUBLANES = 8
V7X_LANES = 128
NTOK_P = BATCH * SEQ
NTOK_S = DEC_BATCH * DEC_SEQ
NTOK = NTOK_P + NTOK_S
TM = DEC_SEQ
N_TILES = NTOK // TM
N_TILES_P = NTOK_P // TM
SEQ_PER_TILE = TM // SEQ
N_COND = 8
COND_LATENT0 = 2
FFN_TM = 2 * TM
FFN_PIECE = TM
FFN_CHUNK = 256
N_FFN_CHUNKS = FFN_DIM // FFN_CHUNK
EVEN_IN_ROWS = 1024
EVEN_IN_PIECE = 256
ODD_IN_PIECE = 1024
ADALN_TN = 2304
CONV_ROWS = 128
CONV_HALO = 16
ATT_TQ = SEQ
V7X_VMEM_LIMIT_BYTES = 56 * 1024 * 1024


def _cparams(*sem):
    return pltpu.CompilerParams(dimension_semantics=sem, vmem_limit_bytes=V7X_VMEM_LIMIT_BYTES)


def _cond_of_tile(i, rows=TM):
    return jnp.maximum((i * rows - NTOK_P) // DEC_SEQ + COND_LATENT0, 0)


def _dot(a, b):
    return jnp.dot(a, b, preferred_element_type=F32)


def _modulate(x, g, shift, scale):
    ms = jnp.mean(x * x, axis=-1, keepdims=True)
    return (x * lax.rsqrt(ms + RMS_EPS)) * (g * (1.0 + scale)) + shift


def _layer_norm(x, g, b):
    mu = jnp.mean(x, axis=-1, keepdims=True)
    xc = x - mu
    var = jnp.mean(xc * xc, axis=-1, keepdims=True)
    return xc * lax.rsqrt(var + LN_EPS) * g + b


def _adaln_kernel(c_ref, w_ref, b_ref, o_ref):
    c = c_ref[...]
    s = (c * jax.nn.sigmoid(c)).astype(BF16)
    o_ref[...] = _dot(s, w_ref[...].astype(BF16)) + b_ref[...]


def _adaln(cond, w_mod, b_mod):
    n_out = N_MOD * D_MODEL
    return pl.pallas_call(
        _adaln_kernel,
        grid=(DEPTH, n_out // ADALN_TN),
        in_specs=[
            pl.BlockSpec((N_COND, D_MODEL), lambda l, n: (0, 0)),
            pl.BlockSpec((None, D_MODEL, ADALN_TN), lambda l, n: (l, 0, n)),
            pl.BlockSpec((None, 1, ADALN_TN), lambda l, n: (l, 0, n)),
        ],
        out_specs=pl.BlockSpec((None, N_COND, ADALN_TN), lambda l, n: (l, 0, n)),
        out_shape=jax.ShapeDtypeStruct((DEPTH, N_COND, n_out), F32),
        compiler_params=_cparams("parallel", "parallel"),
        name="adaln",
    )(cond, w_mod, b_mod.reshape(DEPTH, 1, n_out))


def _ffn_kernel(subs, n_x, x_split, *refs):
    n_ph = len(subs)
    x_refs = refs[:n_x]
    mod_refs = refs[n_x:n_x + n_ph]
    g_refs = refs[n_x + n_ph:n_x + 2 * n_ph]
    wg_ref, wu_ref, wo_ref, o_ref, h_scr = refs[n_x + 2 * n_ph:]
    i = pl.program_id(0)
    j = pl.program_id(1)
    halves = [(r0 // TM, slice(r0, r0 + FFN_PIECE)) for r0 in range(0, FFN_TM, FFN_PIECE)]

    if n_x == 1:
        @pl.when(j == 0)
        def _():
            o_ref[...] = x_refs[0][...]
    else:
        @pl.when((j == 0) & (i < x_split))
        def _():
            o_ref[...] = x_refs[0][...]

        @pl.when((j == 0) & (i >= x_split))
        def _():
            o_ref[...] = x_refs[1][...]

    def chunk(h, rows, out_gate):
        gate = _dot(h, wg_ref[...].astype(BF16))
        up = _dot(h, wu_ref[...].astype(BF16))
        a = (gate * jax.nn.sigmoid(gate) * up).astype(BF16)
        o_ref[rows, :] += _dot(a, (wo_ref[...] * (0.5 * out_gate)).astype(BF16))

    for p, sub in enumerate(subs):
        @pl.when(j == p * N_FFN_CHUNKS)
        def _(p=p, sub=sub):
            for c, rows in halves:
                h = _modulate(o_ref[rows, :], g_refs[p][...], mod_refs[p][c, 3 * sub:3 * sub + 1, :],
                              mod_refs[p][c, 3 * sub + 1:3 * sub + 2, :]).astype(BF16)
                h_scr[rows, :] = h
                chunk(h, rows, mod_refs[p][c, 3 * sub + 2:3 * sub + 3, :])

    @pl.when(j % N_FFN_CHUNKS != 0)
    def _():
        for c, rows in halves:
            out_gate = mod_refs[0][c, 3 * subs[0] + 2:3 * subs[0] + 3, :]
            if n_ph == 2:
                out_gate = jnp.where(j < N_FFN_CHUNKS, out_gate,
                                     mod_refs[1][c, 3 * subs[1] + 2:3 * subs[1] + 3, :])
            chunk(h_scr[rows, :], rows, out_gate)


def _ffn(xs, mod, norm_g, ffn_w_in, ffn_w_out, phases, row0=0, n_rows=NTOK):
    per = FFN_TM // TM
    tile0 = row0 // FFN_TM
    n_ph = len(phases)
    layers = [p[0] for p in phases]
    whichs = [p[1] for p in phases]

    def by_phase(vals):
        if n_ph == 1:
            return lambda j: vals[0]
        return lambda j: vals[0] + (j // N_FFN_CHUNKS) * (vals[1] - vals[0])

    layer_of, which_of = by_phase(layers), by_phase(whichs)
    chunk_of = lambda j: j % N_FFN_CHUNKS
    if len(xs) == 1:
        x_specs = [pl.BlockSpec((FFN_TM, D_MODEL), lambda i, j: (tile0 + i, 0))]
        x_split = 0
    else:
        x_split = xs[0].shape[0] // FFN_TM
        x_specs = [
            pl.BlockSpec((FFN_TM, D_MODEL), lambda i, j: (jnp.minimum(i, x_split - 1), 0),
                         pipeline_mode=pl.Buffered(1)),
            pl.BlockSpec((FFN_TM, D_MODEL), lambda i, j: (jnp.maximum(i - x_split, 0), 0),
                         pipeline_mode=pl.Buffered(1)),
        ]
    mod_specs = [pl.BlockSpec((None, per, N_MOD, D_MODEL),
                              lambda i, j, l=l: (l, _cond_of_tile(tile0 + i, FFN_TM) // per, 0, 0))
                 for l in layers]
    g_specs = [pl.BlockSpec((None, None, 1, D_MODEL), lambda i, j, l=l, w=w: (l, 2 * w, 0, 0))
               for l, w in phases]
    return pl.pallas_call(
        functools.partial(_ffn_kernel, tuple(2 * w for w in whichs), len(xs), x_split),
        grid=(n_rows // FFN_TM, n_ph * N_FFN_CHUNKS),
        in_specs=x_specs + mod_specs + g_specs + [
            pl.BlockSpec((None, None, D_MODEL, FFN_CHUNK),
                         lambda i, j: (layer_of(j), which_of(j), 0, chunk_of(j))),
            pl.BlockSpec((None, None, D_MODEL, FFN_CHUNK),
                         lambda i, j: (layer_of(j), which_of(j), 0, N_FFN_CHUNKS + chunk_of(j))),
            pl.BlockSpec((None, None, FFN_CHUNK, D_MODEL),
                         lambda i, j: (layer_of(j), which_of(j), chunk_of(j), 0)),
        ],
        out_specs=pl.BlockSpec((FFN_TM, D_MODEL), lambda i, j: (i, 0)),
        out_shape=jax.ShapeDtypeStruct((n_rows, D_MODEL), F32),
        scratch_shapes=[pltpu.VMEM((FFN_TM, D_MODEL), BF16)],
        compiler_params=_cparams("parallel", "arbitrary"),
        name="ffn",
    )(*xs, *([mod] * n_ph), *([norm_g.reshape(DEPTH, 3, 1, D_MODEL)] * n_ph),
      ffn_w_in, ffn_w_in, ffn_w_out)


def _group_mean_sq(x, width):
    r = lax.broadcasted_iota(jnp.int32, (width, width), 0) // HEAD_DIM
    c = lax.broadcasted_iota(jnp.int32, (width, width), 1) // HEAD_DIM
    ones = (r == c).astype(BF16)
    return _dot((x * x).astype(BF16), ones) * (1.0 / HEAD_DIM)


def _head_norm_rope(x, gain, cos, sin_next, sin_prev):
    width = x.shape[-1]
    xn = x * lax.rsqrt(_group_mean_sq(x, width) + RMS_EPS) * gain
    nxt = pltpu.roll(xn, width - 1, 1)
    prv = pltpu.roll(xn, 1, 1)
    return xn * cos + nxt * sin_next + prv * sin_prev


CONV_WIN = EVEN_IN_PIECE + 2 * CONV_HALO
CONV_SPAN = CONV_WIN - V7X_SUBLANES


def _conv_piece(pad_scr, start, win_scr, w_ref, y_scr):
    off = CONV_HALO - CONV_K // 2
    for r in range(1, V7X_SUBLANES):
        win_scr[r - 1] = pad_scr[start + r:start + r + CONV_SPAN, :]
    for r0 in range(0, EVEN_IN_PIECE, CONV_ROWS):
        for c0 in range(0, CONV_CH, V7X_LANES):
            cols = slice(c0, c0 + V7X_LANES)
            acc = jnp.zeros((CONV_ROWS, V7X_LANES), F32)
            for k in range(CONV_K):
                res = (off + k) % V7X_SUBLANES
                a = r0 + off + k - res
                if res == 0:
                    tap = pad_scr[start + a:start + a + CONV_ROWS, cols]
                else:
                    tap = win_scr[res - 1, a:a + CONV_ROWS, cols]
                acc = acc + tap * w_ref[k:k + 1, cols]
            y_scr[r0:r0 + CONV_ROWS, cols] = acc


def _even_in_kernel(x_ref, mod_ref, g_ref, w_ref, qg_ref, kg_ref,
                    cq_ref, snq_ref, spq_ref, ck_ref, snk_ref, spk_ref,
                    cw_ref, cb_ref, cg_ref, cbeta_ref,
                    q_ref, k_ref, v_ref, conv_ref, kc_ref, vc_ref, w_scr, pad_scr, win_scr, y_scr):
    i = pl.program_id(0)
    n_context_tiles = NTOK_P // EVEN_IN_ROWS
    pieces = EVEN_IN_ROWS // EVEN_IN_PIECE
    a0 = ATT_Q + 2 * ATT_KV
    border = jnp.zeros((CONV_HALO, CONV_CH), F32)

    @pl.when(i == 0)
    def _():
        w_scr[...] = w_ref[...].astype(BF16)

    def project(p, glu_row0):
        rows = slice(p * EVEN_IN_PIECE, (p + 1) * EVEN_IN_PIECE)
        h = _modulate(x_ref[rows, :], g_ref[...], mod_ref[0, 3:4, :], mod_ref[0, 4:5, :]).astype(BF16)

        def proj(lo, hi):
            return _dot(h, w_scr[:, lo:hi])

        q = _head_norm_rope(proj(0, ATT_Q), qg_ref[...], cq_ref[rows, :], snq_ref[rows, :], spq_ref[rows, :])
        q_ref[rows, :] = (q * (HEAD_DIM ** -0.5 * LOG2_E)).astype(BF16)
        k_ref[rows, :] = _head_norm_rope(proj(ATT_Q, ATT_Q + ATT_KV), kg_ref[...],
                                         ck_ref[rows, :], snk_ref[rows, :], spk_ref[rows, :])
        v_ref[rows, :] = proj(ATT_Q + ATT_KV, a0)
        pad_scr[glu_row0:glu_row0 + EVEN_IN_PIECE, :] = (
            proj(a0, a0 + CONV_CH) * jax.nn.sigmoid(proj(a0 + CONV_CH, a0 + 2 * CONV_CH)))

    def conv(p, win_start):
        _conv_piece(pad_scr, win_start, win_scr, cw_ref, y_scr)
        y = _layer_norm(y_scr[...] + cb_ref[...], cg_ref[...], cbeta_ref[...])
        conv_ref[p * EVEN_IN_PIECE:(p + 1) * EVEN_IN_PIECE, :] = (y * jax.nn.sigmoid(y)).astype(BF16)

    @pl.when(i < n_context_tiles)
    def _():
        for s in range(pieces):
            base = s * CONV_WIN
            pad_scr[base:base + CONV_HALO, :] = border
            pad_scr[base + CONV_HALO + EVEN_IN_PIECE:base + CONV_WIN, :] = border
            project(s, base + CONV_HALO)
            conv(s, base)

    @pl.when(i >= n_context_tiles)
    def _():
        pad_scr[0:CONV_HALO, :] = border
        pad_scr[CONV_HALO + EVEN_IN_ROWS:2 * CONV_HALO + EVEN_IN_ROWS, :] = border
        for p in range(pieces):
            project(p, CONV_HALO + p * EVEN_IN_PIECE)
            if p > 0:
                conv(p - 1, (p - 1) * EVEN_IN_PIECE)
        conv(pieces - 1, (pieces - 1) * EVEN_IN_PIECE)

    @pl.when(i < n_context_tiles)
    def _():
        kc_ref[...] = k_ref[...]
        vc_ref[...] = v_ref[...]


def _rope_tables():
    t = np.arange(DEC_SEQ)
    row = (t // GRID_W).astype(np.float64)
    col = (t % GRID_W).astype(np.float64)
    inv_freq = 1.0 / (ROPE_THETA ** (np.arange(0, AXIS_ROPE_DIM, 2, dtype=np.float64) / AXIS_ROPE_DIM))
    ang = np.concatenate([row[:, None] * inv_freq, col[:, None] * inv_freq], axis=-1)
    ang = np.repeat(ang, 2, axis=-1)
    even = (np.arange(HEAD_DIM) % 2 == 0)[None, :]
    cos = np.cos(ang)
    sin_next = np.where(even, -np.sin(ang), 0.0)
    sin_prev = np.where(even, 0.0, np.sin(ang))

    def both(tab):
        ident = np.ones_like(tab) if tab is cos else np.zeros_like(tab)
        return np.stack([np.tile(ident, (1, N_HEADS)), np.tile(tab, (1, N_HEADS))]).astype(np.float32)

    return both(cos), both(sin_next), both(sin_prev)


def _even_in(x, mod_l, norm_g, ev_w_in, q_norm_g, k_norm_g, conv_w, conv_b, cn_g, cn_b, layer, j):
    assert EVEN_IN_PIECE == SEQ and EVEN_IN_ROWS == DEC_SEQ
    n_even = conv_b.shape[0]
    vec = pl.BlockSpec((None, 1, CONV_CH), lambda i: (j, 0, 0))
    cos, sin_next, sin_prev = (jnp.asarray(t) for t in _rope_tables())
    qg = jnp.tile(q_norm_g[j], N_HEADS).reshape(1, ATT_Q)
    kg = jnp.tile(k_norm_g[j], N_KV_HEADS).reshape(1, ATT_KV)
    rows = EVEN_IN_ROWS
    per_seq = DEC_SEQ // rows
    first_latent = NTOK_P // rows
    tab_idx = lambda i: (jnp.minimum(i // first_latent, 1), i % per_seq, 0)
    tab_q = pl.BlockSpec((None, rows, ATT_Q), tab_idx)
    tab_k = pl.BlockSpec((None, rows, ATT_KV), tab_idx)
    tok = lambda n: pl.BlockSpec((rows, n), lambda i: (i, 0))
    cache = pl.BlockSpec((rows, ATT_KV), lambda i: (jnp.minimum(i, first_latent - 1), 0))
    return pl.pallas_call(
        _even_in_kernel,
        grid=(NTOK // rows,),
        in_specs=[
            tok(D_MODEL),
            pl.BlockSpec((1, N_MOD, D_MODEL), lambda i: (_cond_of_tile(i, rows), 0, 0)),
            pl.BlockSpec((None, None, 1, D_MODEL), lambda i: (layer, 1, 0, 0)),
            pl.BlockSpec((None, D_MODEL, EVEN_IN), lambda i: (j, 0, 0), pipeline_mode=pl.Buffered(1)),
            pl.BlockSpec((1, ATT_Q), lambda i: (0, 0)),
            pl.BlockSpec((1, ATT_KV), lambda i: (0, 0)),
            tab_q, tab_q, tab_q, tab_k, tab_k, tab_k,
            pl.BlockSpec((None, CONV_K, CONV_CH), lambda i: (j, 0, 0)),
            vec, vec, vec,
        ],
        out_specs=[tok(ATT_Q), tok(ATT_KV), tok(ATT_KV), tok(CONV_CH), cache, cache],
        out_shape=[
            jax.ShapeDtypeStruct((NTOK, ATT_Q), BF16),
            jax.ShapeDtypeStruct((NTOK, ATT_KV), F32),
            jax.ShapeDtypeStruct((NTOK, ATT_KV), F32),
            jax.ShapeDtypeStruct((NTOK, CONV_CH), BF16),
            jax.ShapeDtypeStruct((NTOK_P, ATT_KV), F32),
            jax.ShapeDtypeStruct((NTOK_P, ATT_KV), F32),
        ],
        scratch_shapes=[
            pltpu.VMEM((D_MODEL, EVEN_IN), BF16),
            pltpu.VMEM((max((EVEN_IN_ROWS // EVEN_IN_PIECE) * CONV_WIN, EVEN_IN_ROWS + 2 * CONV_HALO),
                        CONV_CH), F32),
            pltpu.VMEM((V7X_SUBLANES - 1, CONV_SPAN, CONV_CH), F32),
            pltpu.VMEM((EVEN_IN_PIECE, CONV_CH), F32),
        ],
        compiler_params=_cparams("arbitrary"),
        name="even_in",
    )(x, mod_l, norm_g.reshape(DEPTH, 3, 1, D_MODEL), ev_w_in, qg, kg,
      cos, sin_next, sin_prev, cos, sin_next, sin_prev,
      conv_w, conv_b.reshape(n_even, 1, CONV_CH), cn_g.reshape(n_even, 1, CONV_CH),
      cn_b.reshape(n_even, 1, CONV_CH))


ATT_WIDTH = GQA_GROUP * HEAD_DIM


def _expand_kv_head(x, kv):
    lane = lax.broadcasted_iota(jnp.int32, x.shape, 1)
    swapped = pltpu.roll(x, HEAD_DIM, 1)
    pair = jnp.where((lane // HEAD_DIM) == kv, x, swapped)
    return jnp.concatenate([pair, pair], axis=1)


def _attend(q_ref, q_rows, kbig, vbig, n_keys, attn_scr):
    lane_group = lax.broadcasted_iota(jnp.int32, (ATT_TQ, ATT_WIDTH), 1) // HEAD_DIM
    for kv in range(N_KV_HEADS):
        cols = slice(kv * ATT_WIDTH, (kv + 1) * ATT_WIDTH)
        q = q_ref[q_rows, cols]
        kb = kbig[kv, 0:n_keys, :]
        vb = vbig[kv, 0:n_keys, :]
        acc = jnp.zeros((ATT_TQ, ATT_WIDTH), F32)
        for g in range(GQA_GROUP):
            mine = lane_group == g
            qm = jnp.where(mine, q, jnp.zeros_like(q))
            s = lax.dot_general(qm, kb, (((1,), (1,)), ((), ())), preferred_element_type=F32)
            m = jnp.max(s, axis=-1, keepdims=True)
            p = jnp.exp2(s - m)
            denom = jnp.sum(p, axis=-1, keepdims=True)
            o = _dot(p.astype(BF16), vb)
            acc = jnp.where(mine, o / denom, acc)
        attn_scr[:, cols] = acc.astype(BF16)


def _even_mix_kernel(q_ref, k_ref, v_ref, ck_ref, cv_ref, conv_ref, res_ref, mod_ref, wo_ref, o_ref,
                     kbig, vbig, attn_scr, wo_scr):
    i = pl.program_id(0)
    t = pl.program_id(1)
    rows = pl.ds(pl.multiple_of(t * ATT_TQ, ATT_TQ), ATT_TQ)

    @pl.when((i == 0) & (t == 0))
    def _():
        wo_scr[...] = wo_ref[...].astype(BF16)

    def expand(k, v, n):
        for kv in range(N_KV_HEADS):
            kbig[kv, 0:n, :] = _expand_kv_head(k, kv).astype(BF16)
            vbig[kv, 0:n, :] = _expand_kv_head(v, kv).astype(BF16)

    def mix(n_keys):
        _attend(q_ref, rows, kbig, vbig, n_keys, attn_scr)
        mixed = _dot(attn_scr[...], wo_scr[:ATT_Q, :]) + _dot(conv_ref[...], wo_scr[ATT_Q:, :])
        o_ref[...] = res_ref[...] + mod_ref[0, 5:6, :] * mixed

    @pl.when(i < N_TILES_P)
    def _():
        expand(k_ref[rows, :], v_ref[rows, :], SEQ)
        mix(SEQ)

    @pl.when(i >= N_TILES_P)
    def _():
        @pl.when(t == 0)
        def _():
            expand(jnp.concatenate([ck_ref[...], k_ref[...]], axis=0),
                   jnp.concatenate([cv_ref[...], v_ref[...]], axis=0), PAST_LEN + DEC_SEQ)

        mix(PAST_LEN + DEC_SEQ)


def _even_mix(x, mod_l, q, k, v, ck, cv, conv, w_out, j):
    nq = TM // ATT_TQ
    tok = lambda n: pl.BlockSpec((TM, n), lambda i, t: (i, 0))
    blk = lambda n: pl.BlockSpec((ATT_TQ, n), lambda i, t: (i * nq + t, 0))
    ctx = pl.BlockSpec((None, None, PAST_LEN, ATT_KV), lambda i, t: (jnp.maximum(i - N_TILES_P, 0), j, 0, 0))
    n_keys = PAST_LEN + DEC_SEQ
    return pl.pallas_call(
        _even_mix_kernel,
        grid=(N_TILES, nq),
        in_specs=[
            tok(ATT_Q), tok(ATT_KV), tok(ATT_KV), ctx, ctx, blk(CONV_CH), blk(D_MODEL),
            pl.BlockSpec((1, N_MOD, D_MODEL), lambda i, t: (_cond_of_tile(i), 0, 0)),
            pl.BlockSpec((None, ATT_Q + CONV_CH, D_MODEL), lambda i, t: (j, 0, 0),
                         pipeline_mode=pl.Buffered(1)),
        ],
        out_specs=blk(D_MODEL),
        out_shape=jax.ShapeDtypeStruct((NTOK, D_MODEL), F32),
        scratch_shapes=[
            pltpu.VMEM((N_KV_HEADS, n_keys, ATT_WIDTH), BF16),
            pltpu.VMEM((N_KV_HEADS, n_keys, ATT_WIDTH), BF16),
            pltpu.VMEM((ATT_TQ, ATT_Q), BF16),
            pltpu.VMEM((ATT_Q + CONV_CH, D_MODEL), BF16),
        ],
        compiler_params=_cparams("arbitrary", "arbitrary"),
        name="even_mix",
    )(q, k, v, ck, cv, conv, x, mod_l, w_out)


def _odd_in_kernel(x_ref, mod_ref, g_ref, w_ref, sg_ref, sb_ref, ws_ref, bs_ref, f_ref, sgu_ref, w_scr):
    @pl.when(pl.program_id(0) == 0)
    def _():
        w_scr[...] = w_ref[...].astype(BF16)

    ws = [ws_ref[g].astype(BF16) for g in range(SGU_GROUPS)]
    for r0 in range(0, TM, ODD_IN_PIECE):
        piece = slice(r0, r0 + ODD_IN_PIECE)
        h = _modulate(x_ref[piece, :], g_ref[...], mod_ref[0, 3:4, :], mod_ref[0, 4:5, :]).astype(BF16)

        def proj(lo, hi):
            return _dot(h, w_scr[:, lo:hi])

        f_ref[piece, :] = proj(0, FOURIER_CH).astype(BF16)
        u = proj(FOURIER_CH, FOURIER_CH + SGU_CH)
        v = _layer_norm(proj(FOURIER_CH + SGU_CH, ODD_IN), sg_ref[...], sb_ref[...]).astype(BF16)
        for g in range(SGU_GROUPS):
            bias = bs_ref[:, g:g + 1]
            cols = slice(g * SGU_GROUP_CH, (g + 1) * SGU_GROUP_CH)
            for n in range(ODD_IN_PIECE // CHUNK):
                rows = slice(n * CHUNK, (n + 1) * CHUNK)
                mixed = _dot(ws[g], v[rows, cols]) + bias
                sgu_ref[r0 + n * CHUNK:r0 + (n + 1) * CHUNK, cols] = (u[rows, cols] * mixed).astype(BF16)


def _odd_in(x, mod_l, norm_g, od_w_in, sgu_norm_g, sgu_norm_b, sgu_w, sgu_b, layer, j):
    tok = lambda n: pl.BlockSpec((TM, n), lambda i: (i, 0))
    return pl.pallas_call(
        _odd_in_kernel,
        grid=(N_TILES,),
        in_specs=[
            tok(D_MODEL),
            pl.BlockSpec((1, N_MOD, D_MODEL), lambda i: (_cond_of_tile(i), 0, 0)),
            pl.BlockSpec((None, None, 1, D_MODEL), lambda i: (layer, 1, 0, 0)),
            pl.BlockSpec((None, D_MODEL, ODD_IN), lambda i: (j, 0, 0), pipeline_mode=pl.Buffered(1)),
            pl.BlockSpec((1, SGU_CH), lambda i: (0, 0)),
            pl.BlockSpec((1, SGU_CH), lambda i: (0, 0)),
            pl.BlockSpec((None, SGU_GROUPS, CHUNK, CHUNK), lambda i: (j, 0, 0, 0)),
            pl.BlockSpec((CHUNK, SGU_GROUPS), lambda i: (0, 0)),
        ],
        out_specs=[tok(FOURIER_CH), tok(SGU_CH)],
        out_shape=[jax.ShapeDtypeStruct((NTOK, FOURIER_CH), BF16),
                   jax.ShapeDtypeStruct((NTOK, SGU_CH), BF16)],
        scratch_shapes=[pltpu.VMEM((D_MODEL, ODD_IN), BF16)],
        compiler_params=_cparams("arbitrary"),
        name="odd_in",
    )(x, mod_l, norm_g.reshape(DEPTH, 3, 1, D_MODEL), od_w_in,
      sgu_norm_g[j].reshape(1, SGU_CH), sgu_norm_b[j].reshape(1, SGU_CH), sgu_w, sgu_b[j].T)


def _fourier_out_kernel(f_ref, wch_ref, wseq_p_ref, wseq_s_ref, res_ref, mod_ref, sgu_ref, wo_ref,
                        o_ref, four_scr, wch_scr, wseq_p_scr, wseq_s_scr, wo_scr):
    i = pl.program_id(0)

    @pl.when(i == 0)
    def _():
        wch_scr[...] = wch_ref[...].astype(BF16)
        wseq_p_scr[...] = wseq_p_ref[...].astype(BF16)
        wseq_s_scr[...] = wseq_s_ref[...].astype(BF16)
        wo_scr[...] = wo_ref[...].astype(BF16)

    pq = _dot(f_ref[...], wch_scr[...]).astype(BF16)

    def seq_dft(rows, wseq, seq):
        stacked = jnp.concatenate([pq[rows, :FOURIER_CH], pq[rows, FOURIER_CH:]], axis=0)
        scale = 1.0 / np.sqrt(float(seq * FOURIER_GROUP_CH))
        four_scr[rows, :] = (_dot(wseq, stacked) * scale).astype(BF16)

    @pl.when(i < N_TILES_P)
    def _():
        for s in range(SEQ_PER_TILE):
            seq_dft(slice(s * SEQ, (s + 1) * SEQ), wseq_p_scr[...], SEQ)

    @pl.when(i >= N_TILES_P)
    def _():
        seq_dft(slice(0, DEC_SEQ), wseq_s_scr[...], DEC_SEQ)

    gate = mod_ref[0, 5:6, :]
    for s in range(SEQ_PER_TILE):
        rows = slice(s * SEQ, (s + 1) * SEQ)
        mixed = (_dot(four_scr[rows, :], wo_scr[:FOURIER_CH, :])
                 + _dot(sgu_ref[rows, :], wo_scr[FOURIER_CH:, :]))
        o_ref[rows, :] = res_ref[rows, :] + gate * mixed


def _dft_constants():
    def cos_sin(n):
        idx = np.arange(n)
        ang = 2.0 * np.pi * ((idx[:, None] * idx[None, :]) % n) / n
        return np.cos(ang), np.sin(ang)

    cc, sc = cos_sin(FOURIER_GROUP_CH)
    eye = np.eye(FOURIER_GROUPS)
    wch = np.concatenate([np.kron(eye, cc), np.kron(eye, sc)], axis=1)

    def wseq(seq):
        cl, sl = cos_sin(seq)
        return jnp.asarray(np.concatenate([cl, -sl], axis=1), dtype=F32)

    return jnp.asarray(wch, dtype=F32), wseq(SEQ), wseq(DEC_SEQ)


def _fourier_out(x, mod_l, f, sgu, w_out, j):
    wch, wseq_p, wseq_s = _dft_constants()
    tok = lambda n: pl.BlockSpec((TM, n), lambda i: (i, 0))
    const = lambda shape: pl.BlockSpec(shape, lambda i: (0,) * len(shape), pipeline_mode=pl.Buffered(1))
    return pl.pallas_call(
        _fourier_out_kernel,
        grid=(N_TILES,),
        in_specs=[
            tok(FOURIER_CH), const(wch.shape), const(wseq_p.shape), const(wseq_s.shape),
            tok(D_MODEL),
            pl.BlockSpec((1, N_MOD, D_MODEL), lambda i: (_cond_of_tile(i), 0, 0)),
            tok(SGU_CH),
            pl.BlockSpec((None, FOURIER_CH + SGU_CH, D_MODEL), lambda i: (j, 0, 0),
                         pipeline_mode=pl.Buffered(1)),
        ],
        out_specs=tok(D_MODEL),
        out_shape=jax.ShapeDtypeStruct((NTOK, D_MODEL), F32),
        scratch_shapes=[pltpu.VMEM((TM, FOURIER_CH), BF16), pltpu.VMEM(wch.shape, BF16),
                        pltpu.VMEM(wseq_p.shape, BF16), pltpu.VMEM(wseq_s.shape, BF16),
                        pltpu.VMEM((FOURIER_CH + SGU_CH, D_MODEL), BF16)],
        compiler_params=_cparams("arbitrary"),
        name="fourier_out",
    )(f, wch, wseq_p, wseq_s, x, mod_l, sgu, w_out)


def kernel(x_prompt, x_sample, cache_k, cache_v, c, c_ctx, w_mod, b_mod, norm_g, ffn_w_in, ffn_w_out, ev_w_in, ev_w_out, q_norm_g, k_norm_g, conv_w, conv_b, conv_norm_g, conv_norm_b, od_w_in, od_w_out, sgu_norm_g, sgu_norm_b, sgu_w, sgu_b):
    cond = jnp.concatenate([c_ctx[None, :], c_ctx[None, :], c,
                            jnp.zeros((N_COND - COND_LATENT0 - DEC_BATCH, D_MODEL), F32)], axis=0)
    mod = _adaln(cond, w_mod, b_mod).reshape(DEPTH, N_COND, N_MOD, D_MODEL)
    ck = cache_k.reshape(DEC_BATCH, -1, PAST_LEN, ATT_KV)
    cv = cache_v.reshape(DEC_BATCH, -1, PAST_LEN, ATT_KV)

    ffn = functools.partial(_ffn, mod=mod, norm_g=norm_g, ffn_w_in=ffn_w_in, ffn_w_out=ffn_w_out)
    x = ffn((x_prompt.reshape(NTOK_P, D_MODEL), x_sample.reshape(NTOK_S, D_MODEL)), phases=[(0, 0)])
    new_k, new_v = [], []
    for layer in range(DEPTH):
        mod_l = mod[layer]
        j = layer // 2
        if layer % 2 == 0:
            q, k, v, conv, kc, vc = _even_in(x, mod_l, norm_g, ev_w_in, q_norm_g, k_norm_g,
                                             conv_w, conv_b, conv_norm_g, conv_norm_b, layer, j)
            new_k.append(kc.reshape(BATCH, SEQ, N_KV_HEADS, HEAD_DIM))
            new_v.append(vc.reshape(BATCH, SEQ, N_KV_HEADS, HEAD_DIM))
            x = _even_mix(x, mod_l, q, k, v, ck, cv, conv, ev_w_out, j)
        else:
            f, sgu = _odd_in(x, mod_l, norm_g, od_w_in, sgu_norm_g, sgu_norm_b, sgu_w, sgu_b, layer, j)
            x = _fourier_out(x, mod_l, f, sgu, od_w_out, j)
        if layer < DEPTH - 1:
            x = ffn((x,), phases=[(layer, 1), (layer + 1, 0)])

    last = [(DEPTH - 1, 1)]
    y_prompt = ffn((x,), phases=last, row0=0, n_rows=NTOK_P)
    y_sample = ffn((x,), phases=last, row0=NTOK_P, n_rows=NTOK_S)
    return (y_prompt.reshape(BATCH, SEQ, D_MODEL), y_sample.reshape(DEC_BATCH, DEC_SEQ, D_MODEL),
            jnp.stack(new_k, axis=1), jnp.stack(new_v, axis=1))
```

```python
import functools

import numpy as np
import jax
import jax.numpy as jnp
from jax import lax
from jax.experimental import pallas as pl
from jax.experimental.pallas import tpu as pltpu

D_MODEL = 1024
BATCH = 32
SEQ = 256
DEPTH = 4
DEC_BATCH = 4
DEC_SEQ = 1024
PAST_LEN = 512
GRID_W = 64
N_HEADS = 8
N_KV_HEADS = 2
HEAD_DIM = 64
GQA_GROUP = N_HEADS // N_KV_HEADS
AXIS_ROPE_DIM = HEAD_DIM // 2
ROPE_THETA = 10000.0
ATT_Q = N_HEADS * HEAD_DIM
ATT_KV = N_KV_HEADS * HEAD_DIM
CONV_CH = 512
CONV_K = 31
FOURIER_GROUPS = 4
FOURIER_GROUP_CH = 128
FOURIER_CH = FOURIER_GROUPS * FOURIER_GROUP_CH
SGU_GROUPS = 4
SGU_GROUP_CH = 128
SGU_CH = SGU_GROUPS * SGU_GROUP_CH
CHUNK = 128
FFN_DIM = 2816
N_MOD = 9
EVEN_IN = ATT_Q + 2 * ATT_KV + 2 * CONV_CH
ODD_IN = FOURIER_CH + 2 * SGU_CH
RMS_EPS = 1e-6
LN_EPS = 1e-5
LOG2_E = 1.4426950408889634

BF16 = jnp.bfloat16
F32 = jnp.float32

V7X_SUBLANES = 8
V7X_LANES = 128
NTOK_P = BATCH * SEQ
NTOK_S = DEC_BATCH * DEC_SEQ
NTOK = NTOK_P + NTOK_S
TM = DEC_SEQ
N_TILES = NTOK // TM
N_TILES_P = NTOK_P // TM
SEQ_PER_TILE = TM // SEQ
N_COND = 8
COND_LATENT0 = 2
FFN_TM = 2 * TM
FFN_PIECE = TM
FFN_CHUNK = 256
N_FFN_CHUNKS = FFN_DIM // FFN_CHUNK
EVEN_IN_ROWS = 1024
EVEN_IN_PIECE = 256
ODD_IN_PIECE = 1024
ADALN_TN = 2304
CONV_ROWS = 128
CONV_HALO = 16
ATT_TQ = SEQ
V7X_VMEM_LIMIT_BYTES = 56 * 1024 * 1024


def _cparams(*sem):
    return pltpu.CompilerParams(dimension_semantics=sem, vmem_limit_bytes=V7X_VMEM_LIMIT_BYTES)


def _cond_of_tile(i, rows=TM):
    return jnp.maximum((i * rows - NTOK_P) // DEC_SEQ + COND_LATENT0, 0)


def _dot(a, b):
    return jnp.dot(a, b, preferred_element_type=F32)


def _modulate(x, g, shift, scale):
    ms = jnp.mean(x * x, axis=-1, keepdims=True)
    return (x * lax.rsqrt(ms + RMS_EPS)) * (g * (1.0 + scale)) + shift


def _layer_norm(x, g, b):
    mu = jnp.mean(x, axis=-1, keepdims=True)
    xc = x - mu
    var = jnp.mean(xc * xc, axis=-1, keepdims=True)
    return xc * lax.rsqrt(var + LN_EPS) * g + b


def _adaln_kernel(c_ref, w_ref, b_ref, o_ref):
    c = c_ref[...]
    s = (c * jax.nn.sigmoid(c)).astype(BF16)
    o_ref[...] = _dot(s, w_ref[...].astype(BF16)) + b_ref[...]


def _adaln(cond, w_mod, b_mod):
    n_out = N_MOD * D_MODEL
    return pl.pallas_call(
        _adaln_kernel,
        grid=(DEPTH, n_out // ADALN_TN),
        in_specs=[
            pl.BlockSpec((N_COND, D_MODEL), lambda l, n: (0, 0)),
            pl.BlockSpec((None, D_MODEL, ADALN_TN), lambda l, n: (l, 0, n)),
            pl.BlockSpec((None, 1, ADALN_TN), lambda l, n: (l, 0, n)),
        ],
        out_specs=pl.BlockSpec((None, N_COND, ADALN_TN), lambda l, n: (l, 0, n)),
        out_shape=jax.ShapeDtypeStruct((DEPTH, N_COND, n_out), F32),
        compiler_params=_cparams("parallel", "parallel"),
        name="adaln",
    )(cond, w_mod, b_mod.reshape(DEPTH, 1, n_out))


def _ffn_kernel(subs, n_x, x_split, *refs):
    n_ph = len(subs)
    x_refs = refs[:n_x]
    mod_refs = refs[n_x:n_x + n_ph]
    g_refs = refs[n_x + n_ph:n_x + 2 * n_ph]
    wg_ref, wu_ref, wo_ref, o_ref, h_scr = refs[n_x + 2 * n_ph:]
    i = pl.program_id(0)
    j = pl.program_id(1)
    halves = [(r0 // TM, slice(r0, r0 + FFN_PIECE)) for r0 in range(0, FFN_TM, FFN_PIECE)]

    if n_x == 1:
        @pl.when(j == 0)
        def _():
            o_ref[...] = x_refs[0][...]
    else:
        @pl.when((j == 0) & (i < x_split))
        def _():
            o_ref[...] = x_refs[0][...]

        @pl.when((j == 0) & (i >= x_split))
        def _():
            o_ref[...] = x_refs[1][...]

    def chunk(h, rows, out_gate):
        gate = _dot(h, wg_ref[...].astype(BF16))
        up = _dot(h, wu_ref[...].astype(BF16))
        a = (gate * jax.nn.sigmoid(gate) * up).astype(BF16)
        o_ref[rows, :] += _dot(a, (wo_ref[...] * (0.5 * out_gate)).astype(BF16))

    for p, sub in enumerate(subs):
        @pl.when(j == p * N_FFN_CHUNKS)
        def _(p=p, sub=sub):
            for c, rows in halves:
                h = _modulate(o_ref[rows, :], g_refs[p][...], mod_refs[p][c, 3 * sub:3 * sub + 1, :],
                              mod_refs[p][c, 3 * sub + 1:3 * sub + 2, :]).astype(BF16)
                h_scr[rows, :] = h
                chunk(h, rows, mod_refs[p][c, 3 * sub + 2:3 * sub + 3, :])

    @pl.when(j % N_FFN_CHUNKS != 0)
    def _():
        for c, rows in halves:
            out_gate = mod_refs[0][c, 3 * subs[0] + 2:3 * subs[0] + 3, :]
            if n_ph == 2:
                out_gate = jnp.where(j < N_FFN_CHUNKS, out_gate,
                                     mod_refs[1][c, 3 * subs[1] + 2:3 * subs[1] + 3, :])
            chunk(h_scr[rows, :], rows, out_gate)


def _ffn(xs, mod, norm_g, ffn_w_in, ffn_w_out, phases, row0=0, n_rows=NTOK):
    per = FFN_TM // TM
    tile0 = row0 // FFN_TM
    n_ph = len(phases)
    layers = [p[0] for p in phases]
    whichs = [p[1] for p in phases]

    def by_phase(vals):
        if n_ph == 1:
            return lambda j: vals[0]
        return lambda j: vals[0] + (j // N_FFN_CHUNKS) * (vals[1] - vals[0])

    layer_of, which_of = by_phase(layers), by_phase(whichs)
    chunk_of = lambda j: j % N_FFN_CHUNKS
    if len(xs) == 1:
        x_specs = [pl.BlockSpec((FFN_TM, D_MODEL), lambda i, j: (tile0 + i, 0))]
        x_split = 0
    else:
        x_split = xs[0].shape[0] // FFN_TM
        x_specs = [
            pl.BlockSpec((FFN_TM, D_MODEL), lambda i, j: (jnp.minimum(i, x_split - 1), 0)),
            pl.BlockSpec((FFN_TM, D_MODEL), lambda i, j: (jnp.maximum(i - x_split, 0), 0),
                         pipeline_mode=pl.Buffered(1)),
        ]
    mod_specs = [pl.BlockSpec((None, per, N_MOD, D_MODEL),
                              lambda i, j, l=l: (l, _cond_of_tile(tile0 + i, FFN_TM) // per, 0, 0))
                 for l in layers]
    g_specs = [pl.BlockSpec((None, None, 1, D_MODEL), lambda i, j, l=l, w=w: (l, 2 * w, 0, 0))
               for l, w in phases]
    return pl.pallas_call(
        functools.partial(_ffn_kernel, tuple(2 * w for w in whichs), len(xs), x_split),
        grid=(n_rows // FFN_TM, n_ph * N_FFN_CHUNKS),
        in_specs=x_specs + mod_specs + g_specs + [
            pl.BlockSpec((None, None, D_MODEL, FFN_CHUNK),
                         lambda i, j: (layer_of(j), which_of(j), 0, chunk_of(j))),
            pl.BlockSpec((None, None, D_MODEL, FFN_CHUNK),
                         lambda i, j: (layer_of(j), which_of(j), 0, N_FFN_CHUNKS + chunk_of(j))),
            pl.BlockSpec((None, None, FFN_CHUNK, D_MODEL),
                         lambda i, j: (layer_of(j), which_of(j), chunk_of(j), 0)),
        ],
        out_specs=pl.BlockSpec((FFN_TM, D_MODEL), lambda i, j: (i, 0)),
        out_shape=jax.ShapeDtypeStruct((n_rows, D_MODEL), F32),
        scratch_shapes=[pltpu.VMEM((FFN_TM, D_MODEL), BF16)],
        compiler_params=_cparams("parallel", "arbitrary"),
        name="ffn",
    )(*xs, *([mod] * n_ph), *([norm_g.reshape(DEPTH, 3, 1, D_MODEL)] * n_ph),
      ffn_w_in, ffn_w_in, ffn_w_out)


def _group_mean_sq(x, width):
    r = lax.broadcasted_iota(jnp.int32, (width, width), 0) // HEAD_DIM
    c = lax.broadcasted_iota(jnp.int32, (width, width), 1) // HEAD_DIM
    ones = (r == c).astype(BF16)
    return _dot((x * x).astype(BF16), ones) * (1.0 / HEAD_DIM)


def _head_norm_rope(x, gain, cos, sin_next, sin_prev):
    width = x.shape[-1]
    xn = x * lax.rsqrt(_group_mean_sq(x, width) + RMS_EPS) * gain
    nxt = pltpu.roll(xn, width - 1, 1)
    prv = pltpu.roll(xn, 1, 1)
    return xn * cos + nxt * sin_next + prv * sin_prev


def _even_in_kernel(x_ref, mod_ref, g_ref, w_ref, qg_ref, kg_ref,
                    cq_ref, snq_ref, spq_ref, ck_ref, snk_ref, spk_ref,
                    q_ref, k_ref, v_ref, glu_ref, kc_ref, vc_ref, w_scr):
    i = pl.program_id(0)

    @pl.when(i == 0)
    def _():
        w_scr[...] = w_ref[...].astype(BF16)

    a0 = ATT_Q + 2 * ATT_KV
    for r0 in range(0, EVEN_IN_ROWS, EVEN_IN_PIECE):
        rows = slice(r0, r0 + EVEN_IN_PIECE)
        h = _modulate(x_ref[rows, :], g_ref[...], mod_ref[0, 3:4, :], mod_ref[0, 4:5, :]).astype(BF16)

        def proj(lo, hi):
            return _dot(h, w_scr[:, lo:hi])

        q = _head_norm_rope(proj(0, ATT_Q), qg_ref[...], cq_ref[rows, :], snq_ref[rows, :], spq_ref[rows, :])
        q_ref[rows, :] = (q * (HEAD_DIM ** -0.5 * LOG2_E)).astype(BF16)
        k_ref[rows, :] = _head_norm_rope(proj(ATT_Q, ATT_Q + ATT_KV), kg_ref[...],
                                         ck_ref[rows, :], snk_ref[rows, :], spk_ref[rows, :])
        v_ref[rows, :] = proj(ATT_Q + ATT_KV, a0)
        glu_ref[rows, :] = proj(a0, a0 + CONV_CH) * jax.nn.sigmoid(proj(a0 + CONV_CH, a0 + 2 * CONV_CH))

    @pl.when(i < NTOK_P // EVEN_IN_ROWS)
    def _():
        kc_ref[...] = k_ref[...]
        vc_ref[...] = v_ref[...]


def _rope_tables():
    t = np.arange(DEC_SEQ)
    row = (t // GRID_W).astype(np.float64)
    col = (t % GRID_W).astype(np.float64)
    inv_freq = 1.0 / (ROPE_THETA ** (np.arange(0, AXIS_ROPE_DIM, 2, dtype=np.float64) / AXIS_ROPE_DIM))
    ang = np.concatenate([row[:, None] * inv_freq, col[:, None] * inv_freq], axis=-1)
    ang = np.repeat(ang, 2, axis=-1)
    even = (np.arange(HEAD_DIM) % 2 == 0)[None, :]
    cos = np.cos(ang)
    sin_next = np.where(even, -np.sin(ang), 0.0)
    sin_prev = np.where(even, 0.0, np.sin(ang))

    def both(tab):
        ident = np.ones_like(tab) if tab is cos else np.zeros_like(tab)
        return np.stack([np.tile(ident, (1, N_HEADS)), np.tile(tab, (1, N_HEADS))]).astype(np.float32)

    return both(cos), both(sin_next), both(sin_prev)


def _even_in(x, mod_l, norm_g, ev_w_in, q_norm_g, k_norm_g, layer, j):
    cos, sin_next, sin_prev = (jnp.asarray(t) for t in _rope_tables())
    qg = jnp.tile(q_norm_g[j], N_HEADS).reshape(1, ATT_Q)
    kg = jnp.tile(k_norm_g[j], N_KV_HEADS).reshape(1, ATT_KV)
    rows = EVEN_IN_ROWS
    per_seq = DEC_SEQ // rows
    first_latent = NTOK_P // rows
    tab_idx = lambda i: (jnp.minimum(i // first_latent, 1), i % per_seq, 0)
    tab_q = pl.BlockSpec((None, rows, ATT_Q), tab_idx)
    tab_k = pl.BlockSpec((None, rows, ATT_KV), tab_idx)
    tok = lambda n: pl.BlockSpec((rows, n), lambda i: (i, 0))
    cache = pl.BlockSpec((rows, ATT_KV), lambda i: (jnp.minimum(i, first_latent - 1), 0))
    return pl.pallas_call(
        _even_in_kernel,
        grid=(NTOK // rows,),
        in_specs=[
            tok(D_MODEL),
            pl.BlockSpec((1, N_MOD, D_MODEL), lambda i: (_cond_of_tile(i, rows), 0, 0)),
            pl.BlockSpec((None, None, 1, D_MODEL), lambda i: (layer, 1, 0, 0)),
            pl.BlockSpec((None, D_MODEL, EVEN_IN), lambda i: (j, 0, 0), pipeline_mode=pl.Buffered(1)),
            pl.BlockSpec((1, ATT_Q), lambda i: (0, 0)),
            pl.BlockSpec((1, ATT_KV), lambda i: (0, 0)),
            tab_q, tab_q, tab_q, tab_k, tab_k, tab_k,
        ],
        out_specs=[tok(ATT_Q), tok(ATT_KV), tok(ATT_KV), tok(CONV_CH), cache, cache],
        out_shape=[
            jax.ShapeDtypeStruct((NTOK, ATT_Q), BF16),
            jax.ShapeDtypeStruct((NTOK, ATT_KV), F32),
            jax.ShapeDtypeStruct((NTOK, ATT_KV), F32),
            jax.ShapeDtypeStruct((NTOK, CONV_CH), F32),
            jax.ShapeDtypeStruct((NTOK_P, ATT_KV), F32),
            jax.ShapeDtypeStruct((NTOK_P, ATT_KV), F32),
        ],
        scratch_shapes=[pltpu.VMEM((D_MODEL, EVEN_IN), BF16)],
        compiler_params=_cparams("arbitrary"),
        name="even_in",
    )(x, mod_l, norm_g.reshape(DEPTH, 3, 1, D_MODEL), ev_w_in, qg, kg,
      cos, sin_next, sin_prev, cos, sin_next, sin_prev)


ATT_WIDTH = GQA_GROUP * HEAD_DIM


def _expand_kv_head(x, kv):
    lane = lax.broadcasted_iota(jnp.int32, x.shape, 1)
    swapped = pltpu.roll(x, HEAD_DIM, 1)
    pair = jnp.where((lane // HEAD_DIM) == kv, x, swapped)
    return jnp.concatenate([pair, pair], axis=1)


def _attend(q_ref, q_rows, kbig, vbig, n_keys, attn_scr):
    lane_group = lax.broadcasted_iota(jnp.int32, (ATT_TQ, ATT_WIDTH), 1) // HEAD_DIM
    for kv in range(N_KV_HEADS):
        cols = slice(kv * ATT_WIDTH, (kv + 1) * ATT_WIDTH)
        q = q_ref[q_rows, cols]
        kb = kbig[kv, 0:n_keys, :]
        vb = vbig[kv, 0:n_keys, :]
        acc = jnp.zeros((ATT_TQ, ATT_WIDTH), F32)
        for g in range(GQA_GROUP):
            mine = lane_group == g
            qm = jnp.where(mine, q, jnp.zeros_like(q))
            s = lax.dot_general(qm, kb, (((1,), (1,)), ((), ())), preferred_element_type=F32)
            m = jnp.max(s, axis=-1, keepdims=True)
            p = jnp.exp2(s - m)
            denom = jnp.sum(p, axis=-1, keepdims=True)
            o = _dot(p.astype(BF16), vb)
            acc = jnp.where(mine, o / denom, acc)
        attn_scr[:, cols] = acc.astype(BF16)


CONV_PAD_SEQ = SEQ + 2 * CONV_HALO
CONV_WIN = ATT_TQ + 2 * CONV_HALO
CONV_SPAN = CONV_WIN - V7X_SUBLANES


def _conv_block(win0_scr, win_scr, w_ref, y_scr):
    off = CONV_HALO - CONV_K // 2
    for r0 in range(0, ATT_TQ, CONV_ROWS):
        for c0 in range(0, CONV_CH, V7X_LANES):
            cols = slice(c0, c0 + V7X_LANES)
            acc = jnp.zeros((CONV_ROWS, V7X_LANES), F32)
            for k in range(CONV_K):
                res = (off + k) % V7X_SUBLANES
                a = r0 + off + k - res
                if res == 0:
                    tap = win0_scr[a:a + CONV_ROWS, cols]
                else:
                    tap = win_scr[res - 1, a:a + CONV_ROWS, cols]
                acc = acc + tap * w_ref[k:k + 1, cols]
            y_scr[r0:r0 + CONV_ROWS, cols] = acc


def _even_mix_kernel(q_ref, k_ref, v_ref, ck_ref, cv_ref, glu_ref, w_ref, b_ref, g_ref, beta_ref,
                     res_ref, mod_ref, wo_ref, o_ref,
                     kbig, vbig, pad_scr, win0_scr, win_scr, y_scr, attn_scr, wo_scr):
    i = pl.program_id(0)
    t = pl.program_id(1)
    rows = pl.ds(pl.multiple_of(t * ATT_TQ, ATT_TQ), ATT_TQ)
    border = jnp.zeros((CONV_HALO, CONV_CH), F32)

    @pl.when((i == 0) & (t == 0))
    def _():
        wo_scr[...] = wo_ref[...].astype(BF16)

    def expand(k, v, n):
        for kv in range(N_KV_HEADS):
            kbig[kv, 0:n, :] = _expand_kv_head(k, kv).astype(BF16)
            vbig[kv, 0:n, :] = _expand_kv_head(v, kv).astype(BF16)

    def mix(n_keys, win_start):
        win0_scr[...] = pad_scr[pl.ds(pl.multiple_of(win_start, V7X_SUBLANES), CONV_WIN), :]
        for res in range(1, V7X_SUBLANES):
            win_scr[res - 1] = win0_scr[res:res + CONV_SPAN, :]
        _attend(q_ref, rows, kbig, vbig, n_keys, attn_scr)
        _conv_block(win0_scr, win_scr, w_ref, y_scr)
        y = _layer_norm(y_scr[...] + b_ref[...], g_ref[...], beta_ref[...])
        conv = (y * jax.nn.sigmoid(y)).astype(BF16)
        mixed = _dot(attn_scr[...], wo_scr[:ATT_Q, :]) + _dot(conv, wo_scr[ATT_Q:, :])
        o_ref[...] = res_ref[...] + mod_ref[0, 5:6, :] * mixed

    @pl.when(i < N_TILES_P)
    def _():
        @pl.when(t == 0)
        def _():
            for s in range(SEQ_PER_TILE):
                base = s * CONV_PAD_SEQ
                pad_scr[base:base + CONV_HALO, :] = border
                pad_scr[base + CONV_HALO:base + CONV_HALO + SEQ, :] = glu_ref[s * SEQ:(s + 1) * SEQ, :]
                pad_scr[base + CONV_HALO + SEQ:base + CONV_PAD_SEQ, :] = border

        expand(k_ref[rows, :], v_ref[rows, :], SEQ)
        mix(SEQ, t * CONV_PAD_SEQ)

    @pl.when(i >= N_TILES_P)
    def _():
        @pl.when(t == 0)
        def _():
            pad_scr[0:CONV_HALO, :] = border
            pad_scr[CONV_HALO:CONV_HALO + DEC_SEQ, :] = glu_ref[...]
            pad_scr[CONV_HALO + DEC_SEQ:2 * CONV_HALO + DEC_SEQ, :] = border
            expand(jnp.concatenate([ck_ref[...], k_ref[...]], axis=0),
                   jnp.concatenate([cv_ref[...], v_ref[...]], axis=0), PAST_LEN + DEC_SEQ)

        mix(PAST_LEN + DEC_SEQ, t * ATT_TQ)


def _even_mix(x, mod_l, q, k, v, ck, cv, glu, conv_w, conv_b, cn_g, cn_b, w_out, j):
    nq = TM // ATT_TQ
    tok = lambda n: pl.BlockSpec((TM, n), lambda i, t: (i, 0))
    blk = pl.BlockSpec((ATT_TQ, D_MODEL), lambda i, t: (i * nq + t, 0))
    ctx = pl.BlockSpec((None, None, PAST_LEN, ATT_KV), lambda i, t: (jnp.maximum(i - N_TILES_P, 0), j, 0, 0))
    vec = pl.BlockSpec((None, 1, CONV_CH), lambda i, t: (j, 0, 0))
    n_even = conv_b.shape[0]
    n_keys = PAST_LEN + DEC_SEQ
    return pl.pallas_call(
        _even_mix_kernel,
        grid=(N_TILES, nq),
        in_specs=[
            tok(ATT_Q), tok(ATT_KV), tok(ATT_KV), ctx, ctx, tok(CONV_CH),
            pl.BlockSpec((None, CONV_K, CONV_CH), lambda i, t: (j, 0, 0)),
            vec, vec, vec,
            blk,
            pl.BlockSpec((1, N_MOD, D_MODEL), lambda i, t: (_cond_of_tile(i), 0, 0)),
            pl.BlockSpec((None, ATT_Q + CONV_CH, D_MODEL), lambda i, t: (j, 0, 0),
                         pipeline_mode=pl.Buffered(1)),
        ],
        out_specs=blk,
        out_shape=jax.ShapeDtypeStruct((NTOK, D_MODEL), F32),
        scratch_shapes=[
            pltpu.VMEM((N_KV_HEADS, n_keys, ATT_WIDTH), BF16),
            pltpu.VMEM((N_KV_HEADS, n_keys, ATT_WIDTH), BF16),
            pltpu.VMEM((max(SEQ_PER_TILE * CONV_PAD_SEQ, DEC_SEQ + 2 * CONV_HALO), CONV_CH), F32),
            pltpu.VMEM((CONV_WIN, CONV_CH), F32),
            pltpu.VMEM((V7X_SUBLANES - 1, CONV_SPAN, CONV_CH), F32),
            pltpu.VMEM((ATT_TQ, CONV_CH), F32),
            pltpu.VMEM((ATT_TQ, ATT_Q), BF16),
            pltpu.VMEM((ATT_Q + CONV_CH, D_MODEL), BF16),
        ],
        compiler_params=_cparams("arbitrary", "arbitrary"),
        name="even_mix",
    )(q, k, v, ck, cv, glu, conv_w, conv_b.reshape(n_even, 1, CONV_CH),
      cn_g.reshape(n_even, 1, CONV_CH), cn_b.reshape(n_even, 1, CONV_CH), x, mod_l, w_out)


def _odd_in_kernel(x_ref, mod_ref, g_ref, w_ref, sg_ref, sb_ref, ws_ref, bs_ref, f_ref, sgu_ref, w_scr):
    @pl.when(pl.program_id(0) == 0)
    def _():
        w_scr[...] = w_ref[...].astype(BF16)

    ws = [ws_ref[g].astype(BF16) for g in range(SGU_GROUPS)]
    for r0 in range(0, TM, ODD_IN_PIECE):
        piece = slice(r0, r0 + ODD_IN_PIECE)
        h = _modulate(x_ref[piece, :], g_ref[...], mod_ref[0, 3:4, :], mod_ref[0, 4:5, :]).astype(BF16)

        def proj(lo, hi):
            return _dot(h, w_scr[:, lo:hi])

        f_ref[piece, :] = proj(0, FOURIER_CH).astype(BF16)
        u = proj(FOURIER_CH, FOURIER_CH + SGU_CH)
        v = _layer_norm(proj(FOURIER_CH + SGU_CH, ODD_IN), sg_ref[...], sb_ref[...]).astype(BF16)
        for g in range(SGU_GROUPS):
            bias = bs_ref[:, g:g + 1]
            cols = slice(g * SGU_GROUP_CH, (g + 1) * SGU_GROUP_CH)
            for n in range(ODD_IN_PIECE // CHUNK):
                rows = slice(n * CHUNK, (n + 1) * CHUNK)
                mixed = _dot(ws[g], v[rows, cols]) + bias
                sgu_ref[r0 + n * CHUNK:r0 + (n + 1) * CHUNK, cols] = (u[rows, cols] * mixed).astype(BF16)


def _odd_in(x, mod_l, norm_g, od_w_in, sgu_norm_g, sgu_norm_b, sgu_w, sgu_b, layer, j):
    tok = lambda n: pl.BlockSpec((TM, n), lambda i: (i, 0))
    return pl.pallas_call(
        _odd_in_kernel,
        grid=(N_TILES,),
        in_specs=[
            tok(D_MODEL),
            pl.BlockSpec((1, N_MOD, D_MODEL), lambda i: (_cond_of_tile(i), 0, 0)),
            pl.BlockSpec((None, None, 1, D_MODEL), lambda i: (layer, 1, 0, 0)),
            pl.BlockSpec((None, D_MODEL, ODD_IN), lambda i: (j, 0, 0), pipeline_mode=pl.Buffered(1)),
            pl.BlockSpec((1, SGU_CH), lambda i: (0, 0)),
            pl.BlockSpec((1, SGU_CH), lambda i: (0, 0)),
            pl.BlockSpec((None, SGU_GROUPS, CHUNK, CHUNK), lambda i: (j, 0, 0, 0)),
            pl.BlockSpec((CHUNK, SGU_GROUPS), lambda i: (0, 0)),
        ],
        out_specs=[tok(FOURIER_CH), tok(SGU_CH)],
        out_shape=[jax.ShapeDtypeStruct((NTOK, FOURIER_CH), BF16),
                   jax.ShapeDtypeStruct((NTOK, SGU_CH), BF16)],
        scratch_shapes=[pltpu.VMEM((D_MODEL, ODD_IN), BF16)],
        compiler_params=_cparams("arbitrary"),
        name="odd_in",
    )(x, mod_l, norm_g.reshape(DEPTH, 3, 1, D_MODEL), od_w_in,
      sgu_norm_g[j].reshape(1, SGU_CH), sgu_norm_b[j].reshape(1, SGU_CH), sgu_w, sgu_b[j].T)


def _fourier_out_kernel(f_ref, wch_ref, wseq_p_ref, wseq_s_ref, res_ref, mod_ref, sgu_ref, wo_ref,
                        o_ref, wch_scr, wseq_p_scr, wseq_s_scr, wo_scr):
    i = pl.program_id(0)

    @pl.when(i == 0)
    def _():
        wch_scr[...] = wch_ref[...].astype(BF16)
        wseq_p_scr[...] = wseq_p_ref[...].astype(BF16)
        wseq_s_scr[...] = wseq_s_ref[...].astype(BF16)
        wo_scr[...] = wo_ref[...].astype(BF16)

    gate = mod_ref[0, 5:6, :]

    def channel_dft(rows):
        pq = _dot(f_ref[rows, :], wch_scr[...]).astype(BF16)
        return jnp.concatenate([pq[:, :FOURIER_CH], pq[:, FOURIER_CH:]], axis=0)

    def finish(rows, wseq_rows, stacked, seq):
        scale = 1.0 / np.sqrt(float(seq * FOURIER_GROUP_CH))
        four = (_dot(wseq_rows, stacked) * scale).astype(BF16)
        mixed = _dot(four, wo_scr[:FOURIER_CH, :]) + _dot(sgu_ref[rows, :], wo_scr[FOURIER_CH:, :])
        o_ref[rows, :] = res_ref[rows, :] + gate * mixed

    @pl.when(i < N_TILES_P)
    def _():
        for s in range(SEQ_PER_TILE):
            rows = slice(s * SEQ, (s + 1) * SEQ)
            finish(rows, wseq_p_scr[...], channel_dft(rows), SEQ)

    @pl.when(i >= N_TILES_P)
    def _():
        stacked = channel_dft(slice(0, DEC_SEQ))
        for s in range(SEQ_PER_TILE):
            rows = slice(s * SEQ, (s + 1) * SEQ)
            finish(rows, wseq_s_scr[rows, :], stacked, DEC_SEQ)


def _dft_constants():
    def cos_sin(n):
        idx = np.arange(n)
        ang = 2.0 * np.pi * ((idx[:, None] * idx[None, :]) % n) / n
        return np.cos(ang), np.sin(ang)

    cc, sc = cos_sin(FOURIER_GROUP_CH)
    eye = np.eye(FOURIER_GROUPS)
    wch = np.concatenate([np.kron(eye, cc), np.kron(eye, sc)], axis=1)

    def wseq(seq):
        cl, sl = cos_sin(seq)
        return jnp.asarray(np.concatenate([cl, -sl], axis=1), dtype=F32)

    return jnp.asarray(wch, dtype=F32), wseq(SEQ), wseq(DEC_SEQ)


def _fourier_out(x, mod_l, f, sgu, w_out, j):
    wch, wseq_p, wseq_s = _dft_constants()
    tok = lambda n: pl.BlockSpec((TM, n), lambda i: (i, 0))
    const = lambda shape: pl.BlockSpec(shape, lambda i: (0,) * len(shape), pipeline_mode=pl.Buffered(1))
    return pl.pallas_call(
        _fourier_out_kernel,
        grid=(N_TILES,),
        in_specs=[
            tok(FOURIER_CH), const(wch.shape), const(wseq_p.shape), const(wseq_s.shape),
            tok(D_MODEL),
            pl.BlockSpec((1, N_MOD, D_MODEL), lambda i: (_cond_of_tile(i), 0, 0)),
            tok(SGU_CH),
            pl.BlockSpec((None, FOURIER_CH + SGU_CH, D_MODEL), lambda i: (j, 0, 0),
                         pipeline_mode=pl.Buffered(1)),
        ],
        out_specs=tok(D_MODEL),
        out_shape=jax.ShapeDtypeStruct((NTOK, D_MODEL), F32),
        scratch_shapes=[pltpu.VMEM(wch.shape, BF16),
                        pltpu.VMEM(wseq_p.shape, BF16), pltpu.VMEM(wseq_s.shape, BF16),
                        pltpu.VMEM((FOURIER_CH + SGU_CH, D_MODEL), BF16)],
        compiler_params=_cparams("arbitrary"),
        name="fourier_out",
    )(f, wch, wseq_p, wseq_s, x, mod_l, sgu, w_out)


def kernel(x_prompt, x_sample, cache_k, cache_v, c, c_ctx, w_mod, b_mod, norm_g, ffn_w_in, ffn_w_out, ev_w_in, ev_w_out, q_norm_g, k_norm_g, conv_w, conv_b, conv_norm_g, conv_norm_b, od_w_in, od_w_out, sgu_norm_g, sgu_norm_b, sgu_w, sgu_b):
    cond = jnp.concatenate([c_ctx[None, :], c_ctx[None, :], c,
                            jnp.zeros((N_COND - COND_LATENT0 - DEC_BATCH, D_MODEL), F32)], axis=0)
    mod = _adaln(cond, w_mod, b_mod).reshape(DEPTH, N_COND, N_MOD, D_MODEL)
    ck = cache_k.reshape(DEC_BATCH, -1, PAST_LEN, ATT_KV)
    cv = cache_v.reshape(DEC_BATCH, -1, PAST_LEN, ATT_KV)

    ffn = functools.partial(_ffn, mod=mod, norm_g=norm_g, ffn_w_in=ffn_w_in, ffn_w_out=ffn_w_out)
    x = ffn((x_prompt.reshape(NTOK_P, D_MODEL), x_sample.reshape(NTOK_S, D_MODEL)), phases=[(0, 0)])
    new_k, new_v = [], []
    for layer in range(DEPTH):
        mod_l = mod[layer]
        j = layer // 2
        if layer % 2 == 0:
            q, k, v, glu, kc, vc = _even_in(x, mod_l, norm_g, ev_w_in, q_norm_g, k_norm_g, layer, j)
            new_k.append(kc.reshape(BATCH, SEQ, N_KV_HEADS, HEAD_DIM))
            new_v.append(vc.reshape(BATCH, SEQ, N_KV_HEADS, HEAD_DIM))
            x = _even_mix(x, mod_l, q, k, v, ck, cv, glu, conv_w, conv_b, conv_norm_g, conv_norm_b,
                          ev_w_out, j)
        else:
            f, sgu = _odd_in(x, mod_l, norm_g, od_w_in, sgu_norm_g, sgu_norm_b, sgu_w, sgu_b, layer, j)
            x = _fourier_out(x, mod_l, f, sgu, od_w_out, j)
        if layer < DEPTH - 1:
            x = ffn((x,), phases=[(layer, 1), (layer + 1, 0)])

    last = [(DEPTH - 1, 1)]
    y_prompt = ffn((x,), phases=last, row0=0, n_rows=NTOK_P)
    y_sample = ffn((x,), phases=last, row0=NTOK_P, n_rows=NTOK_S)
    return (y_prompt.reshape(BATCH, SEQ, D_MODEL), y_sample.reshape(DEC_BATCH, DEC_SEQ, D_MODEL),
            jnp.stack(new_k, axis=1), jnp.stack(new_v, axis=1))
```

```python
import functools

import numpy as np
import jax
import jax.numpy as jnp
from jax import lax
from jax.experimental import pallas as pl
from jax.experimental.pallas import tpu as pltpu

D_MODEL = 1024
BATCH = 32
SEQ = 256
DEPTH = 4
DEC_BATCH = 4
DEC_SEQ = 1024
PAST_LEN = 512
GRID_W = 64
N_HEADS = 8
N_KV_HEADS = 2
HEAD_DIM = 64
GQA_GROUP = N_HEADS // N_KV_HEADS
AXIS_ROPE_DIM = HEAD_DIM // 2
ROPE_THETA = 10000.0
ATT_Q = N_HEADS * HEAD_DIM
ATT_KV = N_KV_HEADS * HEAD_DIM
CONV_CH = 512
CONV_K = 31
FOURIER_GROUPS = 4
FOURIER_GROUP_CH = 128
FOURIER_CH = FOURIER_GROUPS * FOURIER_GROUP_CH
SGU_GROUPS = 4
SGU_GROUP_CH = 128
SGU_CH = SGU_GROUPS * SGU_GROUP_CH
CHUNK = 128
FFN_DIM = 2816
N_MOD = 9
EVEN_IN = ATT_Q + 2 * ATT_KV + 2 * CONV_CH
ODD_IN = FOURIER_CH + 2 * SGU_CH
RMS_EPS = 1e-6
LN_EPS = 1e-5
LOG2_E = 1.4426950408889634

BF16 = jnp.bfloat16
F32 = jnp.float32

V7X_SUBLANES = 8
V7X_LANES = 128
NTOK_P = BATCH * SEQ
NTOK_S = DEC_BATCH * DEC_SEQ
NTOK = NTOK_P + NTOK_S
TM = DEC_SEQ
N_TILES = NTOK // TM
N_TILES_P = NTOK_P // TM
SEQ_PER_TILE = TM // SEQ
N_COND = 8
COND_LATENT0 = 2
FFN_TM = 2 * TM
FFN_PIECE = TM
FFN_CHUNK = 256
N_FFN_CHUNKS = FFN_DIM // FFN_CHUNK
EVEN_IN_ROWS = 1024
EVEN_IN_PIECE = 256
ODD_IN_PIECE = 1024
ADALN_TN = 2304
ADALN_LATE_TN = 768
ADALN_LATE_BLOCKS = (DEPTH - 1) * (N_MOD * D_MODEL // ADALN_LATE_TN)
CONV_ROWS = 128
CONV_HALO = 16
ATT_TQ = SEQ
V7X_VMEM_LIMIT_BYTES = 56 * 1024 * 1024


def _cparams(*sem):
    return pltpu.CompilerParams(dimension_semantics=sem, vmem_limit_bytes=V7X_VMEM_LIMIT_BYTES)


def _cond_of_tile(i, rows=TM):
    return jnp.maximum((i * rows - NTOK_P) // DEC_SEQ + COND_LATENT0, 0)


def _dot(a, b):
    return jnp.dot(a, b, preferred_element_type=F32)


def _modulate(x, g, shift, scale):
    ms = jnp.mean(x * x, axis=-1, keepdims=True)
    return (x * lax.rsqrt(ms + RMS_EPS)) * (g * (1.0 + scale)) + shift


def _layer_norm(x, g, b):
    mu = jnp.mean(x, axis=-1, keepdims=True)
    xc = x - mu
    var = jnp.mean(xc * xc, axis=-1, keepdims=True)
    return xc * lax.rsqrt(var + LN_EPS) * g + b


def _adaln_kernel(c_ref, w_ref, b_ref, o_ref):
    c = c_ref[...]
    s = (c * jax.nn.sigmoid(c)).astype(BF16)
    o_ref[...] = _dot(s, w_ref[...].astype(BF16)) + b_ref[...]


def _adaln(cond, w_mod, b_mod, n_layers):
    n_out = N_MOD * D_MODEL
    return pl.pallas_call(
        _adaln_kernel,
        grid=(n_layers, n_out // ADALN_TN),
        in_specs=[
            pl.BlockSpec((N_COND, D_MODEL), lambda l, n: (0, 0)),
            pl.BlockSpec((None, D_MODEL, ADALN_TN), lambda l, n: (l, 0, n)),
            pl.BlockSpec((None, 1, ADALN_TN), lambda l, n: (l, 0, n)),
        ],
        out_specs=pl.BlockSpec((None, N_COND, ADALN_TN), lambda l, n: (l, 0, n)),
        out_shape=jax.ShapeDtypeStruct((n_layers, N_COND, n_out), F32),
        compiler_params=_cparams("parallel", "parallel"),
        name="adaln",
    )(cond, w_mod, b_mod.reshape(DEPTH, 1, n_out))


def _ffn_kernel(subs, n_x, x_split, *refs):
    n_ph = len(subs)
    x_refs = refs[:n_x]
    mod_refs = refs[n_x:n_x + n_ph]
    g_refs = refs[n_x + n_ph:n_x + 2 * n_ph]
    wg_ref, wu_ref, wo_ref, o_ref, h_scr = refs[n_x + 2 * n_ph:]
    i = pl.program_id(0)
    j = pl.program_id(1)
    halves = [(r0 // TM, slice(r0, r0 + FFN_PIECE)) for r0 in range(0, FFN_TM, FFN_PIECE)]

    if n_x == 1:
        @pl.when(j == 0)
        def _():
            o_ref[...] = x_refs[0][...]
    else:
        @pl.when((j == 0) & (i < x_split))
        def _():
            o_ref[...] = x_refs[0][...]

        @pl.when((j == 0) & (i >= x_split))
        def _():
            o_ref[...] = x_refs[1][...]

    def chunk(h, rows, out_gate):
        gate = _dot(h, wg_ref[...].astype(BF16))
        up = _dot(h, wu_ref[...].astype(BF16))
        a = (gate * jax.nn.sigmoid(gate) * up).astype(BF16)
        o_ref[rows, :] += _dot(a, (wo_ref[...] * (0.5 * out_gate)).astype(BF16))

    for p, sub in enumerate(subs):
        @pl.when(j == p * N_FFN_CHUNKS)
        def _(p=p, sub=sub):
            for c, rows in halves:
                h = _modulate(o_ref[rows, :], g_refs[p][...], mod_refs[p][c, 3 * sub:3 * sub + 1, :],
                              mod_refs[p][c, 3 * sub + 1:3 * sub + 2, :]).astype(BF16)
                h_scr[rows, :] = h
                chunk(h, rows, mod_refs[p][c, 3 * sub + 2:3 * sub + 3, :])

    @pl.when(j % N_FFN_CHUNKS != 0)
    def _():
        for c, rows in halves:
            out_gate = mod_refs[0][c, 3 * subs[0] + 2:3 * subs[0] + 3, :]
            if n_ph == 2:
                out_gate = jnp.where(j < N_FFN_CHUNKS, out_gate,
                                     mod_refs[1][c, 3 * subs[1] + 2:3 * subs[1] + 3, :])
            chunk(h_scr[rows, :], rows, out_gate)


def _ffn(xs, mod, norm_g, ffn_w_in, ffn_w_out, phases, row0=0, n_rows=NTOK):
    per = FFN_TM // TM
    tile0 = row0 // FFN_TM
    n_ph = len(phases)
    layers = [p[0] for p in phases]
    whichs = [p[1] for p in phases]

    def by_phase(vals):
        if n_ph == 1:
            return lambda j: vals[0]
        return lambda j: vals[0] + (j // N_FFN_CHUNKS) * (vals[1] - vals[0])

    layer_of, which_of = by_phase(layers), by_phase(whichs)
    chunk_of = lambda j: j % N_FFN_CHUNKS
    if len(xs) == 1:
        x_specs = [pl.BlockSpec((FFN_TM, D_MODEL), lambda i, j: (tile0 + i, 0))]
        x_split = 0
    else:
        x_split = xs[0].shape[0] // FFN_TM
        x_specs = [
            pl.BlockSpec((FFN_TM, D_MODEL), lambda i, j: (jnp.minimum(i, x_split - 1), 0)),
            pl.BlockSpec((FFN_TM, D_MODEL), lambda i, j: (jnp.maximum(i - x_split, 0), 0),
                         pipeline_mode=pl.Buffered(1)),
        ]
    mod_specs = [pl.BlockSpec((None, per, N_MOD, D_MODEL),
                              lambda i, j, l=l: (l, _cond_of_tile(tile0 + i, FFN_TM) // per, 0, 0))
                 for l in layers]
    g_specs = [pl.BlockSpec((None, None, 1, D_MODEL), lambda i, j, l=l, w=w: (l, 2 * w, 0, 0))
               for l, w in phases]
    return pl.pallas_call(
        functools.partial(_ffn_kernel, tuple(2 * w for w in whichs), len(xs), x_split),
        grid=(n_rows // FFN_TM, n_ph * N_FFN_CHUNKS),
        in_specs=x_specs + mod_specs + g_specs + [
            pl.BlockSpec((None, None, D_MODEL, FFN_CHUNK),
                         lambda i, j: (layer_of(j), which_of(j), 0, chunk_of(j))),
            pl.BlockSpec((None, None, D_MODEL, FFN_CHUNK),
                         lambda i, j: (layer_of(j), which_of(j), 0, N_FFN_CHUNKS + chunk_of(j))),
            pl.BlockSpec((None, None, FFN_CHUNK, D_MODEL),
                         lambda i, j: (layer_of(j), which_of(j), chunk_of(j), 0)),
        ],
        out_specs=pl.BlockSpec((FFN_TM, D_MODEL), lambda i, j: (i, 0)),
        out_shape=jax.ShapeDtypeStruct((n_rows, D_MODEL), F32),
        scratch_shapes=[pltpu.VMEM((FFN_TM, D_MODEL), BF16)],
        compiler_params=_cparams("parallel", "arbitrary"),
        name="ffn",
    )(*xs, *([mod] * n_ph), *([norm_g.reshape(DEPTH, 3, 1, D_MODEL)] * n_ph),
      ffn_w_in, ffn_w_in, ffn_w_out)


def _group_mean_sq(x, width):
    r = lax.broadcasted_iota(jnp.int32, (width, width), 0) // HEAD_DIM
    c = lax.broadcasted_iota(jnp.int32, (width, width), 1) // HEAD_DIM
    ones = (r == c).astype(BF16)
    return _dot((x * x).astype(BF16), ones) * (1.0 / HEAD_DIM)


def _head_norm_rope(x, gain, cos, sin_next, sin_prev):
    width = x.shape[-1]
    xn = x * lax.rsqrt(_group_mean_sq(x, width) + RMS_EPS) * gain
    nxt = pltpu.roll(xn, width - 1, 1)
    prv = pltpu.roll(xn, 1, 1)
    return xn * cos + nxt * sin_next + prv * sin_prev


def _even_in_kernel(x_ref, mod_ref, g_ref, w_ref, qg_ref, kg_ref,
                    cq_ref, snq_ref, spq_ref, ck_ref, snk_ref, spk_ref,
                    q_ref, k_ref, v_ref, glu_ref, kc_ref, vc_ref, w_scr):
    i = pl.program_id(0)

    @pl.when(i == 0)
    def _():
        w_scr[...] = w_ref[...].astype(BF16)

    a0 = ATT_Q + 2 * ATT_KV
    for r0 in range(0, EVEN_IN_ROWS, EVEN_IN_PIECE):
        rows = slice(r0, r0 + EVEN_IN_PIECE)
        h = _modulate(x_ref[rows, :], g_ref[...], mod_ref[0, 3:4, :], mod_ref[0, 4:5, :]).astype(BF16)

        def proj(lo, hi):
            return _dot(h, w_scr[:, lo:hi])

        q = _head_norm_rope(proj(0, ATT_Q), qg_ref[...], cq_ref[rows, :], snq_ref[rows, :], spq_ref[rows, :])
        q_ref[rows, :] = (q * (HEAD_DIM ** -0.5 * LOG2_E)).astype(BF16)
        k_ref[rows, :] = _head_norm_rope(proj(ATT_Q, ATT_Q + ATT_KV), kg_ref[...],
                                         ck_ref[rows, :], snk_ref[rows, :], spk_ref[rows, :])
        v_ref[rows, :] = proj(ATT_Q + ATT_KV, a0)
        glu_ref[rows, :] = proj(a0, a0 + CONV_CH) * jax.nn.sigmoid(proj(a0 + CONV_CH, a0 + 2 * CONV_CH))

    @pl.when(i < NTOK_P // EVEN_IN_ROWS)
    def _():
        kc_ref[...] = k_ref[...]
        vc_ref[...] = v_ref[...]


def _rope_tables():
    t = np.arange(DEC_SEQ)
    row = (t // GRID_W).astype(np.float64)
    col = (t % GRID_W).astype(np.float64)
    inv_freq = 1.0 / (ROPE_THETA ** (np.arange(0, AXIS_ROPE_DIM, 2, dtype=np.float64) / AXIS_ROPE_DIM))
    ang = np.concatenate([row[:, None] * inv_freq, col[:, None] * inv_freq], axis=-1)
    ang = np.repeat(ang, 2, axis=-1)
    even = (np.arange(HEAD_DIM) % 2 == 0)[None, :]
    cos = np.cos(ang)
    sin_next = np.where(even, -np.sin(ang), 0.0)
    sin_prev = np.where(even, 0.0, np.sin(ang))

    def both(tab):
        ident = np.ones_like(tab) if tab is cos else np.zeros_like(tab)
        return np.stack([np.tile(ident, (1, N_HEADS)), np.tile(tab, (1, N_HEADS))]).astype(np.float32)

    return both(cos), both(sin_next), both(sin_prev)


def _even_in(x, mod_l, norm_g, ev_w_in, q_norm_g, k_norm_g, layer, j):
    cos, sin_next, sin_prev = (jnp.asarray(t) for t in _rope_tables())
    qg = jnp.tile(q_norm_g[j], N_HEADS).reshape(1, ATT_Q)
    kg = jnp.tile(k_norm_g[j], N_KV_HEADS).reshape(1, ATT_KV)
    rows = EVEN_IN_ROWS
    per_seq = DEC_SEQ // rows
    first_latent = NTOK_P // rows
    tab_idx = lambda i: (jnp.minimum(i // first_latent, 1), i % per_seq, 0)
    tab_q = pl.BlockSpec((None, rows, ATT_Q), tab_idx)
    tab_k = pl.BlockSpec((None, rows, ATT_KV), tab_idx)
    tok = lambda n: pl.BlockSpec((rows, n), lambda i: (i, 0))
    cache = pl.BlockSpec((rows, ATT_KV), lambda i: (jnp.minimum(i, first_latent - 1), 0))
    return pl.pallas_call(
        _even_in_kernel,
        grid=(NTOK // rows,),
        in_specs=[
            tok(D_MODEL),
            pl.BlockSpec((1, N_MOD, D_MODEL), lambda i: (_cond_of_tile(i, rows), 0, 0)),
            pl.BlockSpec((None, None, 1, D_MODEL), lambda i: (layer, 1, 0, 0)),
            pl.BlockSpec((None, D_MODEL, EVEN_IN), lambda i: (j, 0, 0), pipeline_mode=pl.Buffered(1)),
            pl.BlockSpec((1, ATT_Q), lambda i: (0, 0)),
            pl.BlockSpec((1, ATT_KV), lambda i: (0, 0)),
            tab_q, tab_q, tab_q, tab_k, tab_k, tab_k,
        ],
        out_specs=[tok(ATT_Q), tok(ATT_KV), tok(ATT_KV), tok(CONV_CH), cache, cache],
        out_shape=[
            jax.ShapeDtypeStruct((NTOK, ATT_Q), BF16),
            jax.ShapeDtypeStruct((NTOK, ATT_KV), F32),
            jax.ShapeDtypeStruct((NTOK, ATT_KV), F32),
            jax.ShapeDtypeStruct((NTOK, CONV_CH), F32),
            jax.ShapeDtypeStruct((NTOK_P, ATT_KV), F32),
            jax.ShapeDtypeStruct((NTOK_P, ATT_KV), F32),
        ],
        scratch_shapes=[pltpu.VMEM((D_MODEL, EVEN_IN), BF16)],
        compiler_params=_cparams("arbitrary"),
        name="even_in",
    )(x, mod_l, norm_g.reshape(DEPTH, 3, 1, D_MODEL), ev_w_in, qg, kg,
      cos, sin_next, sin_prev, cos, sin_next, sin_prev)


ATT_WIDTH = GQA_GROUP * HEAD_DIM


def _expand_kv_head(x, kv):
    lane = lax.broadcasted_iota(jnp.int32, x.shape, 1)
    swapped = pltpu.roll(x, HEAD_DIM, 1)
    pair = jnp.where((lane // HEAD_DIM) == kv, x, swapped)
    return jnp.concatenate([pair, pair], axis=1)


def _attend(q_ref, q_rows, kbig, vbig, n_keys, attn_scr):
    lane_group = lax.broadcasted_iota(jnp.int32, (ATT_TQ, ATT_WIDTH), 1) // HEAD_DIM
    for kv in range(N_KV_HEADS):
        cols = slice(kv * ATT_WIDTH, (kv + 1) * ATT_WIDTH)
        q = q_ref[q_rows, cols]
        kb = kbig[kv, 0:n_keys, :]
        vb = vbig[kv, 0:n_keys, :]
        acc = jnp.zeros((ATT_TQ, ATT_WIDTH), F32)
        for g in range(GQA_GROUP):
            mine = lane_group == g
            qm = jnp.where(mine, q, jnp.zeros_like(q))
            s = lax.dot_general(qm, kb, (((1,), (1,)), ((), ())), preferred_element_type=F32)
            m = jnp.max(s, axis=-1, keepdims=True)
            p = jnp.exp2(s - m)
            denom = jnp.sum(p, axis=-1, keepdims=True)
            o = _dot(p.astype(BF16), vb)
            acc = jnp.where(mine, o / denom, acc)
        attn_scr[:, cols] = acc.astype(BF16)


CONV_PAD_SEQ = SEQ + 2 * CONV_HALO
CONV_WIN = ATT_TQ + 2 * CONV_HALO
CONV_SPAN = CONV_WIN - V7X_SUBLANES


def _conv_block(win0_scr, win_scr, w_ref, y_scr):
    off = CONV_HALO - CONV_K // 2
    for r0 in range(0, ATT_TQ, CONV_ROWS):
        for c0 in range(0, CONV_CH, V7X_LANES):
            cols = slice(c0, c0 + V7X_LANES)
            acc = jnp.zeros((CONV_ROWS, V7X_LANES), F32)
            for k in range(CONV_K):
                res = (off + k) % V7X_SUBLANES
                a = r0 + off + k - res
                if res == 0:
                    tap = win0_scr[a:a + CONV_ROWS, cols]
                else:
                    tap = win_scr[res - 1, a:a + CONV_ROWS, cols]
                acc = acc + tap * w_ref[k:k + 1, cols]
            y_scr[r0:r0 + CONV_ROWS, cols] = acc


def _even_mix_kernel(with_adaln, q_ref, k_ref, v_ref, ck_ref, cv_ref, glu_ref, w_ref, b_ref, g_ref,
                     beta_ref, res_ref, mod_ref, wo_ref, *rest):
    if with_adaln:
        cond_ref, wm_ref, bm_ref, o_ref, modout_ref = rest[:5]
        rest = rest[5:]
    else:
        o_ref = rest[0]
        rest = rest[1:]
    kbig, vbig, pad_scr, win0_scr, win_scr, y_scr, attn_scr, wo_scr = rest
    i = pl.program_id(0)
    t = pl.program_id(1)
    rows = pl.ds(pl.multiple_of(t * ATT_TQ, ATT_TQ), ATT_TQ)
    border = jnp.zeros((CONV_HALO, CONV_CH), F32)

    @pl.when((i == 0) & (t == 0))
    def _():
        wo_scr[...] = wo_ref[...].astype(BF16)

    if with_adaln:
        @pl.when(i * (TM // ATT_TQ) + t < ADALN_LATE_BLOCKS)
        def _():
            _adaln_kernel(cond_ref, wm_ref, bm_ref, modout_ref)

    def expand(k, v, n):
        for kv in range(N_KV_HEADS):
            kbig[kv, 0:n, :] = _expand_kv_head(k, kv).astype(BF16)
            vbig[kv, 0:n, :] = _expand_kv_head(v, kv).astype(BF16)

    def mix(n_keys, win_start):
        win0_scr[...] = pad_scr[pl.ds(pl.multiple_of(win_start, V7X_SUBLANES), CONV_WIN), :]
        for res in range(1, V7X_SUBLANES):
            win_scr[res - 1] = win0_scr[res:res + CONV_SPAN, :]
        _attend(q_ref, rows, kbig, vbig, n_keys, attn_scr)
        _conv_block(win0_scr, win_scr, w_ref, y_scr)
        y = _layer_norm(y_scr[...] + b_ref[...], g_ref[...], beta_ref[...])
        conv = (y * jax.nn.sigmoid(y)).astype(BF16)
        mixed = _dot(attn_scr[...], wo_scr[:ATT_Q, :]) + _dot(conv, wo_scr[ATT_Q:, :])
        o_ref[...] = res_ref[...] + mod_ref[0, 5:6, :] * mixed

    @pl.when(i < N_TILES_P)
    def _():
        @pl.when(t == 0)
        def _():
            for s in range(SEQ_PER_TILE):
                base = s * CONV_PAD_SEQ
                pad_scr[base:base + CONV_HALO, :] = border
                pad_scr[base + CONV_HALO:base + CONV_HALO + SEQ, :] = glu_ref[s * SEQ:(s + 1) * SEQ, :]
                pad_scr[base + CONV_HALO + SEQ:base + CONV_PAD_SEQ, :] = border

        expand(k_ref[rows, :], v_ref[rows, :], SEQ)
        mix(SEQ, t * CONV_PAD_SEQ)

    @pl.when(i >= N_TILES_P)
    def _():
        @pl.when(t == 0)
        def _():
            pad_scr[0:CONV_HALO, :] = border
            pad_scr[CONV_HALO:CONV_HALO + DEC_SEQ, :] = glu_ref[...]
            pad_scr[CONV_HALO + DEC_SEQ:2 * CONV_HALO + DEC_SEQ, :] = border
            expand(jnp.concatenate([ck_ref[...], k_ref[...]], axis=0),
                   jnp.concatenate([cv_ref[...], v_ref[...]], axis=0), PAST_LEN + DEC_SEQ)

        mix(PAST_LEN + DEC_SEQ, t * ATT_TQ)


def _even_mix(x, mod_l, q, k, v, ck, cv, glu, conv_w, conv_b, cn_g, cn_b, w_out, j, adaln=None):
    nq = TM // ATT_TQ
    tok = lambda n: pl.BlockSpec((TM, n), lambda i, t: (i, 0))
    blk = pl.BlockSpec((ATT_TQ, D_MODEL), lambda i, t: (i * nq + t, 0))
    ctx = pl.BlockSpec((None, None, PAST_LEN, ATT_KV), lambda i, t: (jnp.maximum(i - N_TILES_P, 0), j, 0, 0))
    vec = pl.BlockSpec((None, 1, CONV_CH), lambda i, t: (j, 0, 0))
    n_even = conv_b.shape[0]
    n_keys = PAST_LEN + DEC_SEQ
    n_out = N_MOD * D_MODEL
    per_layer = n_out // ADALN_LATE_TN
    assert ADALN_LATE_BLOCKS <= N_TILES * nq
    late = lambda s: jnp.minimum(s, ADALN_LATE_BLOCKS - 1)
    extra_in, extra_args, out_specs, out_shape = [], [], blk, jax.ShapeDtypeStruct((NTOK, D_MODEL), F32)
    if adaln is not None:
        cond, w_mod, b_mod = adaln
        extra_in = [
            pl.BlockSpec((N_COND, D_MODEL), lambda i, t: (0, 0)),
            pl.BlockSpec((None, D_MODEL, ADALN_LATE_TN),
                         lambda i, t: (1 + late(i * nq + t) // per_layer, 0, late(i * nq + t) % per_layer)),
            pl.BlockSpec((None, 1, ADALN_LATE_TN),
                         lambda i, t: (1 + late(i * nq + t) // per_layer, 0, late(i * nq + t) % per_layer)),
        ]
        extra_args = [cond, w_mod, b_mod.reshape(DEPTH, 1, n_out)]
        out_specs = [blk, pl.BlockSpec((None, N_COND, ADALN_LATE_TN),
                                       lambda i, t: (late(i * nq + t) // per_layer, 0,
                                                     late(i * nq + t) % per_layer))]
        out_shape = [out_shape, jax.ShapeDtypeStruct((DEPTH - 1, N_COND, n_out), F32)]
    return pl.pallas_call(
        functools.partial(_even_mix_kernel, adaln is not None),
        grid=(N_TILES, nq),
        in_specs=[
            tok(ATT_Q), tok(ATT_KV), tok(ATT_KV), ctx, ctx, tok(CONV_CH),
            pl.BlockSpec((None, CONV_K, CONV_CH), lambda i, t: (j, 0, 0)),
            vec, vec, vec,
            blk,
            pl.BlockSpec((1, N_MOD, D_MODEL), lambda i, t: (_cond_of_tile(i), 0, 0)),
            pl.BlockSpec((None, ATT_Q + CONV_CH, D_MODEL), lambda i, t: (j, 0, 0),
                         pipeline_mode=pl.Buffered(1)),
        ] + extra_in,
        out_specs=out_specs,
        out_shape=out_shape,
        scratch_shapes=[
            pltpu.VMEM((N_KV_HEADS, n_keys, ATT_WIDTH), BF16),
            pltpu.VMEM((N_KV_HEADS, n_keys, ATT_WIDTH), BF16),
            pltpu.VMEM((max(SEQ_PER_TILE * CONV_PAD_SEQ, DEC_SEQ + 2 * CONV_HALO), CONV_CH), F32),
            pltpu.VMEM((CONV_WIN, CONV_CH), F32),
            pltpu.VMEM((V7X_SUBLANES - 1, CONV_SPAN, CONV_CH), F32),
            pltpu.VMEM((ATT_TQ, CONV_CH), F32),
            pltpu.VMEM((ATT_TQ, ATT_Q), BF16),
            pltpu.VMEM((ATT_Q + CONV_CH, D_MODEL), BF16),
        ],
        compiler_params=_cparams("arbitrary", "arbitrary"),
        name="even_mix",
    )(q, k, v, ck, cv, glu, conv_w, conv_b.reshape(n_even, 1, CONV_CH),
      cn_g.reshape(n_even, 1, CONV_CH), cn_b.reshape(n_even, 1, CONV_CH), x, mod_l, w_out, *extra_args)


def _odd_in_kernel(x_ref, mod_ref, g_ref, w_ref, sg_ref, sb_ref, ws_ref, bs_ref, f_ref, sgu_ref, w_scr):
    @pl.when(pl.program_id(0) == 0)
    def _():
        w_scr[...] = w_ref[...].astype(BF16)

    ws = [ws_ref[g].astype(BF16) for g in range(SGU_GROUPS)]
    for r0 in range(0, TM, ODD_IN_PIECE):
        piece = slice(r0, r0 + ODD_IN_PIECE)
        h = _modulate(x_ref[piece, :], g_ref[...], mod_ref[0, 3:4, :], mod_ref[0, 4:5, :]).astype(BF16)

        def proj(lo, hi):
            return _dot(h, w_scr[:, lo:hi])

        f_ref[piece, :] = proj(0, FOURIER_CH).astype(BF16)
        u = proj(FOURIER_CH, FOURIER_CH + SGU_CH)
        v = _layer_norm(proj(FOURIER_CH + SGU_CH, ODD_IN), sg_ref[...], sb_ref[...]).astype(BF16)
        for g in range(SGU_GROUPS):
            bias = bs_ref[:, g:g + 1]
            cols = slice(g * SGU_GROUP_CH, (g + 1) * SGU_GROUP_CH)
            for n in range(ODD_IN_PIECE // CHUNK):
                rows = slice(n * CHUNK, (n + 1) * CHUNK)
                mixed = _dot(ws[g], v[rows, cols]) + bias
                sgu_ref[r0 + n * CHUNK:r0 + (n + 1) * CHUNK, cols] = (u[rows, cols] * mixed).astype(BF16)


def _odd_in(x, mod_l, norm_g, od_w_in, sgu_norm_g, sgu_norm_b, sgu_w, sgu_b, layer, j):
    tok = lambda n: pl.BlockSpec((TM, n), lambda i: (i, 0))
    return pl.pallas_call(
        _odd_in_kernel,
        grid=(N_TILES,),
        in_specs=[
            tok(D_MODEL),
            pl.BlockSpec((1, N_MOD, D_MODEL), lambda i: (_cond_of_tile(i), 0, 0)),
            pl.BlockSpec((None, None, 1, D_MODEL), lambda i: (layer, 1, 0, 0)),
            pl.BlockSpec((None, D_MODEL, ODD_IN), lambda i: (j, 0, 0), pipeline_mode=pl.Buffered(1)),
            pl.BlockSpec((1, SGU_CH), lambda i: (0, 0)),
            pl.BlockSpec((1, SGU_CH), lambda i: (0, 0)),
            pl.BlockSpec((None, SGU_GROUPS, CHUNK, CHUNK), lambda i: (j, 0, 0, 0)),
            pl.BlockSpec((CHUNK, SGU_GROUPS), lambda i: (0, 0)),
        ],
        out_specs=[tok(FOURIER_CH), tok(SGU_CH)],
        out_shape=[jax.ShapeDtypeStruct((NTOK, FOURIER_CH), BF16),
                   jax.ShapeDtypeStruct((NTOK, SGU_CH), BF16)],
        scratch_shapes=[pltpu.VMEM((D_MODEL, ODD_IN), BF16)],
        compiler_params=_cparams("arbitrary"),
        name="odd_in",
    )(x, mod_l, norm_g.reshape(DEPTH, 3, 1, D_MODEL), od_w_in,
      sgu_norm_g[j].reshape(1, SGU_CH), sgu_norm_b[j].reshape(1, SGU_CH), sgu_w, sgu_b[j].T)


def _fourier_out_kernel(f_ref, wch_ref, wseq_p_ref, wseq_s_ref, res_ref, mod_ref, sgu_ref, wo_ref,
                        o_ref, wch_scr, wseq_p_scr, wseq_s_scr, wo_scr):
    i = pl.program_id(0)

    @pl.when(i == 0)
    def _():
        wch_scr[...] = wch_ref[...].astype(BF16)
        wseq_p_scr[...] = wseq_p_ref[...].astype(BF16)
        wseq_s_scr[...] = wseq_s_ref[...].astype(BF16)
        wo_scr[...] = wo_ref[...].astype(BF16)

    gate = mod_ref[0, 5:6, :]

    def channel_dft(rows):
        pq = _dot(f_ref[rows, :], wch_scr[...]).astype(BF16)
        return jnp.concatenate([pq[:, :FOURIER_CH], pq[:, FOURIER_CH:]], axis=0)

    def finish(rows, wseq_rows, stacked, seq):
        scale = 1.0 / np.sqrt(float(seq * FOURIER_GROUP_CH))
        four = (_dot(wseq_rows, stacked) * scale).astype(BF16)
        mixed = _dot(four, wo_scr[:FOURIER_CH, :]) + _dot(sgu_ref[rows, :], wo_scr[FOURIER_CH:, :])
        o_ref[rows, :] = res_ref[rows, :] + gate * mixed

    @pl.when(i < N_TILES_P)
    def _():
        for s in range(SEQ_PER_TILE):
            rows = slice(s * SEQ, (s + 1) * SEQ)
            finish(rows, wseq_p_scr[...], channel_dft(rows), SEQ)

    @pl.when(i >= N_TILES_P)
    def _():
        stacked = channel_dft(slice(0, DEC_SEQ))
        for s in range(SEQ_PER_TILE):
            rows = slice(s * SEQ, (s + 1) * SEQ)
            finish(rows, wseq_s_scr[rows, :], stacked, DEC_SEQ)


def _dft_constants():
    def cos_sin(n):
        idx = np.arange(n)
        ang = 2.0 * np.pi * ((idx[:, None] * idx[None, :]) % n) / n
        return np.cos(ang), np.sin(ang)

    cc, sc = cos_sin(FOURIER_GROUP_CH)
    eye = np.eye(FOURIER_GROUPS)
    wch = np.concatenate([np.kron(eye, cc), np.kron(eye, sc)], axis=1)

    def wseq(seq):
        cl, sl = cos_sin(seq)
        return jnp.asarray(np.concatenate([cl, -sl], axis=1), dtype=F32)

    return jnp.asarray(wch, dtype=F32), wseq(SEQ), wseq(DEC_SEQ)


def _fourier_out(x, mod_l, f, sgu, w_out, j):
    wch, wseq_p, wseq_s = _dft_constants()
    tok = lambda n: pl.BlockSpec((TM, n), lambda i: (i, 0))
    const = lambda shape: pl.BlockSpec(shape, lambda i: (0,) * len(shape), pipeline_mode=pl.Buffered(1))
    return pl.pallas_call(
        _fourier_out_kernel,
        grid=(N_TILES,),
        in_specs=[
            tok(FOURIER_CH), const(wch.shape), const(wseq_p.shape), const(wseq_s.shape),
            tok(D_MODEL),
            pl.BlockSpec((1, N_MOD, D_MODEL), lambda i: (_cond_of_tile(i), 0, 0)),
            tok(SGU_CH),
            pl.BlockSpec((None, FOURIER_CH + SGU_CH, D_MODEL), lambda i: (j, 0, 0),
                         pipeline_mode=pl.Buffered(1)),
        ],
        out_specs=tok(D_MODEL),
        out_shape=jax.ShapeDtypeStruct((NTOK, D_MODEL), F32),
        scratch_shapes=[pltpu.VMEM(wch.shape, BF16),
                        pltpu.VMEM(wseq_p.shape, BF16), pltpu.VMEM(wseq_s.shape, BF16),
                        pltpu.VMEM((FOURIER_CH + SGU_CH, D_MODEL), BF16)],
        compiler_params=_cparams("arbitrary"),
        name="fourier_out",
    )(f, wch, wseq_p, wseq_s, x, mod_l, sgu, w_out)


def kernel(x_prompt, x_sample, cache_k, cache_v, c, c_ctx, w_mod, b_mod, norm_g, ffn_w_in, ffn_w_out, ev_w_in, ev_w_out, q_norm_g, k_norm_g, conv_w, conv_b, conv_norm_g, conv_norm_b, od_w_in, od_w_out, sgu_norm_g, sgu_norm_b, sgu_w, sgu_b):
    cond = jnp.concatenate([c_ctx[None, :], c_ctx[None, :], c,
                            jnp.zeros((N_COND - COND_LATENT0 - DEC_BATCH, D_MODEL), F32)], axis=0)
    mod = _adaln(cond, w_mod, b_mod, 1).reshape(1, N_COND, N_MOD, D_MODEL)
    ck = cache_k.reshape(DEC_BATCH, -1, PAST_LEN, ATT_KV)
    cv = cache_v.reshape(DEC_BATCH, -1, PAST_LEN, ATT_KV)

    ffn = functools.partial(_ffn, norm_g=norm_g, ffn_w_in=ffn_w_in, ffn_w_out=ffn_w_out)
    x = ffn((x_prompt.reshape(NTOK_P, D_MODEL), x_sample.reshape(NTOK_S, D_MODEL)), mod, phases=[(0, 0)])
    new_k, new_v = [], []
    for layer in range(DEPTH):
        mod_l = mod[layer]
        j = layer // 2
        if layer % 2 == 0:
            q, k, v, glu, kc, vc = _even_in(x, mod_l, norm_g, ev_w_in, q_norm_g, k_norm_g, layer, j)
            new_k.append(kc.reshape(BATCH, SEQ, N_KV_HEADS, HEAD_DIM))
            new_v.append(vc.reshape(BATCH, SEQ, N_KV_HEADS, HEAD_DIM))
            mixed = _even_mix(x, mod_l, q, k, v, ck, cv, glu, conv_w, conv_b, conv_norm_g, conv_norm_b,
                              ev_w_out, j, adaln=(cond, w_mod, b_mod) if layer == 0 else None)
            if layer == 0:
                x, mod_late = mixed
                mod = jnp.concatenate([mod, mod_late.reshape(DEPTH - 1, N_COND, N_MOD, D_MODEL)], axis=0)
            else:
                x = mixed
        else:
            f, sgu = _odd_in(x, mod_l, norm_g, od_w_in, sgu_norm_g, sgu_norm_b, sgu_w, sgu_b, layer, j)
            x = _fourier_out(x, mod_l, f, sgu, od_w_out, j)
        if layer < DEPTH - 1:
            x = ffn((x,), mod, phases=[(layer, 1), (layer + 1, 0)])

    last = [(DEPTH - 1, 1)]
    y_prompt = ffn((x,), mod, phases=last, row0=0, n_rows=NTOK_P)
    y_sample = ffn((x,), mod, phases=last, row0=NTOK_P, n_rows=NTOK_S)
    return (y_prompt.reshape(BATCH, SEQ, D_MODEL), y_sample.reshape(DEC_BATCH, DEC_SEQ, D_MODEL),
            jnp.stack(new_k, axis=1), jnp.stack(new_v, axis=1))
```

```python
import functools

import numpy as np
import jax
import jax.numpy as jnp
from jax import lax
from jax.experimental import pallas as pl
from jax.experimental.pallas import tpu as pltpu

D_MODEL = 1024
BATCH = 32
SEQ = 256
DEPTH = 4
DEC_BATCH = 4
DEC_SEQ = 1024
PAST_LEN = 512
GRID_W = 64
N_HEADS = 8
N_KV_HEADS = 2
HEAD_DIM = 64
GQA_GROUP = N_HEADS // N_KV_HEADS
AXIS_ROPE_DIM = HEAD_DIM // 2
ROPE_THETA = 10000.0
ATT_Q = N_HEADS * HEAD_DIM
ATT_KV = N_KV_HEADS * HEAD_DIM
CONV_CH = 512
CONV_K = 31
FOURIER_GROUPS = 4
FOURIER_GROUP_CH = 128
FOURIER_CH = FOURIER_GROUPS * FOURIER_GROUP_CH
SGU_GROUPS = 4
SGU_GROUP_CH = 128
SGU_CH = SGU_GROUPS * SGU_GROUP_CH
CHUNK = 128
FFN_DIM = 2816
N_MOD = 9
EVEN_IN = ATT_Q + 2 * ATT_KV + 2 * CONV_CH
ODD_IN = FOURIER_CH + 2 * SGU_CH
RMS_EPS = 1e-6
LN_EPS = 1e-5
LOG2_E = 1.4426950408889634

BF16 = jnp.bfloat16
F32 = jnp.float32

V7X_SUBLANES = 8
V7X_LANES = 128
NTOK_P = BATCH * SEQ
NTOK_S = DEC_BATCH * DEC_SEQ
NTOK = NTOK_P + NTOK_S
TM = DEC_SEQ
N_TILES = NTOK // TM
N_TILES_P = NTOK_P // TM
SEQ_PER_TILE = TM // SEQ
N_COND = 8
COND_LATENT0 = 2
FFN_TM = 2 * TM
FFN_PIECE = TM
FFN_CHUNK = 256
N_FFN_CHUNKS = FFN_DIM // FFN_CHUNK
EVEN_IN_ROWS = 1024
EVEN_IN_PIECE = 256
ODD_IN_PIECE = 1024
ADALN_TN = 2304
CONV_ROWS = 128
CONV_HALO = 16
ATT_TQ = SEQ
V7X_VMEM_LIMIT_BYTES = 56 * 1024 * 1024


def _cparams(*sem):
    return pltpu.CompilerParams(dimension_semantics=sem, vmem_limit_bytes=V7X_VMEM_LIMIT_BYTES)


def _cond_of_tile(i, rows=TM):
    return jnp.maximum((i * rows - NTOK_P) // DEC_SEQ + COND_LATENT0, 0)


def _dot(a, b):
    return jnp.dot(a, b, preferred_element_type=F32)


def _modulate(x, g, shift, scale):
    ms = jnp.mean(x * x, axis=-1, keepdims=True)
    return (x * lax.rsqrt(ms + RMS_EPS)) * (g * (1.0 + scale)) + shift


def _layer_norm(x, g, b):
    mu = jnp.mean(x, axis=-1, keepdims=True)
    xc = x - mu
    var = jnp.mean(xc * xc, axis=-1, keepdims=True)
    return xc * lax.rsqrt(var + LN_EPS) * g + b


def _adaln_kernel(c_ref, w_ref, b_ref, o_ref):
    c = c_ref[...]
    s = (c * jax.nn.sigmoid(c)).astype(BF16)
    o_ref[...] = _dot(s, w_ref[...].astype(BF16)) + b_ref[...]


def _adaln(cond, w_mod, b_mod):
    n_out = N_MOD * D_MODEL
    return pl.pallas_call(
        _adaln_kernel,
        grid=(DEPTH, n_out // ADALN_TN),
        in_specs=[
            pl.BlockSpec((N_COND, D_MODEL), lambda l, n: (0, 0)),
            pl.BlockSpec((None, D_MODEL, ADALN_TN), lambda l, n: (l, 0, n)),
            pl.BlockSpec((None, 1, ADALN_TN), lambda l, n: (l, 0, n)),
        ],
        out_specs=pl.BlockSpec((None, N_COND, ADALN_TN), lambda l, n: (l, 0, n)),
        out_shape=jax.ShapeDtypeStruct((DEPTH, N_COND, n_out), F32),
        compiler_params=_cparams("parallel", "parallel"),
        name="adaln",
    )(cond, w_mod, b_mod.reshape(DEPTH, 1, n_out))


def _ffn_kernel(subs, n_x, x_split, *refs):
    n_ph = len(subs)
    x_refs = refs[:n_x]
    mod_refs = refs[n_x:n_x + n_ph]
    g_refs = refs[n_x + n_ph:n_x + 2 * n_ph]
    wg_ref, wu_ref, wo_ref, o_ref, h_scr = refs[n_x + 2 * n_ph:]
    i = pl.program_id(0)
    j = pl.program_id(1)
    halves = [(r0 // TM, slice(r0, r0 + FFN_PIECE)) for r0 in range(0, FFN_TM, FFN_PIECE)]

    if n_x == 1:
        @pl.when(j == 0)
        def _():
            o_ref[...] = x_refs[0][...]
    else:
        @pl.when((j == 0) & (i < x_split))
        def _():
            o_ref[...] = x_refs[0][...]

        @pl.when((j == 0) & (i >= x_split))
        def _():
            o_ref[...] = x_refs[1][...]

    def chunk(h, rows, out_gate):
        gate = _dot(h, wg_ref[...].astype(BF16))
        up = _dot(h, wu_ref[...].astype(BF16))
        a = (gate * jax.nn.sigmoid(gate) * up).astype(BF16)
        o_ref[rows, :] += _dot(a, (wo_ref[...] * (0.5 * out_gate)).astype(BF16))

    for p, sub in enumerate(subs):
        @pl.when(j == p * N_FFN_CHUNKS)
        def _(p=p, sub=sub):
            for c, rows in halves:
                h = _modulate(o_ref[rows, :], g_refs[p][...], mod_refs[p][c, 3 * sub:3 * sub + 1, :],
                              mod_refs[p][c, 3 * sub + 1:3 * sub + 2, :]).astype(BF16)
                h_scr[rows, :] = h
                chunk(h, rows, mod_refs[p][c, 3 * sub + 2:3 * sub + 3, :])

    @pl.when(j % N_FFN_CHUNKS != 0)
    def _():
        for c, rows in halves:
            out_gate = mod_refs[0][c, 3 * subs[0] + 2:3 * subs[0] + 3, :]
            if n_ph == 2:
                out_gate = jnp.where(j < N_FFN_CHUNKS, out_gate,
                                     mod_refs[1][c, 3 * subs[1] + 2:3 * subs[1] + 3, :])
            chunk(h_scr[rows, :], rows, out_gate)


def _ffn(xs, mod, norm_g, ffn_w_in, ffn_w_out, phases, row0=0, n_rows=NTOK):
    per = FFN_TM // TM
    tile0 = row0 // FFN_TM
    n_ph = len(phases)
    layers = [p[0] for p in phases]
    whichs = [p[1] for p in phases]

    def by_phase(vals):
        if n_ph == 1:
            return lambda j: vals[0]
        return lambda j: vals[0] + (j // N_FFN_CHUNKS) * (vals[1] - vals[0])

    layer_of, which_of = by_phase(layers), by_phase(whichs)
    chunk_of = lambda j: j % N_FFN_CHUNKS
    if len(xs) == 1:
        x_specs = [pl.BlockSpec((FFN_TM, D_MODEL), lambda i, j: (tile0 + i, 0))]
        x_split = 0
    else:
        x_split = xs[0].shape[0] // FFN_TM
        x_specs = [
            pl.BlockSpec((FFN_TM, D_MODEL), lambda i, j: (jnp.minimum(i, x_split - 1), 0)),
            pl.BlockSpec((FFN_TM, D_MODEL), lambda i, j: (jnp.maximum(i - x_split, 0), 0),
                         pipeline_mode=pl.Buffered(1)),
        ]
    mod_specs = [pl.BlockSpec((None, per, N_MOD, D_MODEL),
                              lambda i, j, l=l: (l, _cond_of_tile(tile0 + i, FFN_TM) // per, 0, 0))
                 for l in layers]
    g_specs = [pl.BlockSpec((None, None, 1, D_MODEL), lambda i, j, l=l, w=w: (l, 2 * w, 0, 0))
               for l, w in phases]
    return pl.pallas_call(
        functools.partial(_ffn_kernel, tuple(2 * w for w in whichs), len(xs), x_split),
        grid=(n_rows // FFN_TM, n_ph * N_FFN_CHUNKS),
        in_specs=x_specs + mod_specs + g_specs + [
            pl.BlockSpec((None, None, D_MODEL, FFN_CHUNK),
                         lambda i, j: (layer_of(j), which_of(j), 0, chunk_of(j))),
            pl.BlockSpec((None, None, D_MODEL, FFN_CHUNK),
                         lambda i, j: (layer_of(j), which_of(j), 0, N_FFN_CHUNKS + chunk_of(j))),
            pl.BlockSpec((None, None, FFN_CHUNK, D_MODEL),
                         lambda i, j: (layer_of(j), which_of(j), chunk_of(j), 0)),
        ],
        out_specs=pl.BlockSpec((FFN_TM, D_MODEL), lambda i, j: (i, 0)),
        out_shape=jax.ShapeDtypeStruct((n_rows, D_MODEL), F32),
        scratch_shapes=[pltpu.VMEM((FFN_TM, D_MODEL), BF16)],
        compiler_params=_cparams("parallel", "arbitrary"),
        name="ffn",
    )(*xs, *([mod] * n_ph), *([norm_g.reshape(DEPTH, 3, 1, D_MODEL)] * n_ph),
      ffn_w_in, ffn_w_in, ffn_w_out)


def _group_mean_sq(x, width):
    r = lax.broadcasted_iota(jnp.int32, (width, width), 0) // HEAD_DIM
    c = lax.broadcasted_iota(jnp.int32, (width, width), 1) // HEAD_DIM
    ones = (r == c).astype(BF16)
    return _dot((x * x).astype(BF16), ones) * (1.0 / HEAD_DIM)


def _head_norm_rope(x, gain, cos, sin_next, sin_prev):
    width = x.shape[-1]
    xn = x * lax.rsqrt(_group_mean_sq(x, width) + RMS_EPS) * gain
    nxt = pltpu.roll(xn, width - 1, 1)
    prv = pltpu.roll(xn, 1, 1)
    return xn * cos + nxt * sin_next + prv * sin_prev


def _even_in_kernel(x_ref, mod_ref, g_ref, w_ref, qg_ref, kg_ref,
                    cq_ref, snq_ref, spq_ref, ck_ref, snk_ref, spk_ref,
                    q_ref, k_ref, v_ref, glu_ref, kc_ref, vc_ref, w_scr):
    i = pl.program_id(0)

    @pl.when(i == 0)
    def _():
        w_scr[...] = w_ref[...].astype(BF16)

    a0 = ATT_Q + 2 * ATT_KV
    for r0 in range(0, EVEN_IN_ROWS, EVEN_IN_PIECE):
        rows = slice(r0, r0 + EVEN_IN_PIECE)
        h = _modulate(x_ref[rows, :], g_ref[...], mod_ref[0, 3:4, :], mod_ref[0, 4:5, :]).astype(BF16)

        def proj(lo, hi):
            return _dot(h, w_scr[:, lo:hi])

        q = _head_norm_rope(proj(0, ATT_Q), qg_ref[...], cq_ref[rows, :], snq_ref[rows, :], spq_ref[rows, :])
        q_ref[rows, :] = (q * (HEAD_DIM ** -0.5 * LOG2_E)).astype(BF16)
        k_ref[rows, :] = _head_norm_rope(proj(ATT_Q, ATT_Q + ATT_KV), kg_ref[...],
                                         ck_ref[rows, :], snk_ref[rows, :], spk_ref[rows, :])
        v_ref[rows, :] = proj(ATT_Q + ATT_KV, a0)
        glu_ref[rows, :] = proj(a0, a0 + CONV_CH) * jax.nn.sigmoid(proj(a0 + CONV_CH, a0 + 2 * CONV_CH))

    @pl.when(i < NTOK_P // EVEN_IN_ROWS)
    def _():
        kc_ref[...] = k_ref[...]
        vc_ref[...] = v_ref[...]


def _rope_tables():
    t = np.arange(DEC_SEQ)
    row = (t // GRID_W).astype(np.float64)
    col = (t % GRID_W).astype(np.float64)
    inv_freq = 1.0 / (ROPE_THETA ** (np.arange(0, AXIS_ROPE_DIM, 2, dtype=np.float64) / AXIS_ROPE_DIM))
    ang = np.concatenate([row[:, None] * inv_freq, col[:, None] * inv_freq], axis=-1)
    ang = np.repeat(ang, 2, axis=-1)
    even = (np.arange(HEAD_DIM) % 2 == 0)[None, :]
    cos = np.cos(ang)
    sin_next = np.where(even, -np.sin(ang), 0.0)
    sin_prev = np.where(even, 0.0, np.sin(ang))

    def both(tab):
        ident = np.ones_like(tab) if tab is cos else np.zeros_like(tab)
        return np.stack([np.tile(ident, (1, N_HEADS)), np.tile(tab, (1, N_HEADS))]).astype(np.float32)

    return both(cos), both(sin_next), both(sin_prev)


def _even_in(x, mod_l, norm_g, ev_w_in, q_norm_g, k_norm_g, layer, j):
    cos, sin_next, sin_prev = (jnp.asarray(t) for t in _rope_tables())
    qg = jnp.tile(q_norm_g[j], N_HEADS).reshape(1, ATT_Q)
    kg = jnp.tile(k_norm_g[j], N_KV_HEADS).reshape(1, ATT_KV)
    rows = EVEN_IN_ROWS
    per_seq = DEC_SEQ // rows
    first_latent = NTOK_P // rows
    tab_idx = lambda i: (jnp.minimum(i // first_latent, 1), i % per_seq, 0)
    tab_q = pl.BlockSpec((None, rows, ATT_Q), tab_idx)
    tab_k = pl.BlockSpec((None, rows, ATT_KV), tab_idx)
    tok = lambda n: pl.BlockSpec((rows, n), lambda i: (i, 0))
    cache = pl.BlockSpec((rows, ATT_KV), lambda i: (jnp.minimum(i, first_latent - 1), 0))
    return pl.pallas_call(
        _even_in_kernel,
        grid=(NTOK // rows,),
        in_specs=[
            tok(D_MODEL),
            pl.BlockSpec((1, N_MOD, D_MODEL), lambda i: (_cond_of_tile(i, rows), 0, 0)),
            pl.BlockSpec((None, None, 1, D_MODEL), lambda i: (layer, 1, 0, 0)),
            pl.BlockSpec((None, D_MODEL, EVEN_IN), lambda i: (j, 0, 0), pipeline_mode=pl.Buffered(1)),
            pl.BlockSpec((1, ATT_Q), lambda i: (0, 0)),
            pl.BlockSpec((1, ATT_KV), lambda i: (0, 0)),
            tab_q, tab_q, tab_q, tab_k, tab_k, tab_k,
        ],
        out_specs=[tok(ATT_Q), tok(ATT_KV), tok(ATT_KV), tok(CONV_CH), cache, cache],
        out_shape=[
            jax.ShapeDtypeStruct((NTOK, ATT_Q), BF16),
            jax.ShapeDtypeStruct((NTOK, ATT_KV), F32),
            jax.ShapeDtypeStruct((NTOK, ATT_KV), F32),
            jax.ShapeDtypeStruct((NTOK, CONV_CH), F32),
            jax.ShapeDtypeStruct((NTOK_P, ATT_KV), F32),
            jax.ShapeDtypeStruct((NTOK_P, ATT_KV), F32),
        ],
        scratch_shapes=[pltpu.VMEM((D_MODEL, EVEN_IN), BF16)],
        compiler_params=_cparams("arbitrary"),
        name="even_in",
    )(x, mod_l, norm_g.reshape(DEPTH, 3, 1, D_MODEL), ev_w_in, qg, kg,
      cos, sin_next, sin_prev, cos, sin_next, sin_prev)


ATT_WIDTH = GQA_GROUP * HEAD_DIM


def _expand_kv_head(x, kv):
    lane = lax.broadcasted_iota(jnp.int32, x.shape, 1)
    swapped = pltpu.roll(x, HEAD_DIM, 1)
    pair = jnp.where((lane // HEAD_DIM) == kv, x, swapped)
    return jnp.concatenate([pair, pair], axis=1)


def _attend(q_ref, q_rows, kbig, vbig, key0, n_keys, attn_scr, out_rows):
    lane_group = lax.broadcasted_iota(jnp.int32, (ATT_TQ, ATT_WIDTH), 1) // HEAD_DIM
    for kv in range(N_KV_HEADS):
        cols = slice(kv * ATT_WIDTH, (kv + 1) * ATT_WIDTH)
        q = q_ref[q_rows, cols]
        kb = kbig[kv, key0:key0 + n_keys, :]
        vb = vbig[kv, key0:key0 + n_keys, :]
        acc = jnp.zeros((ATT_TQ, ATT_WIDTH), F32)
        for g in range(GQA_GROUP):
            mine = lane_group == g
            qm = jnp.where(mine, q, jnp.zeros_like(q))
            s = lax.dot_general(qm, kb, (((1,), (1,)), ((), ())), preferred_element_type=F32)
            m = jnp.max(s, axis=-1, keepdims=True)
            p = jnp.exp2(s - m)
            denom = jnp.sum(p, axis=-1, keepdims=True)
            o = _dot(p.astype(BF16), vb)
            acc = jnp.where(mine, o / denom, acc)
        attn_scr[out_rows, cols] = acc.astype(BF16)


MIX_BLOCKS = 2
CONV_PAD_SEQ = SEQ + 2 * CONV_HALO
CONV_WIN = ATT_TQ + 2 * CONV_HALO
CONV_SPAN = CONV_WIN - V7X_SUBLANES


def _conv_block(win0_scr, win_scr, w_ref, y_scr):
    off = CONV_HALO - CONV_K // 2
    for r0 in range(0, ATT_TQ, CONV_ROWS):
        for c0 in range(0, CONV_CH, V7X_LANES):
            cols = slice(c0, c0 + V7X_LANES)
            acc = jnp.zeros((CONV_ROWS, V7X_LANES), F32)
            for k in range(CONV_K):
                res = (off + k) % V7X_SUBLANES
                a = r0 + off + k - res
                if res == 0:
                    tap = win0_scr[a:a + CONV_ROWS, cols]
                else:
                    tap = win_scr[res - 1, a:a + CONV_ROWS, cols]
                acc = acc + tap * w_ref[k:k + 1, cols]
            y_scr[r0:r0 + CONV_ROWS, cols] = acc


def _even_mix_kernel(q_ref, k_ref, v_ref, ck_ref, cv_ref, glu_ref, w_ref, b_ref, g_ref, beta_ref,
                     res_ref, mod_ref, wo_ref, o_ref,
                     kbig, vbig, pad_scr, win0_scr, win_scr, y_scr, attn_scr, wo_scr):
    i = pl.program_id(0)
    t = pl.program_id(1)
    border = jnp.zeros((CONV_HALO, CONV_CH), F32)

    @pl.when((i == 0) & (t == 0))
    def _():
        wo_scr[...] = wo_ref[...].astype(BF16)

    def q_rows(b):
        return pl.ds(pl.multiple_of((t * MIX_BLOCKS + b) * ATT_TQ, ATT_TQ), ATT_TQ)

    def expand(k, v, key0, n):
        for kv in range(N_KV_HEADS):
            kbig[kv, key0:key0 + n, :] = _expand_kv_head(k, kv).astype(BF16)
            vbig[kv, key0:key0 + n, :] = _expand_kv_head(v, kv).astype(BF16)

    def mix(b, key0, n_keys, win_start):
        out_rows = slice(b * ATT_TQ, (b + 1) * ATT_TQ)
        win0, win, y = win0_scr.at[b], win_scr.at[b], y_scr.at[b]
        win0[...] = pad_scr[pl.ds(pl.multiple_of(win_start, V7X_SUBLANES), CONV_WIN), :]
        for res in range(1, V7X_SUBLANES):
            win[res - 1] = win0[res:res + CONV_SPAN, :]
        _attend(q_ref, q_rows(b), kbig, vbig, key0, n_keys, attn_scr, out_rows)
        _conv_block(win0, win, w_ref, y)
        yn = _layer_norm(y[...] + b_ref[...], g_ref[...], beta_ref[...])
        conv = (yn * jax.nn.sigmoid(yn)).astype(BF16)
        mixed = _dot(attn_scr[out_rows, :], wo_scr[:ATT_Q, :]) + _dot(conv, wo_scr[ATT_Q:, :])
        o_ref[out_rows, :] = res_ref[out_rows, :] + mod_ref[0, 5:6, :] * mixed

    @pl.when(i < N_TILES_P)
    def _():
        @pl.when(t == 0)
        def _():
            for s in range(SEQ_PER_TILE):
                base = s * CONV_PAD_SEQ
                pad_scr[base:base + CONV_HALO, :] = border
                pad_scr[base + CONV_HALO:base + CONV_HALO + SEQ, :] = glu_ref[s * SEQ:(s + 1) * SEQ, :]
                pad_scr[base + CONV_HALO + SEQ:base + CONV_PAD_SEQ, :] = border

        for b in range(MIX_BLOCKS):
            expand(k_ref[q_rows(b), :], v_ref[q_rows(b), :], b * SEQ, SEQ)
        for b in range(MIX_BLOCKS):
            mix(b, b * SEQ, SEQ, (t * MIX_BLOCKS + b) * CONV_PAD_SEQ)

    @pl.when(i >= N_TILES_P)
    def _():
        @pl.when(t == 0)
        def _():
            pad_scr[0:CONV_HALO, :] = border
            pad_scr[CONV_HALO:CONV_HALO + DEC_SEQ, :] = glu_ref[...]
            pad_scr[CONV_HALO + DEC_SEQ:2 * CONV_HALO + DEC_SEQ, :] = border
            expand(jnp.concatenate([ck_ref[...], k_ref[...]], axis=0),
                   jnp.concatenate([cv_ref[...], v_ref[...]], axis=0), 0, PAST_LEN + DEC_SEQ)

        for b in range(MIX_BLOCKS):
            mix(b, 0, PAST_LEN + DEC_SEQ, (t * MIX_BLOCKS + b) * ATT_TQ)


def _even_mix(x, mod_l, q, k, v, ck, cv, glu, conv_w, conv_b, cn_g, cn_b, w_out, j):
    rows = MIX_BLOCKS * ATT_TQ
    nq = TM // rows
    tok = lambda n: pl.BlockSpec((TM, n), lambda i, t: (i, 0))
    blk = pl.BlockSpec((rows, D_MODEL), lambda i, t: (i * nq + t, 0))
    ctx = pl.BlockSpec((None, None, PAST_LEN, ATT_KV), lambda i, t: (jnp.maximum(i - N_TILES_P, 0), j, 0, 0))
    vec = pl.BlockSpec((None, 1, CONV_CH), lambda i, t: (j, 0, 0))
    n_even = conv_b.shape[0]
    n_keys = PAST_LEN + DEC_SEQ
    return pl.pallas_call(
        _even_mix_kernel,
        grid=(N_TILES, nq),
        in_specs=[
            tok(ATT_Q), tok(ATT_KV), tok(ATT_KV), ctx, ctx, tok(CONV_CH),
            pl.BlockSpec((None, CONV_K, CONV_CH), lambda i, t: (j, 0, 0)),
            vec, vec, vec,
            blk,
            pl.BlockSpec((1, N_MOD, D_MODEL), lambda i, t: (_cond_of_tile(i), 0, 0)),
            pl.BlockSpec((None, ATT_Q + CONV_CH, D_MODEL), lambda i, t: (j, 0, 0),
                         pipeline_mode=pl.Buffered(1)),
        ],
        out_specs=blk,
        out_shape=jax.ShapeDtypeStruct((NTOK, D_MODEL), F32),
        scratch_shapes=[
            pltpu.VMEM((N_KV_HEADS, n_keys, ATT_WIDTH), BF16),
            pltpu.VMEM((N_KV_HEADS, n_keys, ATT_WIDTH), BF16),
            pltpu.VMEM((max(SEQ_PER_TILE * CONV_PAD_SEQ, DEC_SEQ + 2 * CONV_HALO), CONV_CH), F32),
            pltpu.VMEM((MIX_BLOCKS, CONV_WIN, CONV_CH), F32),
            pltpu.VMEM((MIX_BLOCKS, V7X_SUBLANES - 1, CONV_SPAN, CONV_CH), F32),
            pltpu.VMEM((MIX_BLOCKS, ATT_TQ, CONV_CH), F32),
            pltpu.VMEM((rows, ATT_Q), BF16),
            pltpu.VMEM((ATT_Q + CONV_CH, D_MODEL), BF16),
        ],
        compiler_params=_cparams("arbitrary", "arbitrary"),
        name="even_mix",
    )(q, k, v, ck, cv, glu, conv_w, conv_b.reshape(n_even, 1, CONV_CH),
      cn_g.reshape(n_even, 1, CONV_CH), cn_b.reshape(n_even, 1, CONV_CH), x, mod_l, w_out)


def _odd_in_kernel(x_ref, mod_ref, g_ref, w_ref, sg_ref, sb_ref, ws_ref, bs_ref, f_ref, sgu_ref, w_scr):
    @pl.when(pl.program_id(0) == 0)
    def _():
        w_scr[...] = w_ref[...].astype(BF16)

    ws = [ws_ref[g].astype(BF16) for g in range(SGU_GROUPS)]
    for r0 in range(0, TM, ODD_IN_PIECE):
        piece = slice(r0, r0 + ODD_IN_PIECE)
        h = _modulate(x_ref[piece, :], g_ref[...], mod_ref[0, 3:4, :], mod_ref[0, 4:5, :]).astype(BF16)

        def proj(lo, hi):
            return _dot(h, w_scr[:, lo:hi])

        f_ref[piece, :] = proj(0, FOURIER_CH).astype(BF16)
        u = proj(FOURIER_CH, FOURIER_CH + SGU_CH)
        v = _layer_norm(proj(FOURIER_CH + SGU_CH, ODD_IN), sg_ref[...], sb_ref[...]).astype(BF16)
        for g in range(SGU_GROUPS):
            bias = bs_ref[:, g:g + 1]
            cols = slice(g * SGU_GROUP_CH, (g + 1) * SGU_GROUP_CH)
            for n in range(ODD_IN_PIECE // CHUNK):
                rows = slice(n * CHUNK, (n + 1) * CHUNK)
                mixed = _dot(ws[g], v[rows, cols]) + bias
                sgu_ref[r0 + n * CHUNK:r0 + (n + 1) * CHUNK, cols] = (u[rows, cols] * mixed).astype(BF16)


def _odd_in(x, mod_l, norm_g, od_w_in, sgu_norm_g, sgu_norm_b, sgu_w, sgu_b, layer, j):
    tok = lambda n: pl.BlockSpec((TM, n), lambda i: (i, 0))
    return pl.pallas_call(
        _odd_in_kernel,
        grid=(N_TILES,),
        in_specs=[
            tok(D_MODEL),
            pl.BlockSpec((1, N_MOD, D_MODEL), lambda i: (_cond_of_tile(i), 0, 0)),
            pl.BlockSpec((None, None, 1, D_MODEL), lambda i: (layer, 1, 0, 0)),
            pl.BlockSpec((None, D_MODEL, ODD_IN), lambda i: (j, 0, 0), pipeline_mode=pl.Buffered(1)),
            pl.BlockSpec((1, SGU_CH), lambda i: (0, 0)),
            pl.BlockSpec((1, SGU_CH), lambda i: (0, 0)),
            pl.BlockSpec((None, SGU_GROUPS, CHUNK, CHUNK), lambda i: (j, 0, 0, 0)),
            pl.BlockSpec((CHUNK, SGU_GROUPS), lambda i: (0, 0)),
        ],
        out_specs=[tok(FOURIER_CH), tok(SGU_CH)],
        out_shape=[jax.ShapeDtypeStruct((NTOK, FOURIER_CH), BF16),
                   jax.ShapeDtypeStruct((NTOK, SGU_CH), BF16)],
        scratch_shapes=[pltpu.VMEM((D_MODEL, ODD_IN), BF16)],
        compiler_params=_cparams("arbitrary"),
        name="odd_in",
    )(x, mod_l, norm_g.reshape(DEPTH, 3, 1, D_MODEL), od_w_in,
      sgu_norm_g[j].reshape(1, SGU_CH), sgu_norm_b[j].reshape(1, SGU_CH), sgu_w, sgu_b[j].T)


def _fourier_out_kernel(f_ref, wch_ref, wseq_p_ref, wseq_s_ref, res_ref, mod_ref, sgu_ref, wo_ref,
                        o_ref, wch_scr, wseq_p_scr, wseq_s_scr, wo_scr):
    i = pl.program_id(0)

    @pl.when(i == 0)
    def _():
        wch_scr[...] = wch_ref[...].astype(BF16)
        wseq_p_scr[...] = wseq_p_ref[...].astype(BF16)
        wseq_s_scr[...] = wseq_s_ref[...].astype(BF16)
        wo_scr[...] = wo_ref[...].astype(BF16)

    gate = mod_ref[0, 5:6, :]

    def channel_dft(rows):
        pq = _dot(f_ref[rows, :], wch_scr[...]).astype(BF16)
        return jnp.concatenate([pq[:, :FOURIER_CH], pq[:, FOURIER_CH:]], axis=0)

    def finish(rows, wseq_rows, stacked, seq):
        scale = 1.0 / np.sqrt(float(seq * FOURIER_GROUP_CH))
        four = (_dot(wseq_rows, stacked) * scale).astype(BF16)
        mixed = _dot(four, wo_scr[:FOURIER_CH, :]) + _dot(sgu_ref[rows, :], wo_scr[FOURIER_CH:, :])
        o_ref[rows, :] = res_ref[rows, :] + gate * mixed

    @pl.when(i < N_TILES_P)
    def _():
        for s in range(SEQ_PER_TILE):
            rows = slice(s * SEQ, (s + 1) * SEQ)
            finish(rows, wseq_p_scr[...], channel_dft(rows), SEQ)

    @pl.when(i >= N_TILES_P)
    def _():
        stacked = channel_dft(slice(0, DEC_SEQ))
        for s in range(SEQ_PER_TILE):
            rows = slice(s * SEQ, (s + 1) * SEQ)
            finish(rows, wseq_s_scr[rows, :], stacked, DEC_SEQ)


def _dft_constants():
    def cos_sin(n):
        idx = np.arange(n)
        ang = 2.0 * np.pi * ((idx[:, None] * idx[None, :]) % n) / n
        return np.cos(ang), np.sin(ang)

    cc, sc = cos_sin(FOURIER_GROUP_CH)
    eye = np.eye(FOURIER_GROUPS)
    wch = np.concatenate([np.kron(eye, cc), np.kron(eye, sc)], axis=1)

    def wseq(seq):
        cl, sl = cos_sin(seq)
        return jnp.asarray(np.concatenate([cl, -sl], axis=1), dtype=F32)

    return jnp.asarray(wch, dtype=F32), wseq(SEQ), wseq(DEC_SEQ)


def _fourier_out(x, mod_l, f, sgu, w_out, j):
    wch, wseq_p, wseq_s = _dft_constants()
    tok = lambda n: pl.BlockSpec((TM, n), lambda i: (i, 0))
    const = lambda shape: pl.BlockSpec(shape, lambda i: (0,) * len(shape), pipeline_mode=pl.Buffered(1))
    return pl.pallas_call(
        _fourier_out_kernel,
        grid=(N_TILES,),
        in_specs=[
            tok(FOURIER_CH), const(wch.shape), const(wseq_p.shape), const(wseq_s.shape),
            tok(D_MODEL),
            pl.BlockSpec((1, N_MOD, D_MODEL), lambda i: (_cond_of_tile(i), 0, 0)),
            tok(SGU_CH),
            pl.BlockSpec((None, FOURIER_CH + SGU_CH, D_MODEL), lambda i: (j, 0, 0),
                         pipeline_mode=pl.Buffered(1)),
        ],
        out_specs=tok(D_MODEL),
        out_shape=jax.ShapeDtypeStruct((NTOK, D_MODEL), F32),
        scratch_shapes=[pltpu.VMEM(wch.shape, BF16),
                        pltpu.VMEM(wseq_p.shape, BF16), pltpu.VMEM(wseq_s.shape, BF16),
                        pltpu.VMEM((FOURIER_CH + SGU_CH, D_MODEL), BF16)],
        compiler_params=_cparams("arbitrary"),
        name="fourier_out",
    )(f, wch, wseq_p, wseq_s, x, mod_l, sgu, w_out)


def kernel(x_prompt, x_sample, cache_k, cache_v, c, c_ctx, w_mod, b_mod, norm_g, ffn_w_in, ffn_w_out, ev_w_in, ev_w_out, q_norm_g, k_norm_g, conv_w, conv_b, conv_norm_g, conv_norm_b, od_w_in, od_w_out, sgu_norm_g, sgu_norm_b, sgu_w, sgu_b):
    cond = jnp.concatenate([c_ctx[None, :], c_ctx[None, :], c,
                            jnp.zeros((N_COND - COND_LATENT0 - DEC_BATCH, D_MODEL), F32)], axis=0)
    mod = _adaln(cond, w_mod, b_mod).reshape(DEPTH, N_COND, N_MOD, D_MODEL)
    ck = cache_k.reshape(DEC_BATCH, -1, PAST_LEN, ATT_KV)
    cv = cache_v.reshape(DEC_BATCH, -1, PAST_LEN, ATT_KV)

    ffn = functools.partial(_ffn, mod=mod, norm_g=norm_g, ffn_w_in=ffn_w_in, ffn_w_out=ffn_w_out)
    x = ffn((x_prompt.reshape(NTOK_P, D_MODEL), x_sample.reshape(NTOK_S, D_MODEL)), phases=[(0, 0)])
    new_k, new_v = [], []
    for layer in range(DEPTH):
        mod_l = mod[layer]
        j = layer // 2
        if layer % 2 == 0:
            q, k, v, glu, kc, vc = _even_in(x, mod_l, norm_g, ev_w_in, q_norm_g, k_norm_g, layer, j)
            new_k.append(kc.reshape(BATCH, SEQ, N_KV_HEADS, HEAD_DIM))
            new_v.append(vc.reshape(BATCH, SEQ, N_KV_HEADS, HEAD_DIM))
            x = _even_mix(x, mod_l, q, k, v, ck, cv, glu, conv_w, conv_b, conv_norm_g, conv_norm_b,
                          ev_w_out, j)
        else:
            f, sgu = _odd_in(x, mod_l, norm_g, od_w_in, sgu_norm_g, sgu_norm_b, sgu_w, sgu_b, layer, j)
            x = _fourier_out(x, mod_l, f, sgu, od_w_out, j)
        if layer < DEPTH - 1:
            x = ffn((x,), phases=[(layer, 1), (layer + 1, 0)])

    last = [(DEPTH - 1, 1)]
    y_prompt = ffn((x,), phases=last, row0=0, n_rows=NTOK_P)
    y_sample = ffn((x,), phases=last, row0=NTOK_P, n_rows=NTOK_S)
    return (y_prompt.reshape(BATCH, SEQ, D_MODEL), y_sample.reshape(DEC_BATCH, DEC_SEQ, D_MODEL),
            jnp.stack(new_k, axis=1), jnp.stack(new_v, axis=1))
```

```python
import functools

import numpy as np
import jax
import jax.numpy as jnp
from jax import lax
from jax.experimental import pallas as pl
from jax.experimental.pallas import tpu as pltpu

D_MODEL = 1024
BATCH = 32
SEQ = 256
DEPTH = 4
DEC_BATCH = 4
DEC_SEQ = 1024
PAST_LEN = 512
GRID_W = 64
N_HEADS = 8
N_KV_HEADS = 2
HEAD_DIM = 64
GQA_GROUP = N_HEADS // N_KV_HEADS
AXIS_ROPE_DIM = HEAD_DIM // 2
ROPE_THETA = 10000.0
ATT_Q = N_HEADS * HEAD_DIM
ATT_KV = N_KV_HEADS * HEAD_DIM
CONV_CH = 512
CONV_K = 31
FOURIER_GROUPS = 4
FOURIER_GROUP_CH = 128
FOURIER_CH = FOURIER_GROUPS * FOURIER_GROUP_CH
SGU_GROUPS = 4
SGU_GROUP_CH = 128
SGU_CH = SGU_GROUPS * SGU_GROUP_CH
CHUNK = 128
FFN_DIM = 2816
N_MOD = 9
EVEN_IN = ATT_Q + 2 * ATT_KV + 2 * CONV_CH
ODD_IN = FOURIER_CH + 2 * SGU_CH
RMS_EPS = 1e-6
LN_EPS = 1e-5
LOG2_E = 1.4426950408889634

BF16 = jnp.bfloat16
F32 = jnp.float32

V7X_SUBLANES = 8
V7X_LANES = 128
NTOK_P = BATCH * SEQ
NTOK_S = DEC_BATCH * DEC_SEQ
NTOK = NTOK_P + NTOK_S
TM = DEC_SEQ
N_TILES = NTOK // TM
N_TILES_P = NTOK_P // TM
SEQ_PER_TILE = TM // SEQ
N_COND = 8
COND_LATENT0 = 2
FFN_TM = 2 * TM
FFN_PIECE = TM
FFN_CHUNK = 256
N_FFN_CHUNKS = FFN_DIM // FFN_CHUNK
FFN_WEIGHT_BUFS = 3
EVEN_IN_ROWS = 1024
EVEN_IN_PIECE = 256
ODD_IN_PIECE = 1024
ADALN_TN = 2304
CONV_ROWS = 128
CONV_HALO = 16
ATT_TQ = SEQ
V7X_VMEM_LIMIT_BYTES = 56 * 1024 * 1024


def _cparams(*sem):
    return pltpu.CompilerParams(dimension_semantics=sem, vmem_limit_bytes=V7X_VMEM_LIMIT_BYTES)


def _cond_of_tile(i, rows=TM):
    return jnp.maximum((i * rows - NTOK_P) // DEC_SEQ + COND_LATENT0, 0)


def _dot(a, b):
    return jnp.dot(a, b, preferred_element_type=F32)


def _modulate(x, g, shift, scale):
    ms = jnp.mean(x * x, axis=-1, keepdims=True)
    return (x * lax.rsqrt(ms + RMS_EPS)) * (g * (1.0 + scale)) + shift


def _layer_norm(x, g, b):
    mu = jnp.mean(x, axis=-1, keepdims=True)
    xc = x - mu
    var = jnp.mean(xc * xc, axis=-1, keepdims=True)
    return xc * lax.rsqrt(var + LN_EPS) * g + b


def _adaln_kernel(c_ref, w_ref, b_ref, o_ref):
    c = c_ref[...]
    s = (c * jax.nn.sigmoid(c)).astype(BF16)
    o_ref[...] = _dot(s, w_ref[...].astype(BF16)) + b_ref[...]


def _adaln(cond, w_mod, b_mod):
    n_out = N_MOD * D_MODEL
    return pl.pallas_call(
        _adaln_kernel,
        grid=(DEPTH, n_out // ADALN_TN),
        in_specs=[
            pl.BlockSpec((N_COND, D_MODEL), lambda l, n: (0, 0)),
            pl.BlockSpec((None, D_MODEL, ADALN_TN), lambda l, n: (l, 0, n)),
            pl.BlockSpec((None, 1, ADALN_TN), lambda l, n: (l, 0, n)),
        ],
        out_specs=pl.BlockSpec((None, N_COND, ADALN_TN), lambda l, n: (l, 0, n)),
        out_shape=jax.ShapeDtypeStruct((DEPTH, N_COND, n_out), F32),
        compiler_params=_cparams("parallel", "parallel"),
        name="adaln",
    )(cond, w_mod, b_mod.reshape(DEPTH, 1, n_out))


def _ffn_kernel(phases, n_x, x_split, n_tiles, *refs):
    n_ph = len(phases)
    subs = tuple(2 * w for _, w in phases)
    x_refs = refs[:n_x]
    mod_refs = refs[n_x:n_x + n_ph]
    g_refs = refs[n_x + n_ph:n_x + 2 * n_ph]
    w_in_hbm, w_out_hbm, o_ref, h_scr, wg_buf, wu_buf, wo_buf, sems = refs[n_x + 2 * n_ph:]
    i = pl.program_id(0)
    j = pl.program_id(1)
    n_steps = n_ph * N_FFN_CHUNKS
    total = n_tiles * n_steps
    step = i * n_steps + j
    slot = step % FFN_WEIGHT_BUFS
    halves = [(r0 // TM, slice(r0, r0 + FFN_PIECE)) for r0 in range(0, FFN_TM, FFN_PIECE)]

    def weight_copies(s):
        js = s % n_steps
        ph = js // N_FFN_CHUNKS
        layer = phases[0][0] + ph * (phases[-1][0] - phases[0][0])
        which = phases[0][1] + ph * (phases[-1][1] - phases[0][1])
        col = pl.multiple_of((js % N_FFN_CHUNKS) * FFN_CHUNK, FFN_CHUNK)
        b = s % FFN_WEIGHT_BUFS
        return (
            pltpu.make_async_copy(w_in_hbm.at[layer, which, :, pl.ds(col, FFN_CHUNK)],
                                  wg_buf.at[b], sems.at[b, 0]),
            pltpu.make_async_copy(w_in_hbm.at[layer, which, :, pl.ds(FFN_DIM + col, FFN_CHUNK)],
                                  wu_buf.at[b], sems.at[b, 1]),
            pltpu.make_async_copy(w_out_hbm.at[layer, which, pl.ds(col, FFN_CHUNK), :],
                                  wo_buf.at[b], sems.at[b, 2]),
        )

    @pl.when(step == 0)
    def _():
        for s0 in range(min(FFN_WEIGHT_BUFS - 1, total)):
            for cp in weight_copies(jnp.int32(s0)):
                cp.start()

    for cp in weight_copies(step):
        cp.wait()

    @pl.when(step + (FFN_WEIGHT_BUFS - 1) < total)
    def _():
        for cp in weight_copies(step + (FFN_WEIGHT_BUFS - 1)):
            cp.start()

    wg_ref, wu_ref, wo_ref = wg_buf.at[slot], wu_buf.at[slot], wo_buf.at[slot]

    if n_x == 1:
        @pl.when(j == 0)
        def _():
            o_ref[...] = x_refs[0][...]
    else:
        @pl.when((j == 0) & (i < x_split))
        def _():
            o_ref[...] = x_refs[0][...]

        @pl.when((j == 0) & (i >= x_split))
        def _():
            o_ref[...] = x_refs[1][...]

    def chunk(h, rows, out_gate):
        gate = _dot(h, wg_ref[...].astype(BF16))
        up = _dot(h, wu_ref[...].astype(BF16))
        a = (gate * jax.nn.sigmoid(gate) * up).astype(BF16)
        o_ref[rows, :] += _dot(a, (wo_ref[...] * (0.5 * out_gate)).astype(BF16))

    for p, sub in enumerate(subs):
        @pl.when(j == p * N_FFN_CHUNKS)
        def _(p=p, sub=sub):
            for c, rows in halves:
                h = _modulate(o_ref[rows, :], g_refs[p][...], mod_refs[p][c, 3 * sub:3 * sub + 1, :],
                              mod_refs[p][c, 3 * sub + 1:3 * sub + 2, :]).astype(BF16)
                h_scr[rows, :] = h
                chunk(h, rows, mod_refs[p][c, 3 * sub + 2:3 * sub + 3, :])

    @pl.when(j % N_FFN_CHUNKS != 0)
    def _():
        for c, rows in halves:
            out_gate = mod_refs[0][c, 3 * subs[0] + 2:3 * subs[0] + 3, :]
            if n_ph == 2:
                out_gate = jnp.where(j < N_FFN_CHUNKS, out_gate,
                                     mod_refs[1][c, 3 * subs[1] + 2:3 * subs[1] + 3, :])
            chunk(h_scr[rows, :], rows, out_gate)


def _ffn(xs, mod, norm_g, ffn_w_in, ffn_w_out, phases, row0=0, n_rows=NTOK):
    per = FFN_TM // TM
    tile0 = row0 // FFN_TM
    n_ph = len(phases)
    n_tiles = n_rows // FFN_TM
    layers = [p[0] for p in phases]
    if len(xs) == 1:
        x_specs = [pl.BlockSpec((FFN_TM, D_MODEL), lambda i, j: (tile0 + i, 0))]
        x_split = 0
    else:
        x_split = xs[0].shape[0] // FFN_TM
        x_specs = [
            pl.BlockSpec((FFN_TM, D_MODEL), lambda i, j: (jnp.minimum(i, x_split - 1), 0)),
            pl.BlockSpec((FFN_TM, D_MODEL), lambda i, j: (jnp.maximum(i - x_split, 0), 0),
                         pipeline_mode=pl.Buffered(1)),
        ]
    mod_specs = [pl.BlockSpec((None, per, N_MOD, D_MODEL),
                              lambda i, j, l=l: (l, _cond_of_tile(tile0 + i, FFN_TM) // per, 0, 0))
                 for l in layers]
    g_specs = [pl.BlockSpec((None, None, 1, D_MODEL), lambda i, j, l=l, w=w: (l, 2 * w, 0, 0))
               for l, w in phases]
    return pl.pallas_call(
        functools.partial(_ffn_kernel, tuple(phases), len(xs), x_split, n_tiles),
        grid=(n_tiles, n_ph * N_FFN_CHUNKS),
        in_specs=x_specs + mod_specs + g_specs + [
            pl.BlockSpec(memory_space=pl.ANY),
            pl.BlockSpec(memory_space=pl.ANY),
        ],
        out_specs=pl.BlockSpec((FFN_TM, D_MODEL), lambda i, j: (i, 0)),
        out_shape=jax.ShapeDtypeStruct((n_rows, D_MODEL), F32),
        scratch_shapes=[
            pltpu.VMEM((FFN_TM, D_MODEL), BF16),
            pltpu.VMEM((FFN_WEIGHT_BUFS, D_MODEL, FFN_CHUNK), F32),
            pltpu.VMEM((FFN_WEIGHT_BUFS, D_MODEL, FFN_CHUNK), F32),
            pltpu.VMEM((FFN_WEIGHT_BUFS, FFN_CHUNK, D_MODEL), F32),
            pltpu.SemaphoreType.DMA((FFN_WEIGHT_BUFS, 3)),
        ],
        compiler_params=_cparams("arbitrary", "arbitrary"),
        name="ffn",
    )(*xs, *([mod] * n_ph), *([norm_g.reshape(DEPTH, 3, 1, D_MODEL)] * n_ph), ffn_w_in, ffn_w_out)


def _group_mean_sq(x, width):
    r = lax.broadcasted_iota(jnp.int32, (width, width), 0) // HEAD_DIM
    c = lax.broadcasted_iota(jnp.int32, (width, width), 1) // HEAD_DIM
    ones = (r == c).astype(BF16)
    return _dot((x * x).astype(BF16), ones) * (1.0 / HEAD_DIM)


def _head_norm_rope(x, gain, cos, sin_next, sin_prev):
    width = x.shape[-1]
    xn = x * lax.rsqrt(_group_mean_sq(x, width) + RMS_EPS) * gain
    nxt = pltpu.roll(xn, width - 1, 1)
    prv = pltpu.roll(xn, 1, 1)
    return xn * cos + nxt * sin_next + prv * sin_prev


def _even_in_kernel(x_ref, mod_ref, g_ref, w_ref, qg_ref, kg_ref,
                    cq_ref, snq_ref, spq_ref, ck_ref, snk_ref, spk_ref,
                    q_ref, k_ref, v_ref, glu_ref, kc_ref, vc_ref, w_scr):
    i = pl.program_id(0)

    @pl.when(i == 0)
    def _():
        w_scr[...] = w_ref[...].astype(BF16)

    a0 = ATT_Q + 2 * ATT_KV
    for r0 in range(0, EVEN_IN_ROWS, EVEN_IN_PIECE):
        rows = slice(r0, r0 + EVEN_IN_PIECE)
        h = _modulate(x_ref[rows, :], g_ref[...], mod_ref[0, 3:4, :], mod_ref[0, 4:5, :]).astype(BF16)

        def proj(lo, hi):
            return _dot(h, w_scr[:, lo:hi])

        q = _head_norm_rope(proj(0, ATT_Q), qg_ref[...], cq_ref[rows, :], snq_ref[rows, :], spq_ref[rows, :])
        q_ref[rows, :] = (q * (HEAD_DIM ** -0.5 * LOG2_E)).astype(BF16)
        k_ref[rows, :] = _head_norm_rope(proj(ATT_Q, ATT_Q + ATT_KV), kg_ref[...],
                                         ck_ref[rows, :], snk_ref[rows, :], spk_ref[rows, :])
        v_ref[rows, :] = proj(ATT_Q + ATT_KV, a0)
        glu_ref[rows, :] = proj(a0, a0 + CONV_CH) * jax.nn.sigmoid(proj(a0 + CONV_CH, a0 + 2 * CONV_CH))

    @pl.when(i < NTOK_P // EVEN_IN_ROWS)
    def _():
        kc_ref[...] = k_ref[...]
        vc_ref[...] = v_ref[...]


def _rope_tables():
    t = np.arange(DEC_SEQ)
    row = (t // GRID_W).astype(np.float64)
    col = (t % GRID_W).astype(np.float64)
    inv_freq = 1.0 / (ROPE_THETA ** (np.arange(0, AXIS_ROPE_DIM, 2, dtype=np.float64) / AXIS_ROPE_DIM))
    ang = np.concatenate([row[:, None] * inv_freq, col[:, None] * inv_freq], axis=-1)
    ang = np.repeat(ang, 2, axis=-1)
    even = (np.arange(HEAD_DIM) % 2 == 0)[None, :]
    cos = np.cos(ang)
    sin_next = np.where(even, -np.sin(ang), 0.0)
    sin_prev = np.where(even, 0.0, np.sin(ang))

    def both(tab):
        ident = np.ones_like(tab) if tab is cos else np.zeros_like(tab)
        return np.stack([np.tile(ident, (1, N_HEADS)), np.tile(tab, (1, N_HEADS))]).astype(np.float32)

    return both(cos), both(sin_next), both(sin_prev)


def _even_in(x, mod_l, norm_g, ev_w_in, q_norm_g, k_norm_g, layer, j):
    cos, sin_next, sin_prev = (jnp.asarray(t) for t in _rope_tables())
    qg = jnp.tile(q_norm_g[j], N_HEADS).reshape(1, ATT_Q)
    kg = jnp.tile(k_norm_g[j], N_KV_HEADS).reshape(1, ATT_KV)
    rows = EVEN_IN_ROWS
    per_seq = DEC_SEQ // rows
    first_latent = NTOK_P // rows
    tab_idx = lambda i: (jnp.minimum(i // first_latent, 1), i % per_seq, 0)
    tab_q = pl.BlockSpec((None, rows, ATT_Q), tab_idx)
    tab_k = pl.BlockSpec((None, rows, ATT_KV), tab_idx)
    tok = lambda n: pl.BlockSpec((rows, n), lambda i: (i, 0))
    cache = pl.BlockSpec((rows, ATT_KV), lambda i: (jnp.minimum(i, first_latent - 1), 0))
    return pl.pallas_call(
        _even_in_kernel,
        grid=(NTOK // rows,),
        in_specs=[
            tok(D_MODEL),
            pl.BlockSpec((1, N_MOD, D_MODEL), lambda i: (_cond_of_tile(i, rows), 0, 0)),
            pl.BlockSpec((None, None, 1, D_MODEL), lambda i: (layer, 1, 0, 0)),
            pl.BlockSpec((None, D_MODEL, EVEN_IN), lambda i: (j, 0, 0), pipeline_mode=pl.Buffered(1)),
            pl.BlockSpec((1, ATT_Q), lambda i: (0, 0)),
            pl.BlockSpec((1, ATT_KV), lambda i: (0, 0)),
            tab_q, tab_q, tab_q, tab_k, tab_k, tab_k,
        ],
        out_specs=[tok(ATT_Q), tok(ATT_KV), tok(ATT_KV), tok(CONV_CH), cache, cache],
        out_shape=[
            jax.ShapeDtypeStruct((NTOK, ATT_Q), BF16),
            jax.ShapeDtypeStruct((NTOK, ATT_KV), F32),
            jax.ShapeDtypeStruct((NTOK, ATT_KV), F32),
            jax.ShapeDtypeStruct((NTOK, CONV_CH), F32),
            jax.ShapeDtypeStruct((NTOK_P, ATT_KV), F32),
            jax.ShapeDtypeStruct((NTOK_P, ATT_KV), F32),
        ],
        scratch_shapes=[pltpu.VMEM((D_MODEL, EVEN_IN), BF16)],
        compiler_params=_cparams("arbitrary"),
        name="even_in",
    )(x, mod_l, norm_g.reshape(DEPTH, 3, 1, D_MODEL), ev_w_in, qg, kg,
      cos, sin_next, sin_prev, cos, sin_next, sin_prev)


ATT_WIDTH = GQA_GROUP * HEAD_DIM


def _expand_kv_head(x, kv):
    lane = lax.broadcasted_iota(jnp.int32, x.shape, 1)
    swapped = pltpu.roll(x, HEAD_DIM, 1)
    pair = jnp.where((lane // HEAD_DIM) == kv, x, swapped)
    return jnp.concatenate([pair, pair], axis=1)


def _attend(q_ref, q_rows, kbig, vbig, key0, n_keys, attn_scr, out_rows):
    lane_group = lax.broadcasted_iota(jnp.int32, (ATT_TQ, ATT_WIDTH), 1) // HEAD_DIM
    for kv in range(N_KV_HEADS):
        cols = slice(kv * ATT_WIDTH, (kv + 1) * ATT_WIDTH)
        q = q_ref[q_rows, cols]
        kb = kbig[kv, key0:key0 + n_keys, :]
        vb = vbig[kv, key0:key0 + n_keys, :]
        acc = jnp.zeros((ATT_TQ, ATT_WIDTH), F32)
        for g in range(GQA_GROUP):
            mine = lane_group == g
            qm = jnp.where(mine, q, jnp.zeros_like(q))
            s = lax.dot_general(qm, kb, (((1,), (1,)), ((), ())), preferred_element_type=F32)
            m = jnp.max(s, axis=-1, keepdims=True)
            p = jnp.exp2(s - m)
            denom = jnp.sum(p, axis=-1, keepdims=True)
            o = _dot(p.astype(BF16), vb)
            acc = jnp.where(mine, o / denom, acc)
        attn_scr[out_rows, cols] = acc.astype(BF16)


MIX_BLOCKS = 2
CONV_PAD_SEQ = SEQ + 2 * CONV_HALO
CONV_WIN = ATT_TQ + 2 * CONV_HALO
CONV_SPAN = CONV_WIN - V7X_SUBLANES


def _conv_block(win0_scr, win_scr, w_ref, y_scr):
    off = CONV_HALO - CONV_K // 2
    for r0 in range(0, ATT_TQ, CONV_ROWS):
        for c0 in range(0, CONV_CH, V7X_LANES):
            cols = slice(c0, c0 + V7X_LANES)
            acc = jnp.zeros((CONV_ROWS, V7X_LANES), F32)
            for k in range(CONV_K):
                res = (off + k) % V7X_SUBLANES
                a = r0 + off + k - res
                if res == 0:
                    tap = win0_scr[a:a + CONV_ROWS, cols]
                else:
                    tap = win_scr[res - 1, a:a + CONV_ROWS, cols]
                acc = acc + tap * w_ref[k:k + 1, cols]
            y_scr[r0:r0 + CONV_ROWS, cols] = acc


def _even_mix_kernel(q_ref, k_ref, v_ref, ck_ref, cv_ref, glu_ref, w_ref, b_ref, g_ref, beta_ref,
                     res_ref, mod_ref, wo_ref, o_ref,
                     kbig, vbig, pad_scr, win0_scr, win_scr, y_scr, attn_scr, wo_scr):
    i = pl.program_id(0)
    t = pl.program_id(1)
    border = jnp.zeros((CONV_HALO, CONV_CH), F32)

    @pl.when((i == 0) & (t == 0))
    def _():
        wo_scr[...] = wo_ref[...].astype(BF16)

    def q_rows(b):
        return pl.ds(pl.multiple_of((t * MIX_BLOCKS + b) * ATT_TQ, ATT_TQ), ATT_TQ)

    def expand(k, v, key0, n):
        for kv in range(N_KV_HEADS):
            kbig[kv, key0:key0 + n, :] = _expand_kv_head(k, kv).astype(BF16)
            vbig[kv, key0:key0 + n, :] = _expand_kv_head(v, kv).astype(BF16)

    def mix(b, key0, n_keys, win_start):
        out_rows = slice(b * ATT_TQ, (b + 1) * ATT_TQ)
        win0, win, y = win0_scr.at[b], win_scr.at[b], y_scr.at[b]
        win0[...] = pad_scr[pl.ds(pl.multiple_of(win_start, V7X_SUBLANES), CONV_WIN), :]
        for res in range(1, V7X_SUBLANES):
            win[res - 1] = win0[res:res + CONV_SPAN, :]
        _attend(q_ref, q_rows(b), kbig, vbig, key0, n_keys, attn_scr, out_rows)
        _conv_block(win0, win, w_ref, y)
        yn = _layer_norm(y[...] + b_ref[...], g_ref[...], beta_ref[...])
        conv = (yn * jax.nn.sigmoid(yn)).astype(BF16)
        mixed = _dot(attn_scr[out_rows, :], wo_scr[:ATT_Q, :]) + _dot(conv, wo_scr[ATT_Q:, :])
        o_ref[out_rows, :] = res_ref[out_rows, :] + mod_ref[0, 5:6, :] * mixed

    @pl.when(i < N_TILES_P)
    def _():
        @pl.when(t == 0)
        def _():
            for s in range(SEQ_PER_TILE):
                base = s * CONV_PAD_SEQ
                pad_scr[base:base + CONV_HALO, :] = border
                pad_scr[base + CONV_HALO:base + CONV_HALO + SEQ, :] = glu_ref[s * SEQ:(s + 1) * SEQ, :]
                pad_scr[base + CONV_HALO + SEQ:base + CONV_PAD_SEQ, :] = border

        for b in range(MIX_BLOCKS):
            expand(k_ref[q_rows(b), :], v_ref[q_rows(b), :], b * SEQ, SEQ)
        for b in range(MIX_BLOCKS):
            mix(b, b * SEQ, SEQ, (t * MIX_BLOCKS + b) * CONV_PAD_SEQ)

    @pl.when(i >= N_TILES_P)
    def _():
        @pl.when(t == 0)
        def _():
            pad_scr[0:CONV_HALO, :] = border
            pad_scr[CONV_HALO:CONV_HALO + DEC_SEQ, :] = glu_ref[...]
            pad_scr[CONV_HALO + DEC_SEQ:2 * CONV_HALO + DEC_SEQ, :] = border
            expand(jnp.concatenate([ck_ref[...], k_ref[...]], axis=0),
                   jnp.concatenate([cv_ref[...], v_ref[...]], axis=0), 0, PAST_LEN + DEC_SEQ)

        for b in range(MIX_BLOCKS):
            mix(b, 0, PAST_LEN + DEC_SEQ, (t * MIX_BLOCKS + b) * ATT_TQ)


def _even_mix(x, mod_l, q, k, v, ck, cv, glu, conv_w, conv_b, cn_g, cn_b, w_out, j):
    rows = MIX_BLOCKS * ATT_TQ
    nq = TM // rows
    tok = lambda n: pl.BlockSpec((TM, n), lambda i, t: (i, 0))
    blk = pl.BlockSpec((rows, D_MODEL), lambda i, t: (i * nq + t, 0))
    ctx = pl.BlockSpec((None, None, PAST_LEN, ATT_KV), lambda i, t: (jnp.maximum(i - N_TILES_P, 0), j, 0, 0))
    vec = pl.BlockSpec((None, 1, CONV_CH), lambda i, t: (j, 0, 0))
    n_even = conv_b.shape[0]
    n_keys = PAST_LEN + DEC_SEQ
    return pl.pallas_call(
        _even_mix_kernel,
        grid=(N_TILES, nq),
        in_specs=[
            tok(ATT_Q), tok(ATT_KV), tok(ATT_KV), ctx, ctx, tok(CONV_CH),
            pl.BlockSpec((None, CONV_K, CONV_CH), lambda i, t: (j, 0, 0)),
            vec, vec, vec,
            blk,
            pl.BlockSpec((1, N_MOD, D_MODEL), lambda i, t: (_cond_of_tile(i), 0, 0)),
            pl.BlockSpec((None, ATT_Q + CONV_CH, D_MODEL), lambda i, t: (j, 0, 0),
                         pipeline_mode=pl.Buffered(1)),
        ],
        out_specs=blk,
        out_shape=jax.ShapeDtypeStruct((NTOK, D_MODEL), F32),
        scratch_shapes=[
            pltpu.VMEM((N_KV_HEADS, n_keys, ATT_WIDTH), BF16),
            pltpu.VMEM((N_KV_HEADS, n_keys, ATT_WIDTH), BF16),
            pltpu.VMEM((max(SEQ_PER_TILE * CONV_PAD_SEQ, DEC_SEQ + 2 * CONV_HALO), CONV_CH), F32),
            pltpu.VMEM((MIX_BLOCKS, CONV_WIN, CONV_CH), F32),
            pltpu.VMEM((MIX_BLOCKS, V7X_SUBLANES - 1, CONV_SPAN, CONV_CH), F32),
            pltpu.VMEM((MIX_BLOCKS, ATT_TQ, CONV_CH), F32),
            pltpu.VMEM((rows, ATT_Q), BF16),
            pltpu.VMEM((ATT_Q + CONV_CH, D_MODEL), BF16),
        ],
        compiler_params=_cparams("arbitrary", "arbitrary"),
        name="even_mix",
    )(q, k, v, ck, cv, glu, conv_w, conv_b.reshape(n_even, 1, CONV_CH),
      cn_g.reshape(n_even, 1, CONV_CH), cn_b.reshape(n_even, 1, CONV_CH), x, mod_l, w_out)


def _odd_in_kernel(x_ref, mod_ref, g_ref, w_ref, sg_ref, sb_ref, ws_ref, bs_ref, f_ref, sgu_ref, w_scr):
    @pl.when(pl.program_id(0) == 0)
    def _():
        w_scr[...] = w_ref[...].astype(BF16)

    ws = [ws_ref[g].astype(BF16) for g in range(SGU_GROUPS)]
    for r0 in range(0, TM, ODD_IN_PIECE):
        piece = slice(r0, r0 + ODD_IN_PIECE)
        h = _modulate(x_ref[piece, :], g_ref[...], mod_ref[0, 3:4, :], mod_ref[0, 4:5, :]).astype(BF16)

        def proj(lo, hi):
            return _dot(h, w_scr[:, lo:hi])

        f_ref[piece, :] = proj(0, FOURIER_CH).astype(BF16)
        u = proj(FOURIER_CH, FOURIER_CH + SGU_CH)
        v = _layer_norm(proj(FOURIER_CH + SGU_CH, ODD_IN), sg_ref[...], sb_ref[...]).astype(BF16)
        for g in range(SGU_GROUPS):
            bias = bs_ref[:, g:g + 1]
            cols = slice(g * SGU_GROUP_CH, (g + 1) * SGU_GROUP_CH)
            for n in range(ODD_IN_PIECE // CHUNK):
                rows = slice(n * CHUNK, (n + 1) * CHUNK)
                mixed = _dot(ws[g], v[rows, cols]) + bias
                sgu_ref[r0 + n * CHUNK:r0 + (n + 1) * CHUNK, cols] = (u[rows, cols] * mixed).astype(BF16)


def _odd_in(x, mod_l, norm_g, od_w_in, sgu_norm_g, sgu_norm_b, sgu_w, sgu_b, layer, j):
    tok = lambda n: pl.BlockSpec((TM, n), lambda i: (i, 0))
    return pl.pallas_call(
        _odd_in_kernel,
        grid=(N_TILES,),
        in_specs=[
            tok(D_MODEL),
            pl.BlockSpec((1, N_MOD, D_MODEL), lambda i: (_cond_of_tile(i), 0, 0)),
            pl.BlockSpec((None, None, 1, D_MODEL), lambda i: (layer, 1, 0, 0)),
            pl.BlockSpec((None, D_MODEL, ODD_IN), lambda i: (j, 0, 0), pipeline_mode=pl.Buffered(1)),
            pl.BlockSpec((1, SGU_CH), lambda i: (0, 0)),
            pl.BlockSpec((1, SGU_CH), lambda i: (0, 0)),
            pl.BlockSpec((None, SGU_GROUPS, CHUNK, CHUNK), lambda i: (j, 0, 0, 0)),
            pl.BlockSpec((CHUNK, SGU_GROUPS), lambda i: (0, 0)),
        ],
        out_specs=[tok(FOURIER_CH), tok(SGU_CH)],
        out_shape=[jax.ShapeDtypeStruct((NTOK, FOURIER_CH), BF16),
                   jax.ShapeDtypeStruct((NTOK, SGU_CH), BF16)],
        scratch_shapes=[pltpu.VMEM((D_MODEL, ODD_IN), BF16)],
        compiler_params=_cparams("arbitrary"),
        name="odd_in",
    )(x, mod_l, norm_g.reshape(DEPTH, 3, 1, D_MODEL), od_w_in,
      sgu_norm_g[j].reshape(1, SGU_CH), sgu_norm_b[j].reshape(1, SGU_CH), sgu_w, sgu_b[j].T)


def _fourier_out_kernel(f_ref, wch_ref, wseq_p_ref, wseq_s_ref, res_ref, mod_ref, sgu_ref, wo_ref,
                        o_ref, wch_scr, wseq_p_scr, wseq_s_scr, wo_scr):
    i = pl.program_id(0)

    @pl.when(i == 0)
    def _():
        wch_scr[...] = wch_ref[...].astype(BF16)
        wseq_p_scr[...] = wseq_p_ref[...].astype(BF16)
        wseq_s_scr[...] = wseq_s_ref[...].astype(BF16)
        wo_scr[...] = wo_ref[...].astype(BF16)

    gate = mod_ref[0, 5:6, :]

    def channel_dft(rows):
        pq = _dot(f_ref[rows, :], wch_scr[...]).astype(BF16)
        return jnp.concatenate([pq[:, :FOURIER_CH], pq[:, FOURIER_CH:]], axis=0)

    def finish(rows, wseq_rows, stacked, seq):
        scale = 1.0 / np.sqrt(float(seq * FOURIER_GROUP_CH))
        four = (_dot(wseq_rows, stacked) * scale).astype(BF16)
        mixed = _dot(four, wo_scr[:FOURIER_CH, :]) + _dot(sgu_ref[rows, :], wo_scr[FOURIER_CH:, :])
        o_ref[rows, :] = res_ref[rows, :] + gate * mixed

    @pl.when(i < N_TILES_P)
    def _():
        for s in range(SEQ_PER_TILE):
            rows = slice(s * SEQ, (s + 1) * SEQ)
            finish(rows, wseq_p_scr[...], channel_dft(rows), SEQ)

    @pl.when(i >= N_TILES_P)
    def _():
        stacked = channel_dft(slice(0, DEC_SEQ))
        for s in range(SEQ_PER_TILE):
            rows = slice(s * SEQ, (s + 1) * SEQ)
            finish(rows, wseq_s_scr[rows, :], stacked, DEC_SEQ)


def _dft_constants():
    def cos_sin(n):
        idx = np.arange(n)
        ang = 2.0 * np.pi * ((idx[:, None] * idx[None, :]) % n) / n
        return np.cos(ang), np.sin(ang)

    cc, sc = cos_sin(FOURIER_GROUP_CH)
    eye = np.eye(FOURIER_GROUPS)
    wch = np.concatenate([np.kron(eye, cc), np.kron(eye, sc)], axis=1)

    def wseq(seq):
        cl, sl = cos_sin(seq)
        return jnp.asarray(np.concatenate([cl, -sl], axis=1), dtype=F32)

    return jnp.asarray(wch, dtype=F32), wseq(SEQ), wseq(DEC_SEQ)


def _fourier_out(x, mod_l, f, sgu, w_out, j):
    wch, wseq_p, wseq_s = _dft_constants()
    tok = lambda n: pl.BlockSpec((TM, n), lambda i: (i, 0))
    const = lambda shape: pl.BlockSpec(shape, lambda i: (0,) * len(shape), pipeline_mode=pl.Buffered(1))
    return pl.pallas_call(
        _fourier_out_kernel,
        grid=(N_TILES,),
        in_specs=[
            tok(FOURIER_CH), const(wch.shape), const(wseq_p.shape), const(wseq_s.shape),
            tok(D_MODEL),
            pl.BlockSpec((1, N_MOD, D_MODEL), lambda i: (_cond_of_tile(i), 0, 0)),
            tok(SGU_CH),
            pl.BlockSpec((None, FOURIER_CH + SGU_CH, D_MODEL), lambda i: (j, 0, 0),
                         pipeline_mode=pl.Buffered(1)),
        ],
        out_specs=tok(D_MODEL),
        out_shape=jax.ShapeDtypeStruct((NTOK, D_MODEL), F32),
        scratch_shapes=[pltpu.VMEM(wch.shape, BF16),
                        pltpu.VMEM(wseq_p.shape, BF16), pltpu.VMEM(wseq_s.shape, BF16),
                        pltpu.VMEM((FOURIER_CH + SGU_CH, D_MODEL), BF16)],
        compiler_params=_cparams("arbitrary"),
        name="fourier_out",
    )(f, wch, wseq_p, wseq_s, x, mod_l, sgu, w_out)


def kernel(x_prompt, x_sample, cache_k, cache_v, c, c_ctx, w_mod, b_mod, norm_g, ffn_w_in, ffn_w_out, ev_w_in, ev_w_out, q_norm_g, k_norm_g, conv_w, conv_b, conv_norm_g, conv_norm_b, od_w_in, od_w_out, sgu_norm_g, sgu_norm_b, sgu_w, sgu_b):
    cond = jnp.concatenate([c_ctx[None, :], c_ctx[None, :], c,
                            jnp.zeros((N_COND - COND_LATENT0 - DEC_BATCH, D_MODEL), F32)], axis=0)
    mod = _adaln(cond, w_mod, b_mod).reshape(DEPTH, N_COND, N_MOD, D_MODEL)
    ck = cache_k.reshape(DEC_BATCH, -1, PAST_LEN, ATT_KV)
    cv = cache_v.reshape(DEC_BATCH, -1, PAST_LEN, ATT_KV)

    ffn = functools.partial(_ffn, mod=mod, norm_g=norm_g, ffn_w_in=ffn_w_in, ffn_w_out=ffn_w_out)
    x = ffn((x_prompt.reshape(NTOK_P, D_MODEL), x_sample.reshape(NTOK_S, D_MODEL)), phases=[(0, 0)])
    new_k, new_v = [], []
    for layer in range(DEPTH):
        mod_l = mod[layer]
        j = layer // 2
        if layer % 2 == 0:
            q, k, v, glu, kc, vc = _even_in(x, mod_l, norm_g, ev_w_in, q_norm_g, k_norm_g, layer, j)
            new_k.append(kc.reshape(BATCH, SEQ, N_KV_HEADS, HEAD_DIM))
            new_v.append(vc.reshape(BATCH, SEQ, N_KV_HEADS, HEAD_DIM))
            x = _even_mix(x, mod_l, q, k, v, ck, cv, glu, conv_w, conv_b, conv_norm_g, conv_norm_b,
                          ev_w_out, j)
        else:
            f, sgu = _odd_in(x, mod_l, norm_g, od_w_in, sgu_norm_g, sgu_norm_b, sgu_w, sgu_b, layer, j)
            x = _fourier_out(x, mod_l, f, sgu, od_w_out, j)
        if layer < DEPTH - 1:
            x = ffn((x,), phases=[(layer, 1), (layer + 1, 0)])

    last = [(DEPTH - 1, 1)]
    y_prompt = ffn((x,), phases=last, row0=0, n_rows=NTOK_P)
    y_sample = ffn((x,), phases=last, row0=NTOK_P, n_rows=NTOK_S)
    return (y_prompt.reshape(BATCH, SEQ, D_MODEL), y_sample.reshape(DEC_BATCH, DEC_SEQ, D_MODEL),
            jnp.stack(new_k, axis=1), jnp.stack(new_v, axis=1))
```

```python
import functools

import numpy as np
import jax
import jax.numpy as jnp
from jax import lax
from jax.experimental import pallas as pl
from jax.experimental.pallas import tpu as pltpu

D_MODEL = 1024
BATCH = 32
SEQ = 256
DEPTH = 4
DEC_BATCH = 4
DEC_SEQ = 1024
PAST_LEN = 512
GRID_W = 64
N_HEADS = 8
N_KV_HEADS = 2
HEAD_DIM = 64
GQA_GROUP = N_HEADS // N_KV_HEADS
AXIS_ROPE_DIM = HEAD_DIM // 2
ROPE_THETA = 10000.0
ATT_Q = N_HEADS * HEAD_DIM
ATT_KV = N_KV_HEADS * HEAD_DIM
CONV_CH = 512
CONV_K = 31
FOURIER_GROUPS = 4
FOURIER_GROUP_CH = 128
FOURIER_CH = FOURIER_GROUPS * FOURIER_GROUP_CH
SGU_GROUPS = 4
SGU_GROUP_CH = 128
SGU_CH = SGU_GROUPS * SGU_GROUP_CH
CHUNK = 128
FFN_DIM = 2816
N_MOD = 9
EVEN_IN = ATT_Q + 2 * ATT_KV + 2 * CONV_CH
ODD_IN = FOURIER_CH + 2 * SGU_CH
RMS_EPS = 1e-6
LN_EPS = 1e-5
LOG2_E = 1.4426950408889634

BF16 = jnp.bfloat16
F32 = jnp.float32

V7X_SUBLANES = 8
V7X_LANES = 128
NTOK_P = BATCH * SEQ
NTOK_S = DEC_BATCH * DEC_SEQ
NTOK = NTOK_P + NTOK_S
TM = DEC_SEQ
N_TILES = NTOK // TM
N_TILES_P = NTOK_P // TM
SEQ_PER_TILE = TM // SEQ
N_COND = 8
COND_LATENT0 = 2
FFN_TM = 2 * TM
FFN_PIECE = TM
FFN_CHUNK = 256
N_FFN_CHUNKS = FFN_DIM // FFN_CHUNK
EVEN_IN_ROWS = 1024
EVEN_IN_PIECE = 256
ODD_IN_PIECE = 1024
ADALN_TN = 2304
CONV_ROWS = 128
CONV_HALO = 16
ATT_TQ = SEQ
V7X_VMEM_LIMIT_BYTES = 56 * 1024 * 1024


def _cparams(*sem):
    return pltpu.CompilerParams(dimension_semantics=sem, vmem_limit_bytes=V7X_VMEM_LIMIT_BYTES)


def _cond_of_tile(i, rows=TM):
    return jnp.maximum((i * rows - NTOK_P) // DEC_SEQ + COND_LATENT0, 0)


def _dot(a, b):
    return jnp.dot(a, b, preferred_element_type=F32)


def _modulate(x, g, shift, scale):
    ms = jnp.mean(x * x, axis=-1, keepdims=True)
    return (x * lax.rsqrt(ms + RMS_EPS)) * (g * (1.0 + scale)) + shift


def _layer_norm(x, g, b):
    mu = jnp.mean(x, axis=-1, keepdims=True)
    xc = x - mu
    var = jnp.mean(xc * xc, axis=-1, keepdims=True)
    return xc * lax.rsqrt(var + LN_EPS) * g + b


def _adaln_kernel(c_ref, w_ref, b_ref, o_ref):
    c = c_ref[...]
    s = (c * jax.nn.sigmoid(c)).astype(BF16)
    o_ref[...] = _dot(s, w_ref[...].astype(BF16)) + b_ref[...]


def _adaln(cond, w_mod, b_mod):
    n_out = N_MOD * D_MODEL
    return pl.pallas_call(
        _adaln_kernel,
        grid=(DEPTH, n_out // ADALN_TN),
        in_specs=[
            pl.BlockSpec((N_COND, D_MODEL), lambda l, n: (0, 0)),
            pl.BlockSpec((None, D_MODEL, ADALN_TN), lambda l, n: (l, 0, n)),
            pl.BlockSpec((None, 1, ADALN_TN), lambda l, n: (l, 0, n)),
        ],
        out_specs=pl.BlockSpec((None, N_COND, ADALN_TN), lambda l, n: (l, 0, n)),
        out_shape=jax.ShapeDtypeStruct((DEPTH, N_COND, n_out), F32),
        compiler_params=_cparams("parallel", "parallel"),
        name="adaln",
    )(cond, w_mod, b_mod.reshape(DEPTH, 1, n_out))


def _ffn_kernel(subs, n_x, x_split, *refs):
    n_ph = len(subs)
    x_refs = refs[:n_x]
    mod_refs = refs[n_x:n_x + n_ph]
    g_refs = refs[n_x + n_ph:n_x + 2 * n_ph]
    wg_ref, wu_ref, wo_ref, o_ref, h_scr = refs[n_x + 2 * n_ph:]
    i = pl.program_id(0)
    j = pl.program_id(1)
    halves = [(r0 // TM, slice(r0, r0 + FFN_PIECE)) for r0 in range(0, FFN_TM, FFN_PIECE)]

    def chunk(h, rows, out_gate, base=None):
        gate = _dot(h, wg_ref[...].astype(BF16))
        up = _dot(h, wu_ref[...].astype(BF16))
        a = (gate * jax.nn.sigmoid(gate) * up).astype(BF16)
        delta = _dot(a, (wo_ref[...] * (0.5 * out_gate)).astype(BF16))
        if base is None:
            o_ref[rows, :] += delta
        else:
            o_ref[rows, :] = base + delta

    def start(p, src_ref):
        sub = subs[p]
        for c, rows in halves:
            res = src_ref[rows, :]
            h = _modulate(res, g_refs[p][...], mod_refs[p][c, 3 * sub:3 * sub + 1, :],
                          mod_refs[p][c, 3 * sub + 1:3 * sub + 2, :]).astype(BF16)
            h_scr[rows, :] = h
            chunk(h, rows, mod_refs[p][c, 3 * sub + 2:3 * sub + 3, :],
                  base=None if src_ref is o_ref else res)

    if n_x == 1:
        pl.when(j == 0)(functools.partial(start, 0, x_refs[0]))
    else:
        pl.when((j == 0) & (i < x_split))(functools.partial(start, 0, x_refs[0]))
        pl.when((j == 0) & (i >= x_split))(functools.partial(start, 0, x_refs[1]))
    for p in range(1, n_ph):
        pl.when(j == p * N_FFN_CHUNKS)(functools.partial(start, p, o_ref))

    @pl.when(j % N_FFN_CHUNKS != 0)
    def _():
        for c, rows in halves:
            out_gate = mod_refs[0][c, 3 * subs[0] + 2:3 * subs[0] + 3, :]
            if n_ph == 2:
                out_gate = jnp.where(j < N_FFN_CHUNKS, out_gate,
                                     mod_refs[1][c, 3 * subs[1] + 2:3 * subs[1] + 3, :])
            chunk(h_scr[rows, :], rows, out_gate)


def _ffn(xs, mod, norm_g, ffn_w_in, ffn_w_out, phases, row0=0, n_rows=NTOK):
    per = FFN_TM // TM
    tile0 = row0 // FFN_TM
    n_ph = len(phases)
    layers = [p[0] for p in phases]
    whichs = [p[1] for p in phases]

    def by_phase(vals):
        if n_ph == 1:
            return lambda j: vals[0]
        return lambda j: vals[0] + (j // N_FFN_CHUNKS) * (vals[1] - vals[0])

    layer_of, which_of = by_phase(layers), by_phase(whichs)
    chunk_of = lambda j: j % N_FFN_CHUNKS
    if len(xs) == 1:
        x_specs = [pl.BlockSpec((FFN_TM, D_MODEL), lambda i, j: (tile0 + i, 0))]
        x_split = 0
    else:
        x_split = xs[0].shape[0] // FFN_TM
        x_specs = [
            pl.BlockSpec((FFN_TM, D_MODEL), lambda i, j: (jnp.minimum(i, x_split - 1), 0)),
            pl.BlockSpec((FFN_TM, D_MODEL), lambda i, j: (jnp.maximum(i - x_split, 0), 0),
                         pipeline_mode=pl.Buffered(1)),
        ]
    mod_specs = [pl.BlockSpec((None, per, N_MOD, D_MODEL),
                              lambda i, j, l=l: (l, _cond_of_tile(tile0 + i, FFN_TM) // per, 0, 0))
                 for l in layers]
    g_specs = [pl.BlockSpec((None, None, 1, D_MODEL), lambda i, j, l=l, w=w: (l, 2 * w, 0, 0))
               for l, w in phases]
    return pl.pallas_call(
        functools.partial(_ffn_kernel, tuple(2 * w for w in whichs), len(xs), x_split),
        grid=(n_rows // FFN_TM, n_ph * N_FFN_CHUNKS),
        in_specs=x_specs + mod_specs + g_specs + [
            pl.BlockSpec((None, None, D_MODEL, FFN_CHUNK),
                         lambda i, j: (layer_of(j), which_of(j), 0, chunk_of(j))),
            pl.BlockSpec((None, None, D_MODEL, FFN_CHUNK),
                         lambda i, j: (layer_of(j), which_of(j), 0, N_FFN_CHUNKS + chunk_of(j))),
            pl.BlockSpec((None, None, FFN_CHUNK, D_MODEL),
                         lambda i, j: (layer_of(j), which_of(j), chunk_of(j), 0)),
        ],
        out_specs=pl.BlockSpec((FFN_TM, D_MODEL), lambda i, j: (i, 0)),
        out_shape=jax.ShapeDtypeStruct((n_rows, D_MODEL), F32),
        scratch_shapes=[pltpu.VMEM((FFN_TM, D_MODEL), BF16)],
        compiler_params=_cparams("parallel", "arbitrary"),
        name="ffn",
    )(*xs, *([mod] * n_ph), *([norm_g.reshape(DEPTH, 3, 1, D_MODEL)] * n_ph),
      ffn_w_in, ffn_w_in, ffn_w_out)


def _group_mean_sq(x, width):
    r = lax.broadcasted_iota(jnp.int32, (width, width), 0) // HEAD_DIM
    c = lax.broadcasted_iota(jnp.int32, (width, width), 1) // HEAD_DIM
    ones = (r == c).astype(BF16)
    return _dot((x * x).astype(BF16), ones) * (1.0 / HEAD_DIM)


def _head_norm_rope(x, gain, cos, sin_next, sin_prev):
    width = x.shape[-1]
    xn = x * lax.rsqrt(_group_mean_sq(x, width) + RMS_EPS) * gain
    nxt = pltpu.roll(xn, width - 1, 1)
    prv = pltpu.roll(xn, 1, 1)
    return xn * cos + nxt * sin_next + prv * sin_prev


def _even_in_kernel(x_ref, mod_ref, g_ref, w_ref, qg_ref, kg_ref,
                    cq_ref, snq_ref, spq_ref, ck_ref, snk_ref, spk_ref,
                    q_ref, k_ref, v_ref, glu_ref, kc_ref, vc_ref, w_scr):
    i = pl.program_id(0)

    @pl.when(i == 0)
    def _():
        w_scr[...] = w_ref[...].astype(BF16)

    a0 = ATT_Q + 2 * ATT_KV
    for r0 in range(0, EVEN_IN_ROWS, EVEN_IN_PIECE):
        rows = slice(r0, r0 + EVEN_IN_PIECE)
        h = _modulate(x_ref[rows, :], g_ref[...], mod_ref[0, 3:4, :], mod_ref[0, 4:5, :]).astype(BF16)

        def proj(lo, hi):
            return _dot(h, w_scr[:, lo:hi])

        q = _head_norm_rope(proj(0, ATT_Q), qg_ref[...], cq_ref[rows, :], snq_ref[rows, :], spq_ref[rows, :])
        q_ref[rows, :] = (q * (HEAD_DIM ** -0.5 * LOG2_E)).astype(BF16)
        k_ref[rows, :] = _head_norm_rope(proj(ATT_Q, ATT_Q + ATT_KV), kg_ref[...],
                                         ck_ref[rows, :], snk_ref[rows, :], spk_ref[rows, :])
        v_ref[rows, :] = proj(ATT_Q + ATT_KV, a0)
        glu = proj(a0, a0 + CONV_CH) * jax.nn.sigmoid(proj(a0 + CONV_CH, a0 + 2 * CONV_CH))
        glu_ref[rows, :] = glu.astype(BF16)

    @pl.when(i < NTOK_P // EVEN_IN_ROWS)
    def _():
        kc_ref[...] = k_ref[...]
        vc_ref[...] = v_ref[...]


def _rope_tables():
    t = np.arange(DEC_SEQ)
    row = (t // GRID_W).astype(np.float64)
    col = (t % GRID_W).astype(np.float64)
    inv_freq = 1.0 / (ROPE_THETA ** (np.arange(0, AXIS_ROPE_DIM, 2, dtype=np.float64) / AXIS_ROPE_DIM))
    ang = np.concatenate([row[:, None] * inv_freq, col[:, None] * inv_freq], axis=-1)
    ang = np.repeat(ang, 2, axis=-1)
    even = (np.arange(HEAD_DIM) % 2 == 0)[None, :]
    cos = np.cos(ang)
    sin_next = np.where(even, -np.sin(ang), 0.0)
    sin_prev = np.where(even, 0.0, np.sin(ang))

    def both(tab):
        ident = np.ones_like(tab) if tab is cos else np.zeros_like(tab)
        return np.stack([np.tile(ident, (1, N_HEADS)), np.tile(tab, (1, N_HEADS))]).astype(np.float32)

    return both(cos), both(sin_next), both(sin_prev)


def _even_in(x, mod_l, norm_g, ev_w_in, q_norm_g, k_norm_g, layer, j):
    cos, sin_next, sin_prev = (jnp.asarray(t) for t in _rope_tables())
    qg = jnp.tile(q_norm_g[j], N_HEADS).reshape(1, ATT_Q)
    kg = jnp.tile(k_norm_g[j], N_KV_HEADS).reshape(1, ATT_KV)
    rows = EVEN_IN_ROWS
    per_seq = DEC_SEQ // rows
    first_latent = NTOK_P // rows
    tab_idx = lambda i: (jnp.minimum(i // first_latent, 1), i % per_seq, 0)
    tab_q = pl.BlockSpec((None, rows, ATT_Q), tab_idx)
    tab_k = pl.BlockSpec((None, rows, ATT_KV), tab_idx)
    tok = lambda n: pl.BlockSpec((rows, n), lambda i: (i, 0))
    cache = pl.BlockSpec((rows, ATT_KV), lambda i: (jnp.minimum(i, first_latent - 1), 0))
    return pl.pallas_call(
        _even_in_kernel,
        grid=(NTOK // rows,),
        in_specs=[
            tok(D_MODEL),
            pl.BlockSpec((1, N_MOD, D_MODEL), lambda i: (_cond_of_tile(i, rows), 0, 0)),
            pl.BlockSpec((None, None, 1, D_MODEL), lambda i: (layer, 1, 0, 0)),
            pl.BlockSpec((None, D_MODEL, EVEN_IN), lambda i: (j, 0, 0), pipeline_mode=pl.Buffered(1)),
            pl.BlockSpec((1, ATT_Q), lambda i: (0, 0)),
            pl.BlockSpec((1, ATT_KV), lambda i: (0, 0)),
            tab_q, tab_q, tab_q, tab_k, tab_k, tab_k,
        ],
        out_specs=[tok(ATT_Q), tok(ATT_KV), tok(ATT_KV), tok(CONV_CH), cache, cache],
        out_shape=[
            jax.ShapeDtypeStruct((NTOK, ATT_Q), BF16),
            jax.ShapeDtypeStruct((NTOK, ATT_KV), F32),
            jax.ShapeDtypeStruct((NTOK, ATT_KV), F32),
            jax.ShapeDtypeStruct((NTOK, CONV_CH), BF16),
            jax.ShapeDtypeStruct((NTOK_P, ATT_KV), F32),
            jax.ShapeDtypeStruct((NTOK_P, ATT_KV), F32),
        ],
        scratch_shapes=[pltpu.VMEM((D_MODEL, EVEN_IN), BF16)],
        compiler_params=_cparams("arbitrary"),
        name="even_in",
    )(x, mod_l, norm_g.reshape(DEPTH, 3, 1, D_MODEL), ev_w_in, qg, kg,
      cos, sin_next, sin_prev, cos, sin_next, sin_prev)


ATT_WIDTH = GQA_GROUP * HEAD_DIM


def _expand_kv_head(x, kv):
    lane = lax.broadcasted_iota(jnp.int32, x.shape, 1)
    swapped = pltpu.roll(x, HEAD_DIM, 1)
    pair = jnp.where((lane // HEAD_DIM) == kv, x, swapped)
    return jnp.concatenate([pair, pair], axis=1)


def _attend(q_ref, q_rows, kbig, vbig, key0, n_keys, attn_scr, out_rows):
    lane_group = lax.broadcasted_iota(jnp.int32, (ATT_TQ, ATT_WIDTH), 1) // HEAD_DIM
    for kv in range(N_KV_HEADS):
        cols = slice(kv * ATT_WIDTH, (kv + 1) * ATT_WIDTH)
        q = q_ref[q_rows, cols]
        kb = kbig[kv, key0:key0 + n_keys, :]
        vb = vbig[kv, key0:key0 + n_keys, :]
        acc = jnp.zeros((ATT_TQ, ATT_WIDTH), F32)
        for g in range(GQA_GROUP):
            mine = lane_group == g
            qm = jnp.where(mine, q, jnp.zeros_like(q))
            s = lax.dot_general(qm, kb, (((1,), (1,)), ((), ())), preferred_element_type=F32)
            m = jnp.max(s, axis=-1, keepdims=True)
            p = jnp.exp2(s - m)
            denom = jnp.sum(p, axis=-1, keepdims=True)
            o = _dot(p.astype(BF16), vb)
            acc = jnp.where(mine, o / denom, acc)
        attn_scr[out_rows, cols] = acc.astype(BF16)


MIX_BLOCKS = 2
CONV_PAD_SEQ = SEQ + 2 * CONV_HALO
CONV_WIN = ATT_TQ + 2 * CONV_HALO
CONV_SPAN = CONV_WIN - V7X_SUBLANES


def _conv_block(win0_scr, win_scr, w_ref, y_scr):
    off = CONV_HALO - CONV_K // 2
    for r0 in range(0, ATT_TQ, CONV_ROWS):
        for c0 in range(0, CONV_CH, V7X_LANES):
            cols = slice(c0, c0 + V7X_LANES)
            acc = jnp.zeros((CONV_ROWS, V7X_LANES), F32)
            for k in range(CONV_K):
                res = (off + k) % V7X_SUBLANES
                a = r0 + off + k - res
                if res == 0:
                    tap = win0_scr[a:a + CONV_ROWS, cols]
                else:
                    tap = win_scr[res - 1, a:a + CONV_ROWS, cols]
                acc = acc + tap * w_ref[k:k + 1, cols]
            y_scr[r0:r0 + CONV_ROWS, cols] = acc


def _even_mix_kernel(q_ref, k_ref, v_ref, ck_ref, cv_ref, glu_ref, w_ref, b_ref, g_ref, beta_ref,
                     res_ref, mod_ref, wo_ref, o_ref,
                     kbig, vbig, pad_scr, win0_scr, win_scr, y_scr, attn_scr, wo_scr):
    i = pl.program_id(0)
    t = pl.program_id(1)
    border = jnp.zeros((CONV_HALO, CONV_CH), F32)

    @pl.when((i == 0) & (t == 0))
    def _():
        wo_scr[...] = wo_ref[...].astype(BF16)

    def q_rows(b):
        return pl.ds(pl.multiple_of((t * MIX_BLOCKS + b) * ATT_TQ, ATT_TQ), ATT_TQ)

    def expand(k, v, key0, n):
        for kv in range(N_KV_HEADS):
            kbig[kv, key0:key0 + n, :] = _expand_kv_head(k, kv).astype(BF16)
            vbig[kv, key0:key0 + n, :] = _expand_kv_head(v, kv).astype(BF16)

    def mix(b, key0, n_keys, win_start):
        out_rows = slice(b * ATT_TQ, (b + 1) * ATT_TQ)
        win0, win, y = win0_scr.at[b], win_scr.at[b], y_scr.at[b]
        win0[...] = pad_scr[pl.ds(pl.multiple_of(win_start, V7X_SUBLANES), CONV_WIN), :]
        for res in range(1, V7X_SUBLANES):
            win[res - 1] = win0[res:res + CONV_SPAN, :]
        _attend(q_ref, q_rows(b), kbig, vbig, key0, n_keys, attn_scr, out_rows)
        _conv_block(win0, win, w_ref, y)
        yn = _layer_norm(y[...] + b_ref[...], g_ref[...], beta_ref[...])
        conv = (yn * jax.nn.sigmoid(yn)).astype(BF16)
        mixed = _dot(attn_scr[out_rows, :], wo_scr[:ATT_Q, :]) + _dot(conv, wo_scr[ATT_Q:, :])
        o_ref[out_rows, :] = res_ref[out_rows, :] + mod_ref[0, 5:6, :] * mixed

    @pl.when(i < N_TILES_P)
    def _():
        @pl.when(t == 0)
        def _():
            for s in range(SEQ_PER_TILE):
                base = s * CONV_PAD_SEQ
                pad_scr[base:base + CONV_HALO, :] = border
                pad_scr[base + CONV_HALO:base + CONV_HALO + SEQ, :] = (
                    glu_ref[s * SEQ:(s + 1) * SEQ, :].astype(F32))
                pad_scr[base + CONV_HALO + SEQ:base + CONV_PAD_SEQ, :] = border

        for b in range(MIX_BLOCKS):
            expand(k_ref[q_rows(b), :], v_ref[q_rows(b), :], b * SEQ, SEQ)
        for b in range(MIX_BLOCKS):
            mix(b, b * SEQ, SEQ, (t * MIX_BLOCKS + b) * CONV_PAD_SEQ)

    @pl.when(i >= N_TILES_P)
    def _():
        @pl.when(t == 0)
        def _():
            pad_scr[0:CONV_HALO, :] = border
            pad_scr[CONV_HALO:CONV_HALO + DEC_SEQ, :] = glu_ref[...].astype(F32)
            pad_scr[CONV_HALO + DEC_SEQ:2 * CONV_HALO + DEC_SEQ, :] = border
            expand(jnp.concatenate([ck_ref[...], k_ref[...]], axis=0),
                   jnp.concatenate([cv_ref[...], v_ref[...]], axis=0), 0, PAST_LEN + DEC_SEQ)

        for b in range(MIX_BLOCKS):
            mix(b, 0, PAST_LEN + DEC_SEQ, (t * MIX_BLOCKS + b) * ATT_TQ)


def _even_mix(x, mod_l, q, k, v, ck, cv, glu, conv_w, conv_b, cn_g, cn_b, w_out, j):
    rows = MIX_BLOCKS * ATT_TQ
    nq = TM // rows
    tok = lambda n: pl.BlockSpec((TM, n), lambda i, t: (i, 0))
    blk = pl.BlockSpec((rows, D_MODEL), lambda i, t: (i * nq + t, 0))
    ctx = pl.BlockSpec((None, None, PAST_LEN, ATT_KV), lambda i, t: (jnp.maximum(i - N_TILES_P, 0), j, 0, 0))
    vec = pl.BlockSpec((None, 1, CONV_CH), lambda i, t: (j, 0, 0))
    n_even = conv_b.shape[0]
    n_keys = PAST_LEN + DEC_SEQ
    return pl.pallas_call(
        _even_mix_kernel,
        grid=(N_TILES, nq),
        in_specs=[
            tok(ATT_Q), tok(ATT_KV), tok(ATT_KV), ctx, ctx, tok(CONV_CH),
            pl.BlockSpec((None, CONV_K, CONV_CH), lambda i, t: (j, 0, 0)),
            vec, vec, vec,
            blk,
            pl.BlockSpec((1, N_MOD, D_MODEL), lambda i, t: (_cond_of_tile(i), 0, 0)),
            pl.BlockSpec((None, ATT_Q + CONV_CH, D_MODEL), lambda i, t: (j, 0, 0),
                         pipeline_mode=pl.Buffered(1)),
        ],
        out_specs=blk,
        out_shape=jax.ShapeDtypeStruct((NTOK, D_MODEL), F32),
        scratch_shapes=[
            pltpu.VMEM((N_KV_HEADS, n_keys, ATT_WIDTH), BF16),
            pltpu.VMEM((N_KV_HEADS, n_keys, ATT_WIDTH), BF16),
            pltpu.VMEM((max(SEQ_PER_TILE * CONV_PAD_SEQ, DEC_SEQ + 2 * CONV_HALO), CONV_CH), F32),
            pltpu.VMEM((MIX_BLOCKS, CONV_WIN, CONV_CH), F32),
            pltpu.VMEM((MIX_BLOCKS, V7X_SUBLANES - 1, CONV_SPAN, CONV_CH), F32),
            pltpu.VMEM((MIX_BLOCKS, ATT_TQ, CONV_CH), F32),
            pltpu.VMEM((rows, ATT_Q), BF16),
            pltpu.VMEM((ATT_Q + CONV_CH, D_MODEL), BF16),
        ],
        compiler_params=_cparams("arbitrary", "arbitrary"),
        name="even_mix",
    )(q, k, v, ck, cv, glu, conv_w, conv_b.reshape(n_even, 1, CONV_CH),
      cn_g.reshape(n_even, 1, CONV_CH), cn_b.reshape(n_even, 1, CONV_CH), x, mod_l, w_out)


def _odd_in_kernel(x_ref, mod_ref, g_ref, w_ref, sg_ref, sb_ref, ws_ref, bs_ref, f_ref, sgu_ref, w_scr):
    @pl.when(pl.program_id(0) == 0)
    def _():
        w_scr[...] = w_ref[...].astype(BF16)

    ws = [ws_ref[g].astype(BF16) for g in range(SGU_GROUPS)]
    for r0 in range(0, TM, ODD_IN_PIECE):
        piece = slice(r0, r0 + ODD_IN_PIECE)
        h = _modulate(x_ref[piece, :], g_ref[...], mod_ref[0, 3:4, :], mod_ref[0, 4:5, :]).astype(BF16)

        def proj(lo, hi):
            return _dot(h, w_scr[:, lo:hi])

        f_ref[piece, :] = proj(0, FOURIER_CH).astype(BF16)
        u = proj(FOURIER_CH, FOURIER_CH + SGU_CH)
        v = _layer_norm(proj(FOURIER_CH + SGU_CH, ODD_IN), sg_ref[...], sb_ref[...]).astype(BF16)
        for g in range(SGU_GROUPS):
            bias = bs_ref[:, g:g + 1]
            cols = slice(g * SGU_GROUP_CH, (g + 1) * SGU_GROUP_CH)
            for n in range(ODD_IN_PIECE // CHUNK):
                rows = slice(n * CHUNK, (n + 1) * CHUNK)
                mixed = _dot(ws[g], v[rows, cols]) + bias
                sgu_ref[r0 + n * CHUNK:r0 + (n + 1) * CHUNK, cols] = (u[rows, cols] * mixed).astype(BF16)


def _odd_in(x, mod_l, norm_g, od_w_in, sgu_norm_g, sgu_norm_b, sgu_w, sgu_b, layer, j):
    tok = lambda n: pl.BlockSpec((TM, n), lambda i: (i, 0))
    return pl.pallas_call(
        _odd_in_kernel,
        grid=(N_TILES,),
        in_specs=[
            tok(D_MODEL),
            pl.BlockSpec((1, N_MOD, D_MODEL), lambda i: (_cond_of_tile(i), 0, 0)),
            pl.BlockSpec((None, None, 1, D_MODEL), lambda i: (layer, 1, 0, 0)),
            pl.BlockSpec((None, D_MODEL, ODD_IN), lambda i: (j, 0, 0), pipeline_mode=pl.Buffered(1)),
            pl.BlockSpec((1, SGU_CH), lambda i: (0, 0)),
            pl.BlockSpec((1, SGU_CH), lambda i: (0, 0)),
            pl.BlockSpec((None, SGU_GROUPS, CHUNK, CHUNK), lambda i: (j, 0, 0, 0)),
            pl.BlockSpec((CHUNK, SGU_GROUPS), lambda i: (0, 0)),
        ],
        out_specs=[tok(FOURIER_CH), tok(SGU_CH)],
        out_shape=[jax.ShapeDtypeStruct((NTOK, FOURIER_CH), BF16),
                   jax.ShapeDtypeStruct((NTOK, SGU_CH), BF16)],
        scratch_shapes=[pltpu.VMEM((D_MODEL, ODD_IN), BF16)],
        compiler_params=_cparams("arbitrary"),
        name="odd_in",
    )(x, mod_l, norm_g.reshape(DEPTH, 3, 1, D_MODEL), od_w_in,
      sgu_norm_g[j].reshape(1, SGU_CH), sgu_norm_b[j].reshape(1, SGU_CH), sgu_w, sgu_b[j].T)


def _fourier_out_kernel(f_ref, wch_ref, wseq_p_ref, wseq_s_ref, res_ref, mod_ref, sgu_ref, wo_ref,
                        o_ref, wch_scr, wseq_p_scr, wseq_s_scr, wo_scr):
    i = pl.program_id(0)

    @pl.when(i == 0)
    def _():
        wch_scr[...] = wch_ref[...].astype(BF16)
        wseq_p_scr[...] = wseq_p_ref[...].astype(BF16)
        wseq_s_scr[...] = wseq_s_ref[...].astype(BF16)
        wo_scr[...] = wo_ref[...].astype(BF16)

    gate = mod_ref[0, 5:6, :]

    def channel_dft(rows):
        pq = _dot(f_ref[rows, :], wch_scr[...]).astype(BF16)
        return jnp.concatenate([pq[:, :FOURIER_CH], pq[:, FOURIER_CH:]], axis=0)

    def finish(rows, wseq_rows, stacked, seq):
        scale = 1.0 / np.sqrt(float(seq * FOURIER_GROUP_CH))
        four = (_dot(wseq_rows, stacked) * scale).astype(BF16)
        mixed = _dot(four, wo_scr[:FOURIER_CH, :]) + _dot(sgu_ref[rows, :], wo_scr[FOURIER_CH:, :])
        o_ref[rows, :] = res_ref[rows, :] + gate * mixed

    @pl.when(i < N_TILES_P)
    def _():
        for s in range(SEQ_PER_TILE):
            rows = slice(s * SEQ, (s + 1) * SEQ)
            finish(rows, wseq_p_scr[...], channel_dft(rows), SEQ)

    @pl.when(i >= N_TILES_P)
    def _():
        stacked = channel_dft(slice(0, DEC_SEQ))
        for s in range(SEQ_PER_TILE):
            rows = slice(s * SEQ, (s + 1) * SEQ)
            finish(rows, wseq_s_scr[rows, :], stacked, DEC_SEQ)


def _dft_constants():
    def cos_sin(n):
        idx = np.arange(n)
        ang = 2.0 * np.pi * ((idx[:, None] * idx[None, :]) % n) / n
        return np.cos(ang), np.sin(ang)

    cc, sc = cos_sin(FOURIER_GROUP_CH)
    eye = np.eye(FOURIER_GROUPS)
    wch = np.concatenate([np.kron(eye, cc), np.kron(eye, sc)], axis=1)

    def wseq(seq):
        cl, sl = cos_sin(seq)
        return jnp.asarray(np.concatenate([cl, -sl], axis=1), dtype=F32)

    return jnp.asarray(wch, dtype=F32), wseq(SEQ), wseq(DEC_SEQ)


def _fourier_out(x, mod_l, f, sgu, w_out, j):
    wch, wseq_p, wseq_s = _dft_constants()
    tok = lambda n: pl.BlockSpec((TM, n), lambda i: (i, 0))
    const = lambda shape: pl.BlockSpec(shape, lambda i: (0,) * len(shape), pipeline_mode=pl.Buffered(1))
    return pl.pallas_call(
        _fourier_out_kernel,
        grid=(N_TILES,),
        in_specs=[
            tok(FOURIER_CH), const(wch.shape), const(wseq_p.shape), const(wseq_s.shape),
            tok(D_MODEL),
            pl.BlockSpec((1, N_MOD, D_MODEL), lambda i: (_cond_of_tile(i), 0, 0)),
            tok(SGU_CH),
            pl.BlockSpec((None, FOURIER_CH + SGU_CH, D_MODEL), lambda i: (j, 0, 0),
                         pipeline_mode=pl.Buffered(1)),
        ],
        out_specs=tok(D_MODEL),
        out_shape=jax.ShapeDtypeStruct((NTOK, D_MODEL), F32),
        scratch_shapes=[pltpu.VMEM(wch.shape, BF16),
                        pltpu.VMEM(wseq_p.shape, BF16), pltpu.VMEM(wseq_s.shape, BF16),
                        pltpu.VMEM((FOURIER_CH + SGU_CH, D_MODEL), BF16)],
        compiler_params=_cparams("arbitrary"),
        name="fourier_out",
    )(f, wch, wseq_p, wseq_s, x, mod_l, sgu, w_out)


def kernel(x_prompt, x_sample, cache_k, cache_v, c, c_ctx, w_mod, b_mod, norm_g, ffn_w_in, ffn_w_out, ev_w_in, ev_w_out, q_norm_g, k_norm_g, conv_w, conv_b, conv_norm_g, conv_norm_b, od_w_in, od_w_out, sgu_norm_g, sgu_norm_b, sgu_w, sgu_b):
    cond = jnp.concatenate([c_ctx[None, :], c_ctx[None, :], c,
                            jnp.zeros((N_COND - COND_LATENT0 - DEC_BATCH, D_MODEL), F32)], axis=0)
    mod = _adaln(cond, w_mod, b_mod).reshape(DEPTH, N_COND, N_MOD, D_MODEL)
    ck = cache_k.reshape(DEC_BATCH, -1, PAST_LEN, ATT_KV)
    cv = cache_v.reshape(DEC_BATCH, -1, PAST_LEN, ATT_KV)

    ffn = functools.partial(_ffn, mod=mod, norm_g=norm_g, ffn_w_in=ffn_w_in, ffn_w_out=ffn_w_out)
    x = ffn((x_prompt.reshape(NTOK_P, D_MODEL), x_sample.reshape(NTOK_S, D_MODEL)), phases=[(0, 0)])
    new_k, new_v = [], []
    for layer in range(DEPTH):
        mod_l = mod[layer]
        j = layer // 2
        if layer % 2 == 0:
            q, k, v, glu, kc, vc = _even_in(x, mod_l, norm_g, ev_w_in, q_norm_g, k_norm_g, layer, j)
            new_k.append(kc.reshape(BATCH, SEQ, N_KV_HEADS, HEAD_DIM))
            new_v.append(vc.reshape(BATCH, SEQ, N_KV_HEADS, HEAD_DIM))
            x = _even_mix(x, mod_l, q, k, v, ck, cv, glu, conv_w, conv_b, conv_norm_g, conv_norm_b,
                          ev_w_out, j)
        else:
            f, sgu = _odd_in(x, mod_l, norm_g, od_w_in, sgu_norm_g, sgu_norm_b, sgu_w, sgu_b, layer, j)
            x = _fourier_out(x, mod_l, f, sgu, od_w_out, j)
        if layer < DEPTH - 1:
            x = ffn((x,), phases=[(layer, 1), (layer + 1, 0)])

    last = [(DEPTH - 1, 1)]
    y_prompt = ffn((x,), phases=last, row0=0, n_rows=NTOK_P)
    y_sample = ffn((x,), phases=last, row0=NTOK_P, n_rows=NTOK_S)
    return (y_prompt.reshape(BATCH, SEQ, D_MODEL), y_sample.reshape(DEC_BATCH, DEC_SEQ, D_MODEL),
            jnp.stack(new_k, axis=1), jnp.stack(new_v, axis=1))
```

```python
import functools

import numpy as np
import jax
import jax.numpy as jnp
from jax import lax
from jax.experimental import pallas as pl
from jax.experimental.pallas import tpu as pltpu

D_MODEL = 1024
BATCH = 32
SEQ = 256
DEPTH = 4
DEC_BATCH = 4
DEC_SEQ = 1024
PAST_LEN = 512
GRID_W = 64
N_HEADS = 8
N_KV_HEADS = 2
HEAD_DIM = 64
GQA_GROUP = N_HEADS // N_KV_HEADS
AXIS_ROPE_DIM = HEAD_DIM // 2
ROPE_THETA = 10000.0
ATT_Q = N_HEADS * HEAD_DIM
ATT_KV = N_KV_HEADS * HEAD_DIM
CONV_CH = 512
CONV_K = 31
FOURIER_GROUPS = 4
FOURIER_GROUP_CH = 128
FOURIER_CH = FOURIER_GROUPS * FOURIER_GROUP_CH
SGU_GROUPS = 4
SGU_GROUP_CH = 128
SGU_CH = SGU_GROUPS * SGU_GROUP_CH
CHUNK = 128
FFN_DIM = 2816
N_MOD = 9
EVEN_IN = ATT_Q + 2 * ATT_KV + 2 * CONV_CH
ODD_IN = FOURIER_CH + 2 * SGU_CH
RMS_EPS = 1e-6
LN_EPS = 1e-5
LOG2_E = 1.4426950408889634

BF16 = jnp.bfloat16
F32 = jnp.float32

V7X_SUBLANES = 8
V7X_LANES = 128
NTOK_P = BATCH * SEQ
NTOK_S = DEC_BATCH * DEC_SEQ
NTOK = NTOK_P + NTOK_S
TM = DEC_SEQ
N_TILES = NTOK // TM
N_TILES_P = NTOK_P // TM
SEQ_PER_TILE = TM // SEQ
N_COND = 8
COND_LATENT0 = 2
FFN_TM = 2 * TM
FFN_PIECE = TM
FFN_CHUNK = 256
N_FFN_CHUNKS = FFN_DIM // FFN_CHUNK
EVEN_IN_ROWS = 1024
EVEN_IN_PIECE = 256
ODD_IN_PIECE = 1024
ADALN_TN = 2304
CONV_ROWS = 128
CONV_HALO = 16
ATT_TQ = SEQ
V7X_VMEM_LIMIT_BYTES = 56 * 1024 * 1024


def _cparams(*sem):
    return pltpu.CompilerParams(dimension_semantics=sem, vmem_limit_bytes=V7X_VMEM_LIMIT_BYTES)


def _cond_of_tile(i, rows=TM):
    return jnp.maximum((i * rows - NTOK_P) // DEC_SEQ + COND_LATENT0, 0)


def _dot(a, b):
    return jnp.dot(a, b, preferred_element_type=F32)


def _modulate(x, g, shift, scale):
    ms = jnp.mean(x * x, axis=-1, keepdims=True)
    return (x * lax.rsqrt(ms + RMS_EPS)) * (g * (1.0 + scale)) + shift


def _layer_norm(x, g, b):
    mu = jnp.mean(x, axis=-1, keepdims=True)
    xc = x - mu
    var = jnp.mean(xc * xc, axis=-1, keepdims=True)
    return xc * lax.rsqrt(var + LN_EPS) * g + b


def _adaln_kernel(c_ref, w_ref, b_ref, o_ref):
    c = c_ref[...]
    s = (c * jax.nn.sigmoid(c)).astype(BF16)
    o_ref[...] = _dot(s, w_ref[...].astype(BF16)) + b_ref[...]


def _adaln(cond, w_mod, b_mod):
    n_out = N_MOD * D_MODEL
    return pl.pallas_call(
        _adaln_kernel,
        grid=(DEPTH, n_out // ADALN_TN),
        in_specs=[
            pl.BlockSpec((N_COND, D_MODEL), lambda l, n: (0, 0)),
            pl.BlockSpec((None, D_MODEL, ADALN_TN), lambda l, n: (l, 0, n)),
            pl.BlockSpec((None, 1, ADALN_TN), lambda l, n: (l, 0, n)),
        ],
        out_specs=pl.BlockSpec((None, N_COND, ADALN_TN), lambda l, n: (l, 0, n)),
        out_shape=jax.ShapeDtypeStruct((DEPTH, N_COND, n_out), F32),
        compiler_params=_cparams("parallel", "parallel"),
        name="adaln",
    )(cond, w_mod, b_mod.reshape(DEPTH, 1, n_out))


def _ffn_kernel(subs, n_x, x_split, *refs):
    n_ph = len(subs)
    x_refs = refs[:n_x]
    mod_refs = refs[n_x:n_x + n_ph]
    g_refs = refs[n_x + n_ph:n_x + 2 * n_ph]
    wg_ref, wu_ref, wo_ref, o_ref, h_scr = refs[n_x + 2 * n_ph:]
    i = pl.program_id(0)
    j = pl.program_id(1)
    halves = [(r0 // TM, slice(r0, r0 + FFN_PIECE)) for r0 in range(0, FFN_TM, FFN_PIECE)]

    def chunk(h, rows, out_gate, base=None):
        gate = _dot(h, wg_ref[...].astype(BF16))
        up = _dot(h, wu_ref[...].astype(BF16))
        a = (gate * jax.nn.sigmoid(gate) * up).astype(BF16)
        delta = _dot(a, (wo_ref[...] * (0.5 * out_gate)).astype(BF16))
        if base is None:
            o_ref[rows, :] += delta
        else:
            o_ref[rows, :] = base + delta

    def start(p, src_ref):
        sub = subs[p]
        for c, rows in halves:
            res = src_ref[rows, :]
            h = _modulate(res, g_refs[p][...], mod_refs[p][c, 3 * sub:3 * sub + 1, :],
                          mod_refs[p][c, 3 * sub + 1:3 * sub + 2, :]).astype(BF16)
            h_scr[rows, :] = h
            chunk(h, rows, mod_refs[p][c, 3 * sub + 2:3 * sub + 3, :],
                  base=None if src_ref is o_ref else res)

    if n_x == 1:
        pl.when(j == 0)(functools.partial(start, 0, x_refs[0]))
    else:
        pl.when((j == 0) & (i < x_split))(functools.partial(start, 0, x_refs[0]))
        pl.when((j == 0) & (i >= x_split))(functools.partial(start, 0, x_refs[1]))
    for p in range(1, n_ph):
        pl.when(j == p * N_FFN_CHUNKS)(functools.partial(start, p, o_ref))

    @pl.when(j % N_FFN_CHUNKS != 0)
    def _():
        for c, rows in halves:
            out_gate = mod_refs[0][c, 3 * subs[0] + 2:3 * subs[0] + 3, :]
            if n_ph == 2:
                out_gate = jnp.where(j < N_FFN_CHUNKS, out_gate,
                                     mod_refs[1][c, 3 * subs[1] + 2:3 * subs[1] + 3, :])
            chunk(h_scr[rows, :], rows, out_gate)


def _ffn(xs, mod, norm_g, ffn_w_in, ffn_w_out, phases, row0=0, n_rows=NTOK):
    per = FFN_TM // TM
    tile0 = row0 // FFN_TM
    n_ph = len(phases)
    layers = [p[0] for p in phases]
    whichs = [p[1] for p in phases]

    def by_phase(vals):
        if n_ph == 1:
            return lambda j: vals[0]
        return lambda j: vals[0] + (j // N_FFN_CHUNKS) * (vals[1] - vals[0])

    layer_of, which_of = by_phase(layers), by_phase(whichs)
    chunk_of = lambda j: j % N_FFN_CHUNKS
    if len(xs) == 1:
        x_specs = [pl.BlockSpec((FFN_TM, D_MODEL), lambda i, j: (tile0 + i, 0))]
        x_split = 0
    else:
        x_split = xs[0].shape[0] // FFN_TM
        x_specs = [
            pl.BlockSpec((FFN_TM, D_MODEL), lambda i, j: (jnp.minimum(i, x_split - 1), 0)),
            pl.BlockSpec((FFN_TM, D_MODEL), lambda i, j: (jnp.maximum(i - x_split, 0), 0),
                         pipeline_mode=pl.Buffered(1)),
        ]
    mod_specs = [pl.BlockSpec((None, per, N_MOD, D_MODEL),
                              lambda i, j, l=l: (l, _cond_of_tile(tile0 + i, FFN_TM) // per, 0, 0))
                 for l in layers]
    g_specs = [pl.BlockSpec((None, None, 1, D_MODEL), lambda i, j, l=l, w=w: (l, 2 * w, 0, 0))
               for l, w in phases]
    return pl.pallas_call(
        functools.partial(_ffn_kernel, tuple(2 * w for w in whichs), len(xs), x_split),
        grid=(n_rows // FFN_TM, n_ph * N_FFN_CHUNKS),
        in_specs=x_specs + mod_specs + g_specs + [
            pl.BlockSpec((None, None, D_MODEL, FFN_CHUNK),
                         lambda i, j: (layer_of(j), which_of(j), 0, chunk_of(j))),
            pl.BlockSpec((None, None, D_MODEL, FFN_CHUNK),
                         lambda i, j: (layer_of(j), which_of(j), 0, N_FFN_CHUNKS + chunk_of(j))),
            pl.BlockSpec((None, None, FFN_CHUNK, D_MODEL),
                         lambda i, j: (layer_of(j), which_of(j), chunk_of(j), 0)),
        ],
        out_specs=pl.BlockSpec((FFN_TM, D_MODEL), lambda i, j: (i, 0)),
        out_shape=jax.ShapeDtypeStruct((n_rows, D_MODEL), F32),
        scratch_shapes=[pltpu.VMEM((FFN_TM, D_MODEL), BF16)],
        compiler_params=_cparams("parallel", "arbitrary"),
        name="ffn",
    )(*xs, *([mod] * n_ph), *([norm_g.reshape(DEPTH, 3, 1, D_MODEL)] * n_ph),
      ffn_w_in, ffn_w_in, ffn_w_out)


def _group_mean_sq(x, width):
    r = lax.broadcasted_iota(jnp.int32, (width, width), 0) // HEAD_DIM
    c = lax.broadcasted_iota(jnp.int32, (width, width), 1) // HEAD_DIM
    ones = (r == c).astype(BF16)
    return _dot((x * x).astype(BF16), ones) * (1.0 / HEAD_DIM)


def _head_norm_rope(x, gain, cos, sin_next, sin_prev):
    width = x.shape[-1]
    xn = x * lax.rsqrt(_group_mean_sq(x, width) + RMS_EPS) * gain
    nxt = pltpu.roll(xn, width - 1, 1)
    prv = pltpu.roll(xn, 1, 1)
    return xn * cos + nxt * sin_next + prv * sin_prev


def _even_in_kernel(x_ref, mod_ref, g_ref, w_ref, qg_ref, kg_ref,
                    cq_ref, snq_ref, spq_ref, ck_ref, snk_ref, spk_ref,
                    q_ref, k_ref, v_ref, glu_ref, kc_ref, vc_ref, w_scr, k_scr, v_scr):
    i = pl.program_id(0)

    @pl.when(i == 0)
    def _():
        w_scr[...] = w_ref[...].astype(BF16)

    a0 = ATT_Q + 2 * ATT_KV
    for r0 in range(0, EVEN_IN_ROWS, EVEN_IN_PIECE):
        rows = slice(r0, r0 + EVEN_IN_PIECE)
        h = _modulate(x_ref[rows, :], g_ref[...], mod_ref[0, 3:4, :], mod_ref[0, 4:5, :]).astype(BF16)

        def proj(lo, hi):
            return _dot(h, w_scr[:, lo:hi])

        q = _head_norm_rope(proj(0, ATT_Q), qg_ref[...], cq_ref[rows, :], snq_ref[rows, :], spq_ref[rows, :])
        q_ref[rows, :] = (q * (HEAD_DIM ** -0.5 * LOG2_E)).astype(BF16)
        k = _head_norm_rope(proj(ATT_Q, ATT_Q + ATT_KV), kg_ref[...],
                            ck_ref[rows, :], snk_ref[rows, :], spk_ref[rows, :])
        v = proj(ATT_Q + ATT_KV, a0)
        k_scr[rows, :] = k
        v_scr[rows, :] = v
        k_ref[rows, :] = k.astype(BF16)
        v_ref[rows, :] = v.astype(BF16)
        glu = proj(a0, a0 + CONV_CH) * jax.nn.sigmoid(proj(a0 + CONV_CH, a0 + 2 * CONV_CH))
        glu_ref[rows, :] = glu.astype(BF16)

    @pl.when(i < NTOK_P // EVEN_IN_ROWS)
    def _():
        kc_ref[...] = k_scr[...]
        vc_ref[...] = v_scr[...]


def _rope_tables():
    t = np.arange(DEC_SEQ)
    row = (t // GRID_W).astype(np.float64)
    col = (t % GRID_W).astype(np.float64)
    inv_freq = 1.0 / (ROPE_THETA ** (np.arange(0, AXIS_ROPE_DIM, 2, dtype=np.float64) / AXIS_ROPE_DIM))
    ang = np.concatenate([row[:, None] * inv_freq, col[:, None] * inv_freq], axis=-1)
    ang = np.repeat(ang, 2, axis=-1)
    even = (np.arange(HEAD_DIM) % 2 == 0)[None, :]
    cos = np.cos(ang)
    sin_next = np.where(even, -np.sin(ang), 0.0)
    sin_prev = np.where(even, 0.0, np.sin(ang))

    def both(tab):
        ident = np.ones_like(tab) if tab is cos else np.zeros_like(tab)
        return np.stack([np.tile(ident, (1, N_HEADS)), np.tile(tab, (1, N_HEADS))]).astype(np.float32)

    return both(cos), both(sin_next), both(sin_prev)


def _even_in(x, mod_l, norm_g, ev_w_in, q_norm_g, k_norm_g, layer, j):
    cos, sin_next, sin_prev = (jnp.asarray(t) for t in _rope_tables())
    qg = jnp.tile(q_norm_g[j], N_HEADS).reshape(1, ATT_Q)
    kg = jnp.tile(k_norm_g[j], N_KV_HEADS).reshape(1, ATT_KV)
    rows = EVEN_IN_ROWS
    per_seq = DEC_SEQ // rows
    first_latent = NTOK_P // rows
    tab_idx = lambda i: (jnp.minimum(i // first_latent, 1), i % per_seq, 0)
    tab_q = pl.BlockSpec((None, rows, ATT_Q), tab_idx)
    tab_k = pl.BlockSpec((None, rows, ATT_KV), tab_idx)
    tok = lambda n: pl.BlockSpec((rows, n), lambda i: (i, 0))
    cache = pl.BlockSpec((rows, ATT_KV), lambda i: (jnp.minimum(i, first_latent - 1), 0))
    return pl.pallas_call(
        _even_in_kernel,
        grid=(NTOK // rows,),
        in_specs=[
            tok(D_MODEL),
            pl.BlockSpec((1, N_MOD, D_MODEL), lambda i: (_cond_of_tile(i, rows), 0, 0)),
            pl.BlockSpec((None, None, 1, D_MODEL), lambda i: (layer, 1, 0, 0)),
            pl.BlockSpec((None, D_MODEL, EVEN_IN), lambda i: (j, 0, 0), pipeline_mode=pl.Buffered(1)),
            pl.BlockSpec((1, ATT_Q), lambda i: (0, 0)),
            pl.BlockSpec((1, ATT_KV), lambda i: (0, 0)),
            tab_q, tab_q, tab_q, tab_k, tab_k, tab_k,
        ],
        out_specs=[tok(ATT_Q), tok(ATT_KV), tok(ATT_KV), tok(CONV_CH), cache, cache],
        out_shape=[
            jax.ShapeDtypeStruct((NTOK, ATT_Q), BF16),
            jax.ShapeDtypeStruct((NTOK, ATT_KV), BF16),
            jax.ShapeDtypeStruct((NTOK, ATT_KV), BF16),
            jax.ShapeDtypeStruct((NTOK, CONV_CH), BF16),
            jax.ShapeDtypeStruct((NTOK_P, ATT_KV), F32),
            jax.ShapeDtypeStruct((NTOK_P, ATT_KV), F32),
        ],
        scratch_shapes=[pltpu.VMEM((D_MODEL, EVEN_IN), BF16),
                        pltpu.VMEM((EVEN_IN_ROWS, ATT_KV), F32), pltpu.VMEM((EVEN_IN_ROWS, ATT_KV), F32)],
        compiler_params=_cparams("arbitrary"),
        name="even_in",
    )(x, mod_l, norm_g.reshape(DEPTH, 3, 1, D_MODEL), ev_w_in, qg, kg,
      cos, sin_next, sin_prev, cos, sin_next, sin_prev)


ATT_WIDTH = GQA_GROUP * HEAD_DIM


def _expand_kv_head(x, kv):
    lane = lax.broadcasted_iota(jnp.int32, x.shape, 1)
    swapped = pltpu.roll(x, HEAD_DIM, 1)
    pair = jnp.where((lane // HEAD_DIM) == kv, x, swapped)
    return jnp.concatenate([pair, pair], axis=1)


def _attend(q_ref, q_rows, kbig, vbig, key0, n_keys, attn_scr, out_rows):
    lane_group = lax.broadcasted_iota(jnp.int32, (ATT_TQ, ATT_WIDTH), 1) // HEAD_DIM
    for kv in range(N_KV_HEADS):
        cols = slice(kv * ATT_WIDTH, (kv + 1) * ATT_WIDTH)
        q = q_ref[q_rows, cols]
        kb = kbig[kv, key0:key0 + n_keys, :]
        vb = vbig[kv, key0:key0 + n_keys, :]
        acc = jnp.zeros((ATT_TQ, ATT_WIDTH), F32)
        for g in range(GQA_GROUP):
            mine = lane_group == g
            qm = jnp.where(mine, q, jnp.zeros_like(q))
            s = lax.dot_general(qm, kb, (((1,), (1,)), ((), ())), preferred_element_type=F32)
            m = jnp.max(s, axis=-1, keepdims=True)
            p = jnp.exp2(s - m)
            denom = jnp.sum(p, axis=-1, keepdims=True)
            o = _dot(p.astype(BF16), vb)
            acc = jnp.where(mine, o / denom, acc)
        attn_scr[out_rows, cols] = acc.astype(BF16)


MIX_BLOCKS = 2
CONV_PAD_SEQ = SEQ + 2 * CONV_HALO
CONV_WIN = ATT_TQ + 2 * CONV_HALO
CONV_SPAN = CONV_WIN - V7X_SUBLANES


def _conv_block(win0_scr, win_scr, w_ref, y_scr):
    off = CONV_HALO - CONV_K // 2
    for r0 in range(0, ATT_TQ, CONV_ROWS):
        for c0 in range(0, CONV_CH, V7X_LANES):
            cols = slice(c0, c0 + V7X_LANES)
            acc = jnp.zeros((CONV_ROWS, V7X_LANES), F32)
            for k in range(CONV_K):
                res = (off + k) % V7X_SUBLANES
                a = r0 + off + k - res
                if res == 0:
                    tap = win0_scr[a:a + CONV_ROWS, cols]
                else:
                    tap = win_scr[res - 1, a:a + CONV_ROWS, cols]
                acc = acc + tap * w_ref[k:k + 1, cols]
            y_scr[r0:r0 + CONV_ROWS, cols] = acc


def _even_mix_kernel(q_ref, k_ref, v_ref, ck_ref, cv_ref, glu_ref, w_ref, b_ref, g_ref, beta_ref,
                     res_ref, mod_ref, wo_ref, o_ref,
                     kbig, vbig, pad_scr, win0_scr, win_scr, y_scr, attn_scr, wo_scr):
    i = pl.program_id(0)
    t = pl.program_id(1)
    border = jnp.zeros((CONV_HALO, CONV_CH), F32)

    @pl.when((i == 0) & (t == 0))
    def _():
        wo_scr[...] = wo_ref[...].astype(BF16)

    def q_rows(b):
        return pl.ds(pl.multiple_of((t * MIX_BLOCKS + b) * ATT_TQ, ATT_TQ), ATT_TQ)

    def expand(k, v, key0, n):
        for kv in range(N_KV_HEADS):
            kbig[kv, key0:key0 + n, :] = _expand_kv_head(k, kv).astype(BF16)
            vbig[kv, key0:key0 + n, :] = _expand_kv_head(v, kv).astype(BF16)

    def mix(b, key0, n_keys, win_start):
        out_rows = slice(b * ATT_TQ, (b + 1) * ATT_TQ)
        win0, win, y = win0_scr.at[b], win_scr.at[b], y_scr.at[b]
        win0[...] = pad_scr[pl.ds(pl.multiple_of(win_start, V7X_SUBLANES), CONV_WIN), :]
        for res in range(1, V7X_SUBLANES):
            win[res - 1] = win0[res:res + CONV_SPAN, :]
        _attend(q_ref, q_rows(b), kbig, vbig, key0, n_keys, attn_scr, out_rows)
        _conv_block(win0, win, w_ref, y)
        yn = _layer_norm(y[...] + b_ref[...], g_ref[...], beta_ref[...])
        conv = (yn * jax.nn.sigmoid(yn)).astype(BF16)
        mixed = _dot(attn_scr[out_rows, :], wo_scr[:ATT_Q, :]) + _dot(conv, wo_scr[ATT_Q:, :])
        o_ref[out_rows, :] = res_ref[out_rows, :] + mod_ref[0, 5:6, :] * mixed

    @pl.when(i < N_TILES_P)
    def _():
        @pl.when(t == 0)
        def _():
            for s in range(SEQ_PER_TILE):
                base = s * CONV_PAD_SEQ
                pad_scr[base:base + CONV_HALO, :] = border
                pad_scr[base + CONV_HALO:base + CONV_HALO + SEQ, :] = (
                    glu_ref[s * SEQ:(s + 1) * SEQ, :].astype(F32))
                pad_scr[base + CONV_HALO + SEQ:base + CONV_PAD_SEQ, :] = border

        for b in range(MIX_BLOCKS):
            expand(k_ref[q_rows(b), :].astype(F32), v_ref[q_rows(b), :].astype(F32), b * SEQ, SEQ)
        for b in range(MIX_BLOCKS):
            mix(b, b * SEQ, SEQ, (t * MIX_BLOCKS + b) * CONV_PAD_SEQ)

    @pl.when(i >= N_TILES_P)
    def _():
        @pl.when(t == 0)
        def _():
            pad_scr[0:CONV_HALO, :] = border
            pad_scr[CONV_HALO:CONV_HALO + DEC_SEQ, :] = glu_ref[...].astype(F32)
            pad_scr[CONV_HALO + DEC_SEQ:2 * CONV_HALO + DEC_SEQ, :] = border
            expand(jnp.concatenate([ck_ref[...], k_ref[...].astype(F32)], axis=0),
                   jnp.concatenate([cv_ref[...], v_ref[...].astype(F32)], axis=0), 0, PAST_LEN + DEC_SEQ)

        for b in range(MIX_BLOCKS):
            mix(b, 0, PAST_LEN + DEC_SEQ, (t * MIX_BLOCKS + b) * ATT_TQ)


def _even_mix(x, mod_l, q, k, v, ck, cv, glu, conv_w, conv_b, cn_g, cn_b, w_out, j):
    rows = MIX_BLOCKS * ATT_TQ
    nq = TM // rows
    tok = lambda n: pl.BlockSpec((TM, n), lambda i, t: (i, 0))
    blk = pl.BlockSpec((rows, D_MODEL), lambda i, t: (i * nq + t, 0))
    ctx = pl.BlockSpec((None, None, PAST_LEN, ATT_KV), lambda i, t: (jnp.maximum(i - N_TILES_P, 0), j, 0, 0))
    vec = pl.BlockSpec((None, 1, CONV_CH), lambda i, t: (j, 0, 0))
    n_even = conv_b.shape[0]
    n_keys = PAST_LEN + DEC_SEQ
    return pl.pallas_call(
        _even_mix_kernel,
        grid=(N_TILES, nq),
        in_specs=[
            tok(ATT_Q), tok(ATT_KV), tok(ATT_KV), ctx, ctx, tok(CONV_CH),
            pl.BlockSpec((None, CONV_K, CONV_CH), lambda i, t: (j, 0, 0)),
            vec, vec, vec,
            blk,
            pl.BlockSpec((1, N_MOD, D_MODEL), lambda i, t: (_cond_of_tile(i), 0, 0)),
            pl.BlockSpec((None, ATT_Q + CONV_CH, D_MODEL), lambda i, t: (j, 0, 0),
                         pipeline_mode=pl.Buffered(1)),
        ],
        out_specs=blk,
        out_shape=jax.ShapeDtypeStruct((NTOK, D_MODEL), F32),
        scratch_shapes=[
            pltpu.VMEM((N_KV_HEADS, n_keys, ATT_WIDTH), BF16),
            pltpu.VMEM((N_KV_HEADS, n_keys, ATT_WIDTH), BF16),
            pltpu.VMEM((max(SEQ_PER_TILE * CONV_PAD_SEQ, DEC_SEQ + 2 * CONV_HALO), CONV_CH), F32),
            pltpu.VMEM((MIX_BLOCKS, CONV_WIN, CONV_CH), F32),
            pltpu.VMEM((MIX_BLOCKS, V7X_SUBLANES - 1, CONV_SPAN, CONV_CH), F32),
            pltpu.VMEM((MIX_BLOCKS, ATT_TQ, CONV_CH), F32),
            pltpu.VMEM((rows, ATT_Q), BF16),
            pltpu.VMEM((ATT_Q + CONV_CH, D_MODEL), BF16),
        ],
        compiler_params=_cparams("arbitrary", "arbitrary"),
        name="even_mix",
    )(q, k, v, ck, cv, glu, conv_w, conv_b.reshape(n_even, 1, CONV_CH),
      cn_g.reshape(n_even, 1, CONV_CH), cn_b.reshape(n_even, 1, CONV_CH), x, mod_l, w_out)


def _odd_in_kernel(x_ref, mod_ref, g_ref, w_ref, sg_ref, sb_ref, ws_ref, bs_ref, f_ref, sgu_ref, w_scr):
    @pl.when(pl.program_id(0) == 0)
    def _():
        w_scr[...] = w_ref[...].astype(BF16)

    ws = [ws_ref[g].astype(BF16) for g in range(SGU_GROUPS)]
    for r0 in range(0, TM, ODD_IN_PIECE):
        piece = slice(r0, r0 + ODD_IN_PIECE)
        h = _modulate(x_ref[piece, :], g_ref[...], mod_ref[0, 3:4, :], mod_ref[0, 4:5, :]).astype(BF16)

        def proj(lo, hi):
            return _dot(h, w_scr[:, lo:hi])

        f_ref[piece, :] = proj(0, FOURIER_CH).astype(BF16)
        u = proj(FOURIER_CH, FOURIER_CH + SGU_CH)
        v = _layer_norm(proj(FOURIER_CH + SGU_CH, ODD_IN), sg_ref[...], sb_ref[...]).astype(BF16)
        for g in range(SGU_GROUPS):
            bias = bs_ref[:, g:g + 1]
            cols = slice(g * SGU_GROUP_CH, (g + 1) * SGU_GROUP_CH)
            for n in range(ODD_IN_PIECE // CHUNK):
                rows = slice(n * CHUNK, (n + 1) * CHUNK)
                mixed = _dot(ws[g], v[rows, cols]) + bias
                sgu_ref[r0 + n * CHUNK:r0 + (n + 1) * CHUNK, cols] = (u[rows, cols] * mixed).astype(BF16)


def _odd_in(x, mod_l, norm_g, od_w_in, sgu_norm_g, sgu_norm_b, sgu_w, sgu_b, layer, j):
    tok = lambda n: pl.BlockSpec((TM, n), lambda i: (i, 0))
    return pl.pallas_call(
        _odd_in_kernel,
        grid=(N_TILES,),
        in_specs=[
            tok(D_MODEL),
            pl.BlockSpec((1, N_MOD, D_MODEL), lambda i: (_cond_of_tile(i), 0, 0)),
            pl.BlockSpec((None, None, 1, D_MODEL), lambda i: (layer, 1, 0, 0)),
            pl.BlockSpec((None, D_MODEL, ODD_IN), lambda i: (j, 0, 0), pipeline_mode=pl.Buffered(1)),
            pl.BlockSpec((1, SGU_CH), lambda i: (0, 0)),
            pl.BlockSpec((1, SGU_CH), lambda i: (0, 0)),
            pl.BlockSpec((None, SGU_GROUPS, CHUNK, CHUNK), lambda i: (j, 0, 0, 0)),
            pl.BlockSpec((CHUNK, SGU_GROUPS), lambda i: (0, 0)),
        ],
        out_specs=[tok(FOURIER_CH), tok(SGU_CH)],
        out_shape=[jax.ShapeDtypeStruct((NTOK, FOURIER_CH), BF16),
                   jax.ShapeDtypeStruct((NTOK, SGU_CH), BF16)],
        scratch_shapes=[pltpu.VMEM((D_MODEL, ODD_IN), BF16)],
        compiler_params=_cparams("arbitrary"),
        name="odd_in",
    )(x, mod_l, norm_g.reshape(DEPTH, 3, 1, D_MODEL), od_w_in,
      sgu_norm_g[j].reshape(1, SGU_CH), sgu_norm_b[j].reshape(1, SGU_CH), sgu_w, sgu_b[j].T)


def _fourier_out_kernel(f_ref, wch_ref, wseq_p_ref, wseq_s_ref, res_ref, mod_ref, sgu_ref, wo_ref,
                        o_ref, wch_scr, wseq_p_scr, wseq_s_scr, wo_scr):
    i = pl.program_id(0)

    @pl.when(i == 0)
    def _():
        wch_scr[...] = wch_ref[...].astype(BF16)
        wseq_p_scr[...] = wseq_p_ref[...].astype(BF16)
        wseq_s_scr[...] = wseq_s_ref[...].astype(BF16)
        wo_scr[...] = wo_ref[...].astype(BF16)

    gate = mod_ref[0, 5:6, :]

    def channel_dft(rows):
        pq = _dot(f_ref[rows, :], wch_scr[...]).astype(BF16)
        return jnp.concatenate([pq[:, :FOURIER_CH], pq[:, FOURIER_CH:]], axis=0)

    def finish(rows, wseq_rows, stacked, seq):
        scale = 1.0 / np.sqrt(float(seq * FOURIER_GROUP_CH))
        four = (_dot(wseq_rows, stacked) * scale).astype(BF16)
        mixed = _dot(four, wo_scr[:FOURIER_CH, :]) + _dot(sgu_ref[rows, :], wo_scr[FOURIER_CH:, :])
        o_ref[rows, :] = res_ref[rows, :] + gate * mixed

    @pl.when(i < N_TILES_P)
    def _():
        for s in range(SEQ_PER_TILE):
            rows = slice(s * SEQ, (s + 1) * SEQ)
            finish(rows, wseq_p_scr[...], channel_dft(rows), SEQ)

    @pl.when(i >= N_TILES_P)
    def _():
        stacked = channel_dft(slice(0, DEC_SEQ))
        for s in range(SEQ_PER_TILE):
            rows = slice(s * SEQ, (s + 1) * SEQ)
            finish(rows, wseq_s_scr[rows, :], stacked, DEC_SEQ)


def _dft_constants():
    def cos_sin(n):
        idx = np.arange(n)
        ang = 2.0 * np.pi * ((idx[:, None] * idx[None, :]) % n) / n
        return np.cos(ang), np.sin(ang)

    cc, sc = cos_sin(FOURIER_GROUP_CH)
    eye = np.eye(FOURIER_GROUPS)
    wch = np.concatenate([np.kron(eye, cc), np.kron(eye, sc)], axis=1)

    def wseq(seq):
        cl, sl = cos_sin(seq)
        return jnp.asarray(np.concatenate([cl, -sl], axis=1), dtype=F32)

    return jnp.asarray(wch, dtype=F32), wseq(SEQ), wseq(DEC_SEQ)


def _fourier_out(x, mod_l, f, sgu, w_out, j):
    wch, wseq_p, wseq_s = _dft_constants()
    tok = lambda n: pl.BlockSpec((TM, n), lambda i: (i, 0))
    const = lambda shape: pl.BlockSpec(shape, lambda i: (0,) * len(shape), pipeline_mode=pl.Buffered(1))
    return pl.pallas_call(
        _fourier_out_kernel,
        grid=(N_TILES,),
        in_specs=[
            tok(FOURIER_CH), const(wch.shape), const(wseq_p.shape), const(wseq_s.shape),
            tok(D_MODEL),
            pl.BlockSpec((1, N_MOD, D_MODEL), lambda i: (_cond_of_tile(i), 0, 0)),
            tok(SGU_CH),
            pl.BlockSpec((None, FOURIER_CH + SGU_CH, D_MODEL), lambda i: (j, 0, 0),
                         pipeline_mode=pl.Buffered(1)),
        ],
        out_specs=tok(D_MODEL),
        out_shape=jax.ShapeDtypeStruct((NTOK, D_MODEL), F32),
        scratch_shapes=[pltpu.VMEM(wch.shape, BF16),
                        pltpu.VMEM(wseq_p.shape, BF16), pltpu.VMEM(wseq_s.shape, BF16),
                        pltpu.VMEM((FOURIER_CH + SGU_CH, D_MODEL), BF16)],
        compiler_params=_cparams("arbitrary"),
        name="fourier_out",
    )(f, wch, wseq_p, wseq_s, x, mod_l, sgu, w_out)


def kernel(x_prompt, x_sample, cache_k, cache_v, c, c_ctx, w_mod, b_mod, norm_g, ffn_w_in, ffn_w_out, ev_w_in, ev_w_out, q_norm_g, k_norm_g, conv_w, conv_b, conv_norm_g, conv_norm_b, od_w_in, od_w_out, sgu_norm_g, sgu_norm_b, sgu_w, sgu_b):
    cond = jnp.concatenate([c_ctx[None, :], c_ctx[None, :], c,
                            jnp.zeros((N_COND - COND_LATENT0 - DEC_BATCH, D_MODEL), F32)], axis=0)
    mod = _adaln(cond, w_mod, b_mod).reshape(DEPTH, N_COND, N_MOD, D_MODEL)
    ck = cache_k.reshape(DEC_BATCH, -1, PAST_LEN, ATT_KV)
    cv = cache_v.reshape(DEC_BATCH, -1, PAST_LEN, ATT_KV)

    ffn = functools.partial(_ffn, mod=mod, norm_g=norm_g, ffn_w_in=ffn_w_in, ffn_w_out=ffn_w_out)
    x = ffn((x_prompt.reshape(NTOK_P, D_MODEL), x_sample.reshape(NTOK_S, D_MODEL)), phases=[(0, 0)])
    new_k, new_v = [], []
    for layer in range(DEPTH):
        mod_l = mod[layer]
        j = layer // 2
        if layer % 2 == 0:
            q, k, v, glu, kc, vc = _even_in(x, mod_l, norm_g, ev_w_in, q_norm_g, k_norm_g, layer, j)
            new_k.append(kc.reshape(BATCH, SEQ, N_KV_HEADS, HEAD_DIM))
            new_v.append(vc.reshape(BATCH, SEQ, N_KV_HEADS, HEAD_DIM))
            x = _even_mix(x, mod_l, q, k, v, ck, cv, glu, conv_w, conv_b, conv_norm_g, conv_norm_b,
                          ev_w_out, j)
        else:
            f, sgu = _odd_in(x, mod_l, norm_g, od_w_in, sgu_norm_g, sgu_norm_b, sgu_w, sgu_b, layer, j)
            x = _fourier_out(x, mod_l, f, sgu, od_w_out, j)
        if layer < DEPTH - 1:
            x = ffn((x,), phases=[(layer, 1), (layer + 1, 0)])

    last = [(DEPTH - 1, 1)]
    y_prompt = ffn((x,), phases=last, row0=0, n_rows=NTOK_P)
    y_sample = ffn((x,), phases=last, row0=NTOK_P, n_rows=NTOK_S)
    return (y_prompt.reshape(BATCH, SEQ, D_MODEL), y_sample.reshape(DEC_BATCH, DEC_SEQ, D_MODEL),
            jnp.stack(new_k, axis=1), jnp.stack(new_v, axis=1))
```

```python
import functools

import numpy as np
import jax
import jax.numpy as jnp
from jax import lax
from jax.experimental import pallas as pl
from jax.experimental.pallas import tpu as pltpu

D_MODEL = 1024
BATCH = 32
SEQ = 256
DEPTH = 4
DEC_BATCH = 4
DEC_SEQ = 1024
PAST_LEN = 512
GRID_W = 64
N_HEADS = 8
N_KV_HEADS = 2
HEAD_DIM = 64
GQA_GROUP = N_HEADS // N_KV_HEADS
AXIS_ROPE_DIM = HEAD_DIM // 2
ROPE_THETA = 10000.0
ATT_Q = N_HEADS * HEAD_DIM
ATT_KV = N_KV_HEADS * HEAD_DIM
CONV_CH = 512
CONV_K = 31
FOURIER_GROUPS = 4
FOURIER_GROUP_CH = 128
FOURIER_CH = FOURIER_GROUPS * FOURIER_GROUP_CH
SGU_GROUPS = 4
SGU_GROUP_CH = 128
SGU_CH = SGU_GROUPS * SGU_GROUP_CH
CHUNK = 128
FFN_DIM = 2816
N_MOD = 9
EVEN_IN = ATT_Q + 2 * ATT_KV + 2 * CONV_CH
ODD_IN = FOURIER_CH + 2 * SGU_CH
RMS_EPS = 1e-6
LN_EPS = 1e-5
LOG2_E = 1.4426950408889634

BF16 = jnp.bfloat16
F32 = jnp.float32

V7X_SUBLANES = 8
V7X_LANES = 128
NTOK_P = BATCH * SEQ
NTOK_S = DEC_BATCH * DEC_SEQ
NTOK = NTOK_P + NTOK_S
TM = DEC_SEQ
N_TILES = NTOK // TM
N_TILES_P = NTOK_P // TM
SEQ_PER_TILE = TM // SEQ
N_COND = 8
COND_LATENT0 = 2
FFN_TM = 2 * TM
FFN_PIECE = TM
FFN_CHUNK = 256
N_FFN_CHUNKS = FFN_DIM // FFN_CHUNK
EVEN_IN_ROWS = 1024
EVEN_IN_PIECE = 256
ODD_IN_PIECE = 1024
ADALN_TN = 2304
CONV_ROWS = 128
CONV_HALO = 16
ATT_TQ = SEQ
V7X_VMEM_LIMIT_BYTES = 56 * 1024 * 1024


def _cparams(*sem):
    return pltpu.CompilerParams(dimension_semantics=sem, vmem_limit_bytes=V7X_VMEM_LIMIT_BYTES)


def _cond_of_tile(i, rows=TM):
    return jnp.maximum((i * rows - NTOK_P) // DEC_SEQ + COND_LATENT0, 0)


def _dot(a, b):
    return jnp.dot(a, b, preferred_element_type=F32)


def _modulate(x, g, shift, scale):
    ms = jnp.mean(x * x, axis=-1, keepdims=True)
    return (x * lax.rsqrt(ms + RMS_EPS)) * (g * (1.0 + scale)) + shift


def _layer_norm(x, g, b):
    mu = jnp.mean(x, axis=-1, keepdims=True)
    xc = x - mu
    var = jnp.mean(xc * xc, axis=-1, keepdims=True)
    return xc * lax.rsqrt(var + LN_EPS) * g + b


def _adaln_kernel(c_ref, w_ref, b_ref, o_ref):
    c = c_ref[...]
    s = (c * jax.nn.sigmoid(c)).astype(BF16)
    o_ref[...] = _dot(s, w_ref[...].astype(BF16)) + b_ref[...]


def _adaln(cond, w_mod, b_mod):
    n_out = N_MOD * D_MODEL
    return pl.pallas_call(
        _adaln_kernel,
        grid=(DEPTH, n_out // ADALN_TN),
        in_specs=[
            pl.BlockSpec((N_COND, D_MODEL), lambda l, n: (0, 0)),
            pl.BlockSpec((None, D_MODEL, ADALN_TN), lambda l, n: (l, 0, n)),
            pl.BlockSpec((None, 1, ADALN_TN), lambda l, n: (l, 0, n)),
        ],
        out_specs=pl.BlockSpec((None, N_COND, ADALN_TN), lambda l, n: (l, 0, n)),
        out_shape=jax.ShapeDtypeStruct((DEPTH, N_COND, n_out), F32),
        compiler_params=_cparams("parallel", "parallel"),
        name="adaln",
    )(cond, w_mod, b_mod.reshape(DEPTH, 1, n_out))


def _ffn_kernel(subs, n_x, x_split, *refs):
    n_ph = len(subs)
    x_refs = refs[:n_x]
    mod_refs = refs[n_x:n_x + n_ph]
    g_refs = refs[n_x + n_ph:n_x + 2 * n_ph]
    wg_ref, wu_ref, wo_ref, o_ref, h_scr = refs[n_x + 2 * n_ph:]
    i = pl.program_id(0)
    j = pl.program_id(1)
    halves = [(r0 // TM, slice(r0, r0 + FFN_PIECE)) for r0 in range(0, FFN_TM, FFN_PIECE)]

    def chunk(h, rows, out_gate, base=None):
        gate = _dot(h, wg_ref[...].astype(BF16))
        up = _dot(h, wu_ref[...].astype(BF16))
        a = (gate * jax.nn.sigmoid(gate) * up).astype(BF16)
        delta = _dot(a, (wo_ref[...] * (0.5 * out_gate)).astype(BF16))
        if base is None:
            o_ref[rows, :] += delta
        else:
            o_ref[rows, :] = base + delta

    def start(p, src_ref):
        sub = subs[p]
        for c, rows in halves:
            res = src_ref[rows, :]
            h = _modulate(res, g_refs[p][...], mod_refs[p][c, 3 * sub:3 * sub + 1, :],
                          mod_refs[p][c, 3 * sub + 1:3 * sub + 2, :]).astype(BF16)
            h_scr[rows, :] = h
            chunk(h, rows, mod_refs[p][c, 3 * sub + 2:3 * sub + 3, :],
                  base=None if src_ref is o_ref else res)

    if n_x == 1:
        pl.when(j == 0)(functools.partial(start, 0, x_refs[0]))
    else:
        pl.when((j == 0) & (i < x_split))(functools.partial(start, 0, x_refs[0]))
        pl.when((j == 0) & (i >= x_split))(functools.partial(start, 0, x_refs[1]))
    for p in range(1, n_ph):
        pl.when(j == p * N_FFN_CHUNKS)(functools.partial(start, p, o_ref))

    @pl.when(j % N_FFN_CHUNKS != 0)
    def _():
        for c, rows in halves:
            out_gate = mod_refs[0][c, 3 * subs[0] + 2:3 * subs[0] + 3, :]
            if n_ph == 2:
                out_gate = jnp.where(j < N_FFN_CHUNKS, out_gate,
                                     mod_refs[1][c, 3 * subs[1] + 2:3 * subs[1] + 3, :])
            chunk(h_scr[rows, :], rows, out_gate)


def _ffn(xs, mod, norm_g, ffn_w_in, ffn_w_out, phases, row0=0, n_rows=NTOK):
    per = FFN_TM // TM
    tile0 = row0 // FFN_TM
    n_ph = len(phases)
    layers = [p[0] for p in phases]
    whichs = [p[1] for p in phases]

    def by_phase(vals):
        if n_ph == 1:
            return lambda j: vals[0]
        return lambda j: vals[0] + (j // N_FFN_CHUNKS) * (vals[1] - vals[0])

    layer_of, which_of = by_phase(layers), by_phase(whichs)
    chunk_of = lambda j: j % N_FFN_CHUNKS
    if len(xs) == 1:
        x_specs = [pl.BlockSpec((FFN_TM, D_MODEL), lambda i, j: (tile0 + i, 0))]
        x_split = 0
    else:
        x_split = xs[0].shape[0] // FFN_TM
        x_specs = [
            pl.BlockSpec((FFN_TM, D_MODEL), lambda i, j: (jnp.minimum(i, x_split - 1), 0)),
            pl.BlockSpec((FFN_TM, D_MODEL), lambda i, j: (jnp.maximum(i - x_split, 0), 0),
                         pipeline_mode=pl.Buffered(1)),
        ]
    mod_specs = [pl.BlockSpec((None, per, N_MOD, D_MODEL),
                              lambda i, j, l=l: (l, _cond_of_tile(tile0 + i, FFN_TM) // per, 0, 0))
                 for l in layers]
    g_specs = [pl.BlockSpec((None, None, 1, D_MODEL), lambda i, j, l=l, w=w: (l, 2 * w, 0, 0))
               for l, w in phases]
    return pl.pallas_call(
        functools.partial(_ffn_kernel, tuple(2 * w for w in whichs), len(xs), x_split),
        grid=(n_rows // FFN_TM, n_ph * N_FFN_CHUNKS),
        in_specs=x_specs + mod_specs + g_specs + [
            pl.BlockSpec((None, None, D_MODEL, FFN_CHUNK),
                         lambda i, j: (layer_of(j), which_of(j), 0, chunk_of(j))),
            pl.BlockSpec((None, None, D_MODEL, FFN_CHUNK),
                         lambda i, j: (layer_of(j), which_of(j), 0, N_FFN_CHUNKS + chunk_of(j))),
            pl.BlockSpec((None, None, FFN_CHUNK, D_MODEL),
                         lambda i, j: (layer_of(j), which_of(j), chunk_of(j), 0)),
        ],
        out_specs=pl.BlockSpec((FFN_TM, D_MODEL), lambda i, j: (i, 0)),
        out_shape=jax.ShapeDtypeStruct((n_rows, D_MODEL), F32),
        scratch_shapes=[pltpu.VMEM((FFN_TM, D_MODEL), BF16)],
        compiler_params=_cparams("parallel", "arbitrary"),
        name="ffn",
    )(*xs, *([mod] * n_ph), *([norm_g.reshape(DEPTH, 3, 1, D_MODEL)] * n_ph),
      ffn_w_in, ffn_w_in, ffn_w_out)


def _group_mean_sq(x, width):
    r = lax.broadcasted_iota(jnp.int32, (width, width), 0) // HEAD_DIM
    c = lax.broadcasted_iota(jnp.int32, (width, width), 1) // HEAD_DIM
    ones = (r == c).astype(BF16)
    return _dot((x * x).astype(BF16), ones) * (1.0 / HEAD_DIM)


def _head_norm_rope(x, gain, cos, sin_signed):
    width = x.shape[-1]
    xn = x * lax.rsqrt(_group_mean_sq(x, width) + RMS_EPS) * gain
    nxt = pltpu.roll(xn, width - 1, 1)
    prv = pltpu.roll(xn, 1, 1)
    even_lane = lax.broadcasted_iota(jnp.int32, xn.shape, 1) % 2 == 0
    return xn * cos + jnp.where(even_lane, nxt, prv) * sin_signed


def _even_in_kernel(x_ref, mod_ref, g_ref, w_ref, qg_ref, kg_ref,
                    cq_ref, sq_ref, ck_ref, sk_ref,
                    q_ref, k_ref, v_ref, glu_ref, kc_ref, vc_ref, w_scr, k_scr, v_scr):
    i = pl.program_id(0)

    @pl.when(i == 0)
    def _():
        w_scr[...] = w_ref[...].astype(BF16)

    a0 = ATT_Q + 2 * ATT_KV
    for r0 in range(0, EVEN_IN_ROWS, EVEN_IN_PIECE):
        rows = slice(r0, r0 + EVEN_IN_PIECE)
        h = _modulate(x_ref[rows, :], g_ref[...], mod_ref[0, 3:4, :], mod_ref[0, 4:5, :]).astype(BF16)

        def proj(lo, hi):
            return _dot(h, w_scr[:, lo:hi])

        q = _head_norm_rope(proj(0, ATT_Q), qg_ref[...], cq_ref[rows, :], sq_ref[rows, :])
        q_ref[rows, :] = (q * (HEAD_DIM ** -0.5 * LOG2_E)).astype(BF16)
        k = _head_norm_rope(proj(ATT_Q, ATT_Q + ATT_KV), kg_ref[...],
                            ck_ref[rows, :], sk_ref[rows, :])
        v = proj(ATT_Q + ATT_KV, a0)
        k_scr[rows, :] = k
        v_scr[rows, :] = v
        k_ref[rows, :] = k.astype(BF16)
        v_ref[rows, :] = v.astype(BF16)
        glu = proj(a0, a0 + CONV_CH) * jax.nn.sigmoid(proj(a0 + CONV_CH, a0 + 2 * CONV_CH))
        glu_ref[rows, :] = glu.astype(BF16)

    @pl.when(i < NTOK_P // EVEN_IN_ROWS)
    def _():
        kc_ref[...] = k_scr[...]
        vc_ref[...] = v_scr[...]


def _rope_tables():
    t = np.arange(DEC_SEQ)
    row = (t // GRID_W).astype(np.float64)
    col = (t % GRID_W).astype(np.float64)
    inv_freq = 1.0 / (ROPE_THETA ** (np.arange(0, AXIS_ROPE_DIM, 2, dtype=np.float64) / AXIS_ROPE_DIM))
    ang = np.concatenate([row[:, None] * inv_freq, col[:, None] * inv_freq], axis=-1)
    ang = np.repeat(ang, 2, axis=-1)
    even = (np.arange(HEAD_DIM) % 2 == 0)[None, :]
    cos = np.cos(ang)
    sin_signed = np.where(even, -np.sin(ang), np.sin(ang))

    def both(tab):
        ident = np.ones_like(tab) if tab is cos else np.zeros_like(tab)
        return np.stack([np.tile(ident, (1, N_HEADS)), np.tile(tab, (1, N_HEADS))]).astype(np.float32)

    return both(cos), both(sin_signed)


def _even_in(x, mod_l, norm_g, ev_w_in, q_norm_g, k_norm_g, layer, j):
    cos, sin_signed = (jnp.asarray(t) for t in _rope_tables())
    qg = jnp.tile(q_norm_g[j], N_HEADS).reshape(1, ATT_Q)
    kg = jnp.tile(k_norm_g[j], N_KV_HEADS).reshape(1, ATT_KV)
    rows = EVEN_IN_ROWS
    per_seq = DEC_SEQ // rows
    first_latent = NTOK_P // rows
    tab_idx = lambda i: (jnp.minimum(i // first_latent, 1), i % per_seq, 0)
    tab_q = pl.BlockSpec((None, rows, ATT_Q), tab_idx)
    tab_k = pl.BlockSpec((None, rows, ATT_KV), tab_idx)
    tok = lambda n: pl.BlockSpec((rows, n), lambda i: (i, 0))
    cache = pl.BlockSpec((rows, ATT_KV), lambda i: (jnp.minimum(i, first_latent - 1), 0))
    return pl.pallas_call(
        _even_in_kernel,
        grid=(NTOK // rows,),
        in_specs=[
            tok(D_MODEL),
            pl.BlockSpec((1, N_MOD, D_MODEL), lambda i: (_cond_of_tile(i, rows), 0, 0)),
            pl.BlockSpec((None, None, 1, D_MODEL), lambda i: (layer, 1, 0, 0)),
            pl.BlockSpec((None, D_MODEL, EVEN_IN), lambda i: (j, 0, 0), pipeline_mode=pl.Buffered(1)),
            pl.BlockSpec((1, ATT_Q), lambda i: (0, 0)),
            pl.BlockSpec((1, ATT_KV), lambda i: (0, 0)),
            tab_q, tab_q, tab_k, tab_k,
        ],
        out_specs=[tok(ATT_Q), tok(ATT_KV), tok(ATT_KV), tok(CONV_CH), cache, cache],
        out_shape=[
            jax.ShapeDtypeStruct((NTOK, ATT_Q), BF16),
            jax.ShapeDtypeStruct((NTOK, ATT_KV), BF16),
            jax.ShapeDtypeStruct((NTOK, ATT_KV), BF16),
            jax.ShapeDtypeStruct((NTOK, CONV_CH), BF16),
            jax.ShapeDtypeStruct((NTOK_P, ATT_KV), F32),
            jax.ShapeDtypeStruct((NTOK_P, ATT_KV), F32),
        ],
        scratch_shapes=[pltpu.VMEM((D_MODEL, EVEN_IN), BF16),
                        pltpu.VMEM((EVEN_IN_ROWS, ATT_KV), F32), pltpu.VMEM((EVEN_IN_ROWS, ATT_KV), F32)],
        compiler_params=_cparams("arbitrary"),
        name="even_in",
    )(x, mod_l, norm_g.reshape(DEPTH, 3, 1, D_MODEL), ev_w_in, qg, kg,
      cos, sin_signed, cos, sin_signed)


ATT_WIDTH = GQA_GROUP * HEAD_DIM


def _expand_kv_head(x, kv):
    lane = lax.broadcasted_iota(jnp.int32, x.shape, 1)
    swapped = pltpu.roll(x, HEAD_DIM, 1)
    pair = jnp.where((lane // HEAD_DIM) == kv, x, swapped)
    return jnp.concatenate([pair, pair], axis=1)


def _attend(q_ref, q_rows, kbig, vbig, key0, n_keys, attn_scr, out_rows):
    lane_group = lax.broadcasted_iota(jnp.int32, (ATT_TQ, ATT_WIDTH), 1) // HEAD_DIM
    for kv in range(N_KV_HEADS):
        cols = slice(kv * ATT_WIDTH, (kv + 1) * ATT_WIDTH)
        q = q_ref[q_rows, cols]
        kb = kbig[kv, key0:key0 + n_keys, :]
        vb = vbig[kv, key0:key0 + n_keys, :]
        acc = jnp.zeros((ATT_TQ, ATT_WIDTH), F32)
        for g in range(GQA_GROUP):
            mine = lane_group == g
            qm = jnp.where(mine, q, jnp.zeros_like(q))
            s = lax.dot_general(qm, kb, (((1,), (1,)), ((), ())), preferred_element_type=F32)
            m = jnp.max(s, axis=-1, keepdims=True)
            p = jnp.exp2(s - m)
            denom = jnp.sum(p, axis=-1, keepdims=True)
            o = _dot(p.astype(BF16), vb)
            acc = jnp.where(mine, o / denom, acc)
        attn_scr[out_rows, cols] = acc.astype(BF16)


MIX_BLOCKS = 2
CONV_PAD_SEQ = SEQ + 2 * CONV_HALO
CONV_WIN = ATT_TQ + 2 * CONV_HALO
CONV_SPAN = CONV_WIN - V7X_SUBLANES


def _conv_block(win0_scr, win_scr, w_ref, y_scr):
    off = CONV_HALO - CONV_K // 2
    for r0 in range(0, ATT_TQ, CONV_ROWS):
        for c0 in range(0, CONV_CH, V7X_LANES):
            cols = slice(c0, c0 + V7X_LANES)
            acc = jnp.zeros((CONV_ROWS, V7X_LANES), F32)
            for k in range(CONV_K):
                res = (off + k) % V7X_SUBLANES
                a = r0 + off + k - res
                if res == 0:
                    tap = win0_scr[a:a + CONV_ROWS, cols]
                else:
                    tap = win_scr[res - 1, a:a + CONV_ROWS, cols]
                acc = acc + tap * w_ref[k:k + 1, cols]
            y_scr[r0:r0 + CONV_ROWS, cols] = acc


def _even_mix_kernel(q_ref, k_ref, v_ref, ck_ref, cv_ref, glu_ref, w_ref, b_ref, g_ref, beta_ref,
                     res_ref, mod_ref, wo_ref, o_ref,
                     kbig, vbig, pad_scr, win0_scr, win_scr, y_scr, attn_scr, wo_scr):
    i = pl.program_id(0)
    t = pl.program_id(1)
    border = jnp.zeros((CONV_HALO, CONV_CH), F32)

    @pl.when((i == 0) & (t == 0))
    def _():
        wo_scr[...] = wo_ref[...].astype(BF16)

    def q_rows(b):
        return pl.ds(pl.multiple_of((t * MIX_BLOCKS + b) * ATT_TQ, ATT_TQ), ATT_TQ)

    def expand(k, v, key0, n):
        for kv in range(N_KV_HEADS):
            kbig[kv, key0:key0 + n, :] = _expand_kv_head(k, kv).astype(BF16)
            vbig[kv, key0:key0 + n, :] = _expand_kv_head(v, kv).astype(BF16)

    def mix(b, key0, n_keys, win_start):
        out_rows = slice(b * ATT_TQ, (b + 1) * ATT_TQ)
        win0, win, y = win0_scr.at[b], win_scr.at[b], y_scr.at[b]
        win0[...] = pad_scr[pl.ds(pl.multiple_of(win_start, V7X_SUBLANES), CONV_WIN), :]
        for res in range(1, V7X_SUBLANES):
            win[res - 1] = win0[res:res + CONV_SPAN, :]
        _attend(q_ref, q_rows(b), kbig, vbig, key0, n_keys, attn_scr, out_rows)
        _conv_block(win0, win, w_ref, y)
        yn = _layer_norm(y[...] + b_ref[...], g_ref[...], beta_ref[...])
        conv = (yn * jax.nn.sigmoid(yn)).astype(BF16)
        mixed = _dot(attn_scr[out_rows, :], wo_scr[:ATT_Q, :]) + _dot(conv, wo_scr[ATT_Q:, :])
        o_ref[out_rows, :] = res_ref[out_rows, :] + mod_ref[0, 5:6, :] * mixed

    @pl.when(i < N_TILES_P)
    def _():
        @pl.when(t == 0)
        def _():
            for s in range(SEQ_PER_TILE):
                base = s * CONV_PAD_SEQ
                pad_scr[base:base + CONV_HALO, :] = border
                pad_scr[base + CONV_HALO:base + CONV_HALO + SEQ, :] = (
                    glu_ref[s * SEQ:(s + 1) * SEQ, :].astype(F32))
                pad_scr[base + CONV_HALO + SEQ:base + CONV_PAD_SEQ, :] = border

        for b in range(MIX_BLOCKS):
            expand(k_ref[q_rows(b), :].astype(F32), v_ref[q_rows(b), :].astype(F32), b * SEQ, SEQ)
        for b in range(MIX_BLOCKS):
            mix(b, b * SEQ, SEQ, (t * MIX_BLOCKS + b) * CONV_PAD_SEQ)

    @pl.when(i >= N_TILES_P)
    def _():
        @pl.when(t == 0)
        def _():
            pad_scr[0:CONV_HALO, :] = border
            pad_scr[CONV_HALO:CONV_HALO + DEC_SEQ, :] = glu_ref[...].astype(F32)
            pad_scr[CONV_HALO + DEC_SEQ:2 * CONV_HALO + DEC_SEQ, :] = border
            expand(jnp.concatenate([ck_ref[...], k_ref[...].astype(F32)], axis=0),
                   jnp.concatenate([cv_ref[...], v_ref[...].astype(F32)], axis=0), 0, PAST_LEN + DEC_SEQ)

        for b in range(MIX_BLOCKS):
            mix(b, 0, PAST_LEN + DEC_SEQ, (t * MIX_BLOCKS + b) * ATT_TQ)


def _even_mix(x, mod_l, q, k, v, ck, cv, glu, conv_w, conv_b, cn_g, cn_b, w_out, j):
    rows = MIX_BLOCKS * ATT_TQ
    nq = TM // rows
    tok = lambda n: pl.BlockSpec((TM, n), lambda i, t: (i, 0))
    blk = pl.BlockSpec((rows, D_MODEL), lambda i, t: (i * nq + t, 0))
    ctx = pl.BlockSpec((None, None, PAST_LEN, ATT_KV), lambda i, t: (jnp.maximum(i - N_TILES_P, 0), j, 0, 0))
    vec = pl.BlockSpec((None, 1, CONV_CH), lambda i, t: (j, 0, 0))
    n_even = conv_b.shape[0]
    n_keys = PAST_LEN + DEC_SEQ
    return pl.pallas_call(
        _even_mix_kernel,
        grid=(N_TILES, nq),
        in_specs=[
            tok(ATT_Q), tok(ATT_KV), tok(ATT_KV), ctx, ctx, tok(CONV_CH),
            pl.BlockSpec((None, CONV_K, CONV_CH), lambda i, t: (j, 0, 0)),
            vec, vec, vec,
            blk,
            pl.BlockSpec((1, N_MOD, D_MODEL), lambda i, t: (_cond_of_tile(i), 0, 0)),
            pl.BlockSpec((None, ATT_Q + CONV_CH, D_MODEL), lambda i, t: (j, 0, 0),
                         pipeline_mode=pl.Buffered(1)),
        ],
        out_specs=blk,
        out_shape=jax.ShapeDtypeStruct((NTOK, D_MODEL), F32),
        scratch_shapes=[
            pltpu.VMEM((N_KV_HEADS, n_keys, ATT_WIDTH), BF16),
            pltpu.VMEM((N_KV_HEADS, n_keys, ATT_WIDTH), BF16),
            pltpu.VMEM((max(SEQ_PER_TILE * CONV_PAD_SEQ, DEC_SEQ + 2 * CONV_HALO), CONV_CH), F32),
            pltpu.VMEM((MIX_BLOCKS, CONV_WIN, CONV_CH), F32),
            pltpu.VMEM((MIX_BLOCKS, V7X_SUBLANES - 1, CONV_SPAN, CONV_CH), F32),
            pltpu.VMEM((MIX_BLOCKS, ATT_TQ, CONV_CH), F32),
            pltpu.VMEM((rows, ATT_Q), BF16),
            pltpu.VMEM((ATT_Q + CONV_CH, D_MODEL), BF16),
        ],
        compiler_params=_cparams("arbitrary", "arbitrary"),
        name="even_mix",
    )(q, k, v, ck, cv, glu, conv_w, conv_b.reshape(n_even, 1, CONV_CH),
      cn_g.reshape(n_even, 1, CONV_CH), cn_b.reshape(n_even, 1, CONV_CH), x, mod_l, w_out)


def _odd_in_kernel(x_ref, mod_ref, g_ref, w_ref, sg_ref, sb_ref, ws_ref, bs_ref, f_ref, sgu_ref, w_scr):
    @pl.when(pl.program_id(0) == 0)
    def _():
        w_scr[...] = w_ref[...].astype(BF16)

    ws = [ws_ref[g].astype(BF16) for g in range(SGU_GROUPS)]
    for r0 in range(0, TM, ODD_IN_PIECE):
        piece = slice(r0, r0 + ODD_IN_PIECE)
        h = _modulate(x_ref[piece, :], g_ref[...], mod_ref[0, 3:4, :], mod_ref[0, 4:5, :]).astype(BF16)

        def proj(lo, hi):
            return _dot(h, w_scr[:, lo:hi])

        f_ref[piece, :] = proj(0, FOURIER_CH).astype(BF16)
        u = proj(FOURIER_CH, FOURIER_CH + SGU_CH)
        v = _layer_norm(proj(FOURIER_CH + SGU_CH, ODD_IN), sg_ref[...], sb_ref[...]).astype(BF16)
        for g in range(SGU_GROUPS):
            bias = bs_ref[:, g:g + 1]
            cols = slice(g * SGU_GROUP_CH, (g + 1) * SGU_GROUP_CH)
            for n in range(ODD_IN_PIECE // CHUNK):
                rows = slice(n * CHUNK, (n + 1) * CHUNK)
                mixed = _dot(ws[g], v[rows, cols]) + bias
                sgu_ref[r0 + n * CHUNK:r0 + (n + 1) * CHUNK, cols] = (u[rows, cols] * mixed).astype(BF16)


def _odd_in(x, mod_l, norm_g, od_w_in, sgu_norm_g, sgu_norm_b, sgu_w, sgu_b, layer, j):
    tok = lambda n: pl.BlockSpec((TM, n), lambda i: (i, 0))
    return pl.pallas_call(
        _odd_in_kernel,
        grid=(N_TILES,),
        in_specs=[
            tok(D_MODEL),
            pl.BlockSpec((1, N_MOD, D_MODEL), lambda i: (_cond_of_tile(i), 0, 0)),
            pl.BlockSpec((None, None, 1, D_MODEL), lambda i: (layer, 1, 0, 0)),
            pl.BlockSpec((None, D_MODEL, ODD_IN), lambda i: (j, 0, 0), pipeline_mode=pl.Buffered(1)),
            pl.BlockSpec((1, SGU_CH), lambda i: (0, 0)),
            pl.BlockSpec((1, SGU_CH), lambda i: (0, 0)),
            pl.BlockSpec((None, SGU_GROUPS, CHUNK, CHUNK), lambda i: (j, 0, 0, 0)),
            pl.BlockSpec((CHUNK, SGU_GROUPS), lambda i: (0, 0)),
        ],
        out_specs=[tok(FOURIER_CH), tok(SGU_CH)],
        out_shape=[jax.ShapeDtypeStruct((NTOK, FOURIER_CH), BF16),
                   jax.ShapeDtypeStruct((NTOK, SGU_CH), BF16)],
        scratch_shapes=[pltpu.VMEM((D_MODEL, ODD_IN), BF16)],
        compiler_params=_cparams("arbitrary"),
        name="odd_in",
    )(x, mod_l, norm_g.reshape(DEPTH, 3, 1, D_MODEL), od_w_in,
      sgu_norm_g[j].reshape(1, SGU_CH), sgu_norm_b[j].reshape(1, SGU_CH), sgu_w, sgu_b[j].T)


def _fourier_out_kernel(f_ref, wch_ref, wseq_p_ref, wseq_s_ref, res_ref, mod_ref, sgu_ref, wo_ref,
                        o_ref, wch_scr, wseq_p_scr, wseq_s_scr, wo_scr):
    i = pl.program_id(0)

    @pl.when(i == 0)
    def _():
        wch_scr[...] = wch_ref[...].astype(BF16)
        wseq_p_scr[...] = wseq_p_ref[...].astype(BF16)
        wseq_s_scr[...] = wseq_s_ref[...].astype(BF16)
        wo_scr[...] = wo_ref[...].astype(BF16)

    gate = mod_ref[0, 5:6, :]

    def channel_dft(rows):
        pq = _dot(f_ref[rows, :], wch_scr[...]).astype(BF16)
        return jnp.concatenate([pq[:, :FOURIER_CH], pq[:, FOURIER_CH:]], axis=0)

    def finish(rows, wseq_rows, stacked, seq):
        scale = 1.0 / np.sqrt(float(seq * FOURIER_GROUP_CH))
        four = (_dot(wseq_rows, stacked) * scale).astype(BF16)
        mixed = _dot(four, wo_scr[:FOURIER_CH, :]) + _dot(sgu_ref[rows, :], wo_scr[FOURIER_CH:, :])
        o_ref[rows, :] = res_ref[rows, :] + gate * mixed

    @pl.when(i < N_TILES_P)
    def _():
        for s in range(SEQ_PER_TILE):
            rows = slice(s * SEQ, (s + 1) * SEQ)
            finish(rows, wseq_p_scr[...], channel_dft(rows), SEQ)

    @pl.when(i >= N_TILES_P)
    def _():
        stacked = channel_dft(slice(0, DEC_SEQ))
        for s in range(SEQ_PER_TILE):
            rows = slice(s * SEQ, (s + 1) * SEQ)
            finish(rows, wseq_s_scr[rows, :], stacked, DEC_SEQ)


def _dft_constants():
    def cos_sin(n):
        idx = np.arange(n)
        ang = 2.0 * np.pi * ((idx[:, None] * idx[None, :]) % n) / n
        return np.cos(ang), np.sin(ang)

    cc, sc = cos_sin(FOURIER_GROUP_CH)
    eye = np.eye(FOURIER_GROUPS)
    wch = np.concatenate([np.kron(eye, cc), np.kron(eye, sc)], axis=1)

    def wseq(seq):
        cl, sl = cos_sin(seq)
        return jnp.asarray(np.concatenate([cl, -sl], axis=1), dtype=F32)

    return jnp.asarray(wch, dtype=F32), wseq(SEQ), wseq(DEC_SEQ)


def _fourier_out(x, mod_l, f, sgu, w_out, j):
    wch, wseq_p, wseq_s = _dft_constants()
    tok = lambda n: pl.BlockSpec((TM, n), lambda i: (i, 0))
    const = lambda shape: pl.BlockSpec(shape, lambda i: (0,) * len(shape), pipeline_mode=pl.Buffered(1))
    return pl.pallas_call(
        _fourier_out_kernel,
        grid=(N_TILES,),
        in_specs=[
            tok(FOURIER_CH), const(wch.shape), const(wseq_p.shape), const(wseq_s.shape),
            tok(D_MODEL),
            pl.BlockSpec((1, N_MOD, D_MODEL), lambda i: (_cond_of_tile(i), 0, 0)),
            tok(SGU_CH),
            pl.BlockSpec((None, FOURIER_CH + SGU_CH, D_MODEL), lambda i: (j, 0, 0),
                         pipeline_mode=pl.Buffered(1)),
        ],
        out_specs=tok(D_MODEL),
        out_shape=jax.ShapeDtypeStruct((NTOK, D_MODEL), F32),
        scratch_shapes=[pltpu.VMEM(wch.shape, BF16),
                        pltpu.VMEM(wseq_p.shape, BF16), pltpu.VMEM(wseq_s.shape, BF16),
                        pltpu.VMEM((FOURIER_CH + SGU_CH, D_MODEL), BF16)],
        compiler_params=_cparams("arbitrary"),
        name="fourier_out",
    )(f, wch, wseq_p, wseq_s, x, mod_l, sgu, w_out)


def kernel(x_prompt, x_sample, cache_k, cache_v, c, c_ctx, w_mod, b_mod, norm_g, ffn_w_in, ffn_w_out, ev_w_in, ev_w_out, q_norm_g, k_norm_g, conv_w, conv_b, conv_norm_g, conv_norm_b, od_w_in, od_w_out, sgu_norm_g, sgu_norm_b, sgu_w, sgu_b):
    cond = jnp.concatenate([c_ctx[None, :], c_ctx[None, :], c,
                            jnp.zeros((N_COND - COND_LATENT0 - DEC_BATCH, D_MODEL), F32)], axis=0)
    mod = _adaln(cond, w_mod, b_mod).reshape(DEPTH, N_COND, N_MOD, D_MODEL)
    ck = cache_k.reshape(DEC_BATCH, -1, PAST_LEN, ATT_KV)
    cv = cache_v.reshape(DEC_BATCH, -1, PAST_LEN, ATT_KV)

    ffn = functools.partial(_ffn, mod=mod, norm_g=norm_g, ffn_w_in=ffn_w_in, ffn_w_out=ffn_w_out)
    x = ffn((x_prompt.reshape(NTOK_P, D_MODEL), x_sample.reshape(NTOK_S, D_MODEL)), phases=[(0, 0)])
    new_k, new_v = [], []
    for layer in range(DEPTH):
        mod_l = mod[layer]
        j = layer // 2
        if layer % 2 == 0:
            q, k, v, glu, kc, vc = _even_in(x, mod_l, norm_g, ev_w_in, q_norm_g, k_norm_g, layer, j)
            new_k.append(kc.reshape(BATCH, SEQ, N_KV_HEADS, HEAD_DIM))
            new_v.append(vc.reshape(BATCH, SEQ, N_KV_HEADS, HEAD_DIM))
            x = _even_mix(x, mod_l, q, k, v, ck, cv, glu, conv_w, conv_b, conv_norm_g, conv_norm_b,
                          ev_w_out, j)
        else:
            f, sgu = _odd_in(x, mod_l, norm_g, od_w_in, sgu_norm_g, sgu_norm_b, sgu_w, sgu_b, layer, j)
            x = _fourier_out(x, mod_l, f, sgu, od_w_out, j)
        if layer < DEPTH - 1:
            x = ffn((x,), phases=[(layer, 1), (layer + 1, 0)])

    last = [(DEPTH - 1, 1)]
    y_prompt = ffn((x,), phases=last, row0=0, n_rows=NTOK_P)
    y_sample = ffn((x,), phases=last, row0=NTOK_P, n_rows=NTOK_S)
    return (y_prompt.reshape(BATCH, SEQ, D_MODEL), y_sample.reshape(DEC_BATCH, DEC_SEQ, D_MODEL),
            jnp.stack(new_k, axis=1), jnp.stack(new_v, axis=1))
```
